```python
import jax
import jax.numpy as jnp
from jax import lax
import numpy as np

D_MODEL = 1024
BATCH = 4
SEQ = 8192
DEPTH = 2

HEAD_DIM = 64
D_MIX = D_MODEL
D_CONV = D_MIX // 4
D_RWKV = D_MIX // 4
D_NSA = D_MIX - D_CONV - D_RWKV
CONV_WIDTH = 31
CONV_GROUPS = D_CONV // HEAD_DIM
CONV_EPS = 1e-5
RWKV_HEADS = D_RWKV // HEAD_DIM
LORA_W = 64
LORA_A = 64
LORA_G = 128
RWKV_GN_EPS = 64e-5
NSA_HEADS = D_NSA // HEAD_DIM
NSA_KV_HEADS = 2
NSA_GROUP = NSA_HEADS // NSA_KV_HEADS
CMP_BLOCK = 32
CMP_STRIDE = 16
CMP_HIDDEN = 2 * HEAD_DIM
SEL_BLOCK = 64
N_SEL = 16
WINDOW = 512
Q_BLOCK = 128
FORCE_SCORE = 1e4
NEG_INF = -1e30
PEER_HEADS = 8
PEER_KEYS = 128
PEER_TOPK = 16
PEER_QDIM = 256
N_EXPERTS = PEER_KEYS * PEER_KEYS
PEER_CHUNK = 128
RMS_EPS = 1e-6
N_CONV_IN = 2 * D_CONV
N_RWKV_IN = 3 * D_RWKV + LORA_W + LORA_A + LORA_G
N_Q_IN = D_NSA
N_KV_IN = 6 * NSA_KV_HEADS * HEAD_DIM
N_GATE_IN = 3 * NSA_HEADS
N_IN = N_CONV_IN + N_RWKV_IN + N_Q_IN + N_KV_IN + N_GATE_IN

kernel_name = 'hymba_conv_rwkv7_nsa_peer_adaln'


def _rmsnorm(x, g):
    x32 = x.astype(jnp.float32)
    y = x32 * lax.rsqrt(jnp.mean(x32 * x32, axis=-1, keepdims=True) + RMS_EPS)
    return (y * g.astype(jnp.float32)).astype(x.dtype)


def _group_norm(y, n_groups, g, b, eps):
    shp = y.shape
    y32 = y.astype(jnp.float32).reshape(shp[:-1] + (n_groups, shp[-1] // n_groups))
    mu = jnp.mean(y32, axis=-1, keepdims=True)
    var = jnp.mean(jnp.square(y32 - mu), axis=-1, keepdims=True)
    yn = ((y32 - mu) * lax.rsqrt(var + eps)).reshape(shp)
    return (yn * g.astype(jnp.float32) + b.astype(jnp.float32)).astype(y.dtype)


def _conv_mixer(p, dw_w, dw_b, gn_g, gn_b):
    val, gate = jnp.split(p, 2, axis=-1)
    u = val * jax.nn.sigmoid(gate)
    y = lax.conv_general_dilated(u, dw_w.astype(u.dtype), window_strides=(1,),
                                 padding=((CONV_WIDTH - 1, 0),),
                                 dimension_numbers=('NWC', 'WIO', 'NWC'),
                                 feature_group_count=D_CONV) + dw_b
    y = _group_norm(y, CONV_GROUPS, gn_g, gn_b, CONV_EPS)
    return jax.nn.silu(y)


def _rwkv_mixer(p, mu, w0, w_up, a0, a_up, g_up, k_k, k_a, r_k, ln_g, ln_b):
    B, T, _ = p.shape
    f32 = jnp.float32
    prev = jnp.pad(p, ((0, 0), (1, 0), (0, 0)))[:, :-1]
    p = p + (prev - p) * mu
    r, k, v, xw, xa, xg = jnp.split(
        p, [D_RWKV, 2 * D_RWKV, 3 * D_RWKV, 3 * D_RWKV + LORA_W, 3 * D_RWKV + LORA_W + LORA_A], axis=-1)
    w_log = -jax.nn.softplus(-(w0 + jnp.tanh(xw) @ w_up).astype(f32)) - 0.5
    decay = jnp.exp(-jnp.exp(w_log))
    a = jax.nn.sigmoid((a0 + xa @ a_up).astype(f32))
    g = (jax.nn.sigmoid(xg) @ g_up).astype(f32)

    def heads(z):
        return z.astype(f32).reshape(B, T, RWKV_HEADS, HEAD_DIM)

    kk = heads(k * k_k)
    kk = kk / jnp.maximum(jnp.linalg.norm(kk, axis=-1, keepdims=True), 1e-12)
    k_eff = k.astype(f32) * (1.0 + (a - 1.0) * k_a.astype(f32))
    r_h, k_h, v_h, w_h, a_h = heads(r), heads(k_eff), heads(v), heads(decay), heads(a)

    def step(S, inp):
        r_t, w_t, k_t, v_t, kk_t, a_t = inp
        sa = jnp.einsum('bhvk,bhk->bhv', S, -kk_t)
        S = (S * w_t[:, :, None, :] + sa[..., None] * (kk_t * a_t)[:, :, None, :]
             + v_t[..., None] * k_t[:, :, None, :])
        return S, jnp.einsum('bhvk,bhk->bhv', S, r_t)

    def tm(z):
        return z.transpose(1, 0, 2, 3)

    S0 = jnp.zeros((B, RWKV_HEADS, HEAD_DIM, HEAD_DIM), f32)
    _, ys = lax.scan(step, S0, (tm(r_h), tm(w_h), tm(k_h), tm(v_h), tm(kk), tm(a_h)))
    y = ys.transpose(1, 0, 2, 3).reshape(B, T, D_RWKV)
    y = _group_norm(y, RWKV_HEADS, ln_g, ln_b, RWKV_GN_EPS)
    bonus = jnp.sum(r_h * k_h * r_k.astype(f32), axis=-1, keepdims=True) * v_h
    y = (y + bonus.reshape(B, T, D_RWKV)) * g
    return y.astype(p.dtype)


def _nsa_mixer(p_q, p_kv, p_gate, ck_pos, ck_w1, ck_w2, cv_pos, cv_w1, cv_w2):
    B, T, _ = p_q.shape
    G, R, dk = NSA_KV_HEADS, NSA_GROUP, HEAD_DIM
    f32 = jnp.float32
    scale = HEAD_DIM ** -0.5

    def kv_heads(z):
        return z.reshape(B, T, G, dk).transpose(0, 2, 1, 3)

    k_cmp, v_cmp, k_sel, v_sel, k_win, v_win = [kv_heads(z) for z in jnp.split(p_kv, 6, axis=-1)]

    n_c = (T - CMP_BLOCK) // CMP_STRIDE + 1
    cmp_idx = jnp.arange(n_c)[:, None] * CMP_STRIDE + jnp.arange(CMP_BLOCK)[None, :]

    def compress(z, pos, w1, w2):
        blk = (z[:, :, cmp_idx] + pos).reshape(B, G, n_c, CMP_BLOCK * dk)
        return jax.nn.silu(blk @ w1) @ w2

    kc = compress(k_cmp, ck_pos, ck_w1, ck_w2)
    vc = compress(v_cmp, cv_pos, cv_w1, cv_w2)
    cmp_end = cmp_idx[:, -1]

    n_s = T // SEL_BLOCK
    n_sel = min(N_SEL, n_s)
    ks = k_sel.reshape(B, G, n_s, SEL_BLOCK, dk)
    vs = v_sel.reshape(B, G, n_s, SEL_BLOCK, dk)
    cmp_start = jnp.arange(n_c) * CMP_STRIDE
    sel_start = jnp.arange(n_s) * SEL_BLOCK
    overlap = ((cmp_start[:, None] < sel_start[None, :] + SEL_BLOCK)
               & (cmp_start[:, None] + CMP_BLOCK > sel_start[None, :])).astype(f32)

    kw = jnp.pad(k_win, ((0, 0), (0, 0), (WINDOW, 0), (0, 0)))
    vw = jnp.pad(v_win, ((0, 0), (0, 0), (WINDOW, 0), (0, 0)))

    n_qb = T // Q_BLOCK
    q_all = p_q.reshape(B, n_qb, Q_BLOCK, G, R, dk).transpose(1, 0, 3, 4, 2, 5)
    g_all = jax.nn.sigmoid(p_gate.astype(f32)).reshape(B, n_qb, Q_BLOCK, 3, G, R).transpose(1, 0, 4, 5, 2, 3)
    b_ix = jnp.arange(B)[:, None, None, None]
    g_ix = jnp.arange(G)[None, :, None, None]

    def block(args):
        qi, q, gate = args
        t = qi * Q_BLOCK + jnp.arange(Q_BLOCK)
        valid_c = cmp_end[None, :] <= t[:, None]
        s_c = jnp.einsum('bgrqd,bgnd->bgrqn', q, kc).astype(f32) * scale
        p_c = jax.nn.softmax(jnp.where(valid_c, s_c, NEG_INF), axis=-1) * jnp.any(valid_c, axis=-1)[:, None]
        o_c = jnp.einsum('bgrqn,bgnd->bgrqd', p_c.astype(vc.dtype), vc)
        imp = jnp.einsum('bgrqn,ns->bgqs', p_c, overlap)
        blk_j = jnp.arange(n_s)[None, :]
        cur = (t // SEL_BLOCK)[:, None]
        forced = (blk_j == 0) | (blk_j == cur) | (blk_j == cur - 1)
        imp = jnp.where(forced, FORCE_SCORE, jnp.where(blk_j <= cur, imp, -1.0))
        _, sel = lax.top_k(imp, n_sel)
        k_g = ks[b_ix, g_ix, sel]
        v_g = vs[b_ix, g_ix, sel]
        key_pos = sel[..., None] * SEL_BLOCK + jnp.arange(SEL_BLOCK)
        valid_s = (key_pos <= t[None, None, :, None, None]).reshape(B, G, Q_BLOCK, n_sel * SEL_BLOCK)[:, :, None]
        s_s = jnp.einsum('bgrqd,bgqnld->bgrqnl', q, k_g).astype(f32).reshape(
            B, G, R, Q_BLOCK, n_sel * SEL_BLOCK) * scale
        p_s = jax.nn.softmax(jnp.where(valid_s, s_s, NEG_INF), axis=-1).reshape(
            B, G, R, Q_BLOCK, n_sel, SEL_BLOCK)
        o_s = jnp.einsum('bgrqnl,bgqnld->bgrqd', p_s.astype(v_g.dtype), v_g)
        start = qi * Q_BLOCK
        k_w = lax.dynamic_slice_in_dim(kw, start, WINDOW + Q_BLOCK, axis=2)
        v_w = lax.dynamic_slice_in_dim(vw, start, WINDOW + Q_BLOCK, axis=2)
        key_w = start - WINDOW + jnp.arange(WINDOW + Q_BLOCK)
        dist = t[:, None] - key_w[None, :]
        valid_w = (dist >= 0) & (dist < WINDOW) & (key_w[None, :] >= 0)
        s_w = jnp.einsum('bgrqd,bgkd->bgrqk', q, k_w).astype(f32) * scale
        p_w = jax.nn.softmax(jnp.where(valid_w, s_w, NEG_INF), axis=-1)
        o_w = jnp.einsum('bgrqk,bgkd->bgrqd', p_w.astype(v_w.dtype), v_w)
        o = gate[..., 0, None] * o_c + gate[..., 1, None] * o_s + gate[..., 2, None] * o_w
        return o.astype(q.dtype)

    out = lax.map(block, (jnp.arange(n_qb), q_all, g_all))
    return out.transpose(1, 0, 4, 2, 3, 5).reshape(B, T, D_NSA)


def _peer(h, wq, k1, k2, u, v):
    B, T, D = h.shape
    half = PEER_QDIM // 2
    tok = h.reshape(-1, PEER_CHUNK, D)

    def chunk(hc):
        q = (hc @ wq).reshape(PEER_CHUNK, PEER_HEADS, PEER_QDIM)
        s1 = jnp.einsum('chd,nd->chn', q[..., :half], k1).astype(jnp.float32)
        s2 = jnp.einsum('chd,nd->chn', q[..., half:], k2).astype(jnp.float32)
        v1, i1 = lax.top_k(s1, PEER_TOPK)
        v2, i2 = lax.top_k(s2, PEER_TOPK)
        cand = (v1[..., :, None] + v2[..., None, :]).reshape(PEER_CHUNK, PEER_HEADS, PEER_TOPK * PEER_TOPK)
        sc, ci = lax.top_k(cand, PEER_TOPK)
        e = (jnp.take_along_axis(i1, ci // PEER_TOPK, axis=-1) * PEER_KEYS
             + jnp.take_along_axis(i2, ci % PEER_TOPK, axis=-1))
        gw = jax.nn.softmax(sc, axis=-1)
        hid = jnp.einsum('chkd,cd->chk', u[e], hc).astype(jnp.float32)
        act = (jax.nn.gelu(hid, approximate=False) * gw).astype(hc.dtype)
        return jnp.einsum('chk,chkd->cd', act, v[e])

    return lax.map(chunk, tok).reshape(B, T, D)


def setup_inputs(seed: int = 0) -> dict:
    key = jax.random.key(seed)
    keys = iter(jax.random.split(key, 64))
    L, D = DEPTH, D_MODEL

    def nrm(shape, s):
        return jax.random.normal(next(keys), shape, jnp.float32) * s

    return {
        'x': nrm((BATCH, SEQ, D), 1.0),
        'c': nrm((BATCH, D), 1.0),
        'ada_w': nrm((L, D, 6 * D), 0.5 * D ** -0.5),
        'ada_b': nrm((L, 6 * D), 0.01),
        'norm1_g': 1.0 + nrm((L, D), 0.02),
        'norm2_g': 1.0 + nrm((L, D), 0.02),
        'w_in': nrm((L, D, N_IN), D ** -0.5),
        'conv_dw_w': nrm((L, CONV_WIDTH, 1, D_CONV), CONV_WIDTH ** -0.5),
        'conv_dw_b': nrm((L, D_CONV), 0.01),
        'conv_gn_g': 1.0 + nrm((L, D_CONV), 0.02),
        'conv_gn_b': nrm((L, D_CONV), 0.01),
        'rwkv_mu': jax.random.uniform(next(keys), (L, N_RWKV_IN), jnp.float32),
        'rwkv_w0': nrm((L, D_RWKV), 1.0) - 1.0,
        'rwkv_w_up': nrm((L, LORA_W, D_RWKV), LORA_W ** -0.5),
        'rwkv_a0': nrm((L, D_RWKV), 0.5),
        'rwkv_a_up': nrm((L, LORA_A, D_RWKV), LORA_A ** -0.5),
        'rwkv_g_up': nrm((L, LORA_G, D_RWKV), LORA_G ** -0.5),
        'rwkv_k_k': 0.85 + nrm((L, D_RWKV), 0.1),
        'rwkv_k_a': 1.0 + nrm((L, D_RWKV), 0.1),
        'rwkv_r_k': nrm((L, RWKV_HEADS, HEAD_DIM), 0.1),
        'rwkv_ln_g': 1.0 + nrm((L, D_RWKV), 0.02),
        'rwkv_ln_b': nrm((L, D_RWKV), 0.01),
        'nsa_ck_pos': nrm((L, CMP_BLOCK, HEAD_DIM), 0.1),
        'nsa_ck_w1': nrm((L, CMP_BLOCK * HEAD_DIM, CMP_HIDDEN), (CMP_BLOCK * HEAD_DIM) ** -0.5),
        'nsa_ck_w2': nrm((L, CMP_HIDDEN, HEAD_DIM), CMP_HIDDEN ** -0.5),
        'nsa_cv_pos': nrm((L, CMP_BLOCK, HEAD_DIM), 0.1),
        'nsa_cv_w1': nrm((L, CMP_BLOCK * HEAD_DIM, CMP_HIDDEN), (CMP_BLOCK * HEAD_DIM) ** -0.5),
        'nsa_cv_w2': nrm((L, CMP_HIDDEN, HEAD_DIM), CMP_HIDDEN ** -0.5),
        'w_out': nrm((L, D_MIX, D), D_MIX ** -0.5),
        'peer_wq': nrm((L, D, PEER_HEADS * PEER_QDIM), D ** -0.5),
        'peer_k1': nrm((L, PEER_KEYS, PEER_QDIM // 2), (PEER_QDIM // 2) ** -0.5),
        'peer_k2': nrm((L, PEER_KEYS, PEER_QDIM // 2), (PEER_QDIM // 2) ** -0.5),
        'peer_u': nrm((L, N_EXPERTS, D), D ** -0.5),
        'peer_v': nrm((L, N_EXPERTS, D), 1.0),
        'final_g': 1.0 + nrm((D,), 0.02),
    }


def reference(x, c, ada_w, ada_b, norm1_g, norm2_g, w_in, conv_dw_w, conv_dw_b, conv_gn_g, conv_gn_b,
              rwkv_mu, rwkv_w0, rwkv_w_up, rwkv_a0, rwkv_a_up, rwkv_g_up, rwkv_k_k, rwkv_k_a, rwkv_r_k,
              rwkv_ln_g, rwkv_ln_b, nsa_ck_pos, nsa_ck_w1, nsa_ck_w2, nsa_cv_pos, nsa_cv_w1, nsa_cv_w2,
              w_out, peer_wq, peer_k1, peer_k2, peer_u, peer_v, final_g):
    cs = jax.nn.silu(c)
    splits = [N_CONV_IN, N_CONV_IN + N_RWKV_IN, N_CONV_IN + N_RWKV_IN + N_Q_IN,
              N_CONV_IN + N_RWKV_IN + N_Q_IN + N_KV_IN]
    for l in range(DEPTH):
        mod = cs @ ada_w[l] + ada_b[l]
        sh1, sc1, g1, sh2, sc2, g2 = [m[:, None, :] for m in jnp.split(mod, 6, axis=-1)]
        h = _rmsnorm(x, norm1_g[l]) * (1.0 + sc1) + sh1
        p_conv, p_rwkv, p_q, p_kv, p_gate = jnp.split(h @ w_in[l], splits, axis=-1)
        o_conv = _conv_mixer(p_conv, conv_dw_w[l], conv_dw_b[l], conv_gn_g[l], conv_gn_b[l])
        o_rwkv = _rwkv_mixer(p_rwkv, rwkv_mu[l], rwkv_w0[l], rwkv_w_up[l], rwkv_a0[l], rwkv_a_up[l],
                             rwkv_g_up[l], rwkv_k_k[l], rwkv_k_a[l], rwkv_r_k[l], rwkv_ln_g[l], rwkv_ln_b[l])
        o_nsa = _nsa_mixer(p_q, p_kv, p_gate, nsa_ck_pos[l], nsa_ck_w1[l], nsa_ck_w2[l],
                           nsa_cv_pos[l], nsa_cv_w1[l], nsa_cv_w2[l])
        mix = jnp.concatenate([o_conv.astype(x.dtype), o_rwkv.astype(x.dtype), o_nsa.astype(x.dtype)], axis=-1)
        x = x + g1 * (mix @ w_out[l])
        h2 = _rmsnorm(x, norm2_g[l]) * (1.0 + sc2) + sh2
        x = x + g2 * _peer(h2, peer_wq[l], peer_k1[l], peer_k2[l], peer_u[l], peer_v[l])
    return _rmsnorm(x, final_g)
```

```python
import functools

import jax
import jax.numpy as jnp
import numpy as np
from jax import lax
from jax.experimental import pallas as pl
from jax.experimental.pallas import tpu as pltpu

F32 = jnp.float32
BF16 = jnp.bfloat16
HIGHEST = lax.Precision.HIGHEST

HEAD_DIM = 64
CONV_WIDTH = 31
CONV_EPS = 1e-5
RWKV_GN_EPS = 64e-5
RMS_EPS = 1e-6
LORA_W = 64
LORA_A = 64
LORA_G = 128
NSA_KV_HEADS = 2
NSA_GROUP = 4
CMP_BLOCK = 32
CMP_STRIDE = 16
CMP_HIDDEN = 128
SEL_BLOCK = 64
N_SEL = 16
WINDOW = 512
FORCE_SCORE = 1e4
NEG_INF = -1e30
PEER_HEADS = 8
PEER_KEYS = 128
PEER_TOPK = 16

LANES = 128
VMEM_LIMIT = 56 * 1024 * 1024


def _dot_hi(a, b):
    return jnp.dot(a, b, precision=HIGHEST, preferred_element_type=F32)


def _dot_bf(a, b):
    return jnp.dot(a.astype(BF16), b.astype(BF16), preferred_element_type=F32)


def _cparams(*sem):
    return pltpu.CompilerParams(dimension_semantics=sem, vmem_limit_bytes=VMEM_LIMIT)


def _group_avg_matrix(n, group):
    i = np.arange(n)
    return jnp.asarray((i[:, None] // group == i[None, :] // group).astype(np.float32) / group)


def _ada_kernel(c_ref, w_ref, b_ref, o_ref):
    c = c_ref[...]
    cs = c * jax.nn.sigmoid(c)
    o_ref[0] = _dot_hi(cs, w_ref[0]) + b_ref[0]


def ada_mod(c, ada_w, ada_b):
    L, D, N = ada_w.shape
    B = c.shape[0]
    bp = 8
    cp = jnp.zeros((bp, D), F32).at[:B].set(c)
    tn = N // 4
    out = pl.pallas_call(
        _ada_kernel,
        grid=(L, N // tn),
        in_specs=[pl.BlockSpec((bp, D), lambda l, j: (0, 0)),
                  pl.BlockSpec((1, D, tn), lambda l, j: (l, 0, j)),
                  pl.BlockSpec((1, 1, tn), lambda l, j: (l, 0, j))],
        out_specs=pl.BlockSpec((1, bp, tn), lambda l, j: (l, 0, j)),
        out_shape=jax.ShapeDtypeStruct((L, bp, N), F32),
        compiler_params=_cparams("parallel", "parallel"),
        name="ada_mod",
    )(cp, ada_w, ada_b.reshape(L, 1, N))
    return out[:, :B]


def _modulated_norm(x, g, sc, sh):
    ms = jnp.mean(x * x, axis=-1, keepdims=True)
    return x * lax.rsqrt(ms + RMS_EPS) * g * (1.0 + sc) + sh


def _proj_in_kernel(x_ref, sh_ref, sc_ref, g_ref, *refs):
    n = len(refs) // 2
    h = _modulated_norm(x_ref[0], g_ref[...], sc_ref[0], sh_ref[0]).astype(BF16)
    for w_ref, o_ref in zip(refs[:n], refs[n:]):
        o_ref[0] = jnp.dot(h, w_ref[...], preferred_element_type=F32)


def proj_in(x, sh, sc, g, weights, tm=512):
    B, T, D = x.shape
    tm = min(tm, T)
    vec = pl.BlockSpec((1, 1, D), lambda b, i: (b, 0, 0))
    in_specs = [pl.BlockSpec((1, tm, D), lambda b, i: (b, i, 0)), vec, vec,
                pl.BlockSpec((1, D), lambda b, i: (0, 0))]
    in_specs += [pl.BlockSpec(w.shape, lambda b, i: (0, 0)) for w in weights]
    out_specs = [pl.BlockSpec((1, tm, w.shape[1]), lambda b, i: (b, i, 0)) for w in weights]
    out_shape = [jax.ShapeDtypeStruct((B, T, w.shape[1]), F32) for w in weights]
    return pl.pallas_call(
        _proj_in_kernel,
        grid=(B, T // tm),
        in_specs=in_specs, out_specs=out_specs, out_shape=out_shape,
        compiler_params=_cparams("parallel", "parallel"),
        name="proj_in",
    )(x, sh.reshape(B, 1, D), sc.reshape(B, 1, D), g.reshape(1, D), *weights)


CONV_HALO = 32


def _conv_kernel(p_ref, w_ref, b_ref, gg_ref, gb_ref, m_ref, o_ref, ext_ref):
    i = pl.program_id(1)
    tt, dc = o_ref.shape[1], o_ref.shape[2]

    @pl.when(i == 0)
    def _():
        ext_ref[0:CONV_HALO, :] = jnp.zeros((CONV_HALO, dc), F32)

    @pl.when(i > 0)
    def _():
        ext_ref[0:CONV_HALO, :] = ext_ref[tt:tt + CONV_HALO, :]

    p = p_ref[0]
    ext_ref[CONV_HALO:CONV_HALO + tt, :] = p[:, :dc] * jax.nn.sigmoid(p[:, dc:])
    off = CONV_HALO - (CONV_WIDTH - 1)
    acc = jnp.zeros((tt, dc), F32) + b_ref[...]
    for j in range(CONV_WIDTH):
        acc = acc + ext_ref[off + j:off + j + tt, :] * w_ref[j:j + 1, :]
    m = m_ref[...]
    mu = _dot_hi(acc, m)
    d = acc - mu
    var = _dot_hi(d * d, m)
    y = d * lax.rsqrt(var + CONV_EPS) * gg_ref[...] + gb_ref[...]
    o_ref[0] = y * jax.nn.sigmoid(y)


def conv_mixer(p, dw_w, dw_b, gn_g, gn_b, tt=512):
    B, T, two_dc = p.shape
    dc = two_dc // 2
    tt = min(tt, T)
    wpad = jnp.zeros((32, dc), F32).at[:CONV_WIDTH].set(dw_w.reshape(CONV_WIDTH, dc))
    row = pl.BlockSpec((1, dc), lambda b, i: (0, 0))
    return pl.pallas_call(
        _conv_kernel,
        grid=(B, T // tt),
        in_specs=[pl.BlockSpec((1, tt, two_dc), lambda b, i: (b, i, 0)),
                  pl.BlockSpec((32, dc), lambda b, i: (0, 0)), row, row, row,
                  pl.BlockSpec((dc, dc), lambda b, i: (0, 0))],
        out_specs=pl.BlockSpec((1, tt, dc), lambda b, i: (b, i, 0)),
        out_shape=jax.ShapeDtypeStruct((B, T, dc), F32),
        scratch_shapes=[pltpu.VMEM((tt + CONV_HALO, dc), F32)],
        compiler_params=_cparams("parallel", "arbitrary"),
        name="conv_mixer",
    )(p, wpad, dw_b.reshape(1, dc), gn_g.reshape(1, dc), gn_b.reshape(1, dc),
      _group_avg_matrix(dc, HEAD_DIM))


def _group_sum_matrix(n, group):
    i = np.arange(n)
    return jnp.asarray((i[:, None] // group == i[None, :] // group).astype(np.float32))


def _rwkv_pre_kernel(p_ref, halo_ref, mu_ref, w0_ref, wup_ref, a0_ref, aup_ref, gup_ref, kk_ref, ka_ref,
                     rk_ref, ones_ref, w_o, kk_o, b_o, k_o, v_o, r_o, bonus_o, g_o, ext_ref):
    i = pl.program_id(1)
    tt = p_ref.shape[1]
    dr = w_o.shape[2]
    p = p_ref[0]
    first = (i > 0).astype(F32)
    ext_ref[0:8, :] = halo_ref[0] * first
    ext_ref[8:8 + tt, :] = p
    prev = ext_ref[7:7 + tt, :]
    xs = p + (prev - p) * mu_ref[...]
    r = xs[:, 0:dr]
    k = xs[:, dr:2 * dr]
    v = xs[:, 2 * dr:3 * dr]
    o = 3 * dr
    xw = xs[:, o:o + LORA_W]
    xa = xs[:, o + LORA_W:o + LORA_W + LORA_A]
    xg = xs[:, o + LORA_W + LORA_A:o + LORA_W + LORA_A + LORA_G]
    z = w0_ref[...] + _dot_hi(jnp.tanh(xw), wup_ref[...])
    w_log = -jax.nn.softplus(-z) - 0.5
    decay = jnp.exp(-jnp.exp(w_log))
    a = jax.nn.sigmoid(a0_ref[...] + _dot_hi(xa, aup_ref[...]))
    g = _dot_hi(jax.nn.sigmoid(xg), gup_ref[...])
    ones = ones_ref[...]
    kk = k * kk_ref[...]
    nrm = jnp.sqrt(_dot_hi(kk * kk, ones))
    kk = kk / jnp.maximum(nrm, 1e-12)
    k_eff = k * (1.0 + (a - 1.0) * ka_ref[...])
    bonus = _dot_hi(r * k_eff * rk_ref[...], ones) * v
    w_o[0] = decay
    kk_o[0] = kk
    b_o[0] = kk * a
    k_o[0] = k_eff
    v_o[0] = v
    r_o[0] = r
    bonus_o[0] = bonus
    g_o[0] = g


def rwkv_pre(p, mu, w0, w_up, a0, a_up, g_up, k_k, k_a, r_k, tt=512):
    B, T, n_in = p.shape
    dr = w0.shape[0]
    tt = min(tt, T)
    nb8 = tt // 8
    full = lambda a: pl.BlockSpec(a.shape, lambda b, i: (0,) * a.ndim)
    args = [mu.reshape(1, n_in), w0.reshape(1, dr), w_up, a0.reshape(1, dr), a_up, g_up,
            k_k.reshape(1, dr), k_a.reshape(1, dr), r_k.reshape(1, dr), _group_sum_matrix(dr, HEAD_DIM)]
    out = jax.ShapeDtypeStruct((B, T, dr), F32)
    ospec = pl.BlockSpec((1, tt, dr), lambda b, i: (b, i, 0))
    return pl.pallas_call(
        _rwkv_pre_kernel,
        grid=(B, T // tt),
        in_specs=[pl.BlockSpec((1, tt, n_in), lambda b, i: (b, i, 0)),
                  pl.BlockSpec((1, 8, n_in), lambda b, i: (b, jnp.maximum(i * nb8 - 1, 0), 0))]
                 + [full(a) for a in args],
        out_specs=[ospec] * 8, out_shape=[out] * 8,
        scratch_shapes=[pltpu.VMEM((tt + 8, n_in), F32)],
        compiler_params=_cparams("parallel", "parallel"),
        name="rwkv_pre",
    )(p, p, *args)


SCAN_SUB = 64
SCAN_CHUNK = 128


def _scan_select_matrices():
    e = np.zeros((SCAN_SUB, 4 * SCAN_SUB, LANES), np.float32)
    for t in range(SCAN_SUB):
        e[t, t, :HEAD_DIM] = 1.0
        e[t, SCAN_SUB + t, HEAD_DIM:] = 1.0
        e[t, 2 * SCAN_SUB + t, :HEAD_DIM] = 1.0
        e[t, 3 * SCAN_SUB + t, HEAD_DIM:] = 1.0
    return jnp.asarray(e, BF16)


def _rwkv_scan_kernel(w_ref, kk_ref, b_ref, k_ref, r_ref, v_ref, bonus_ref, g_ref, lng_ref, lnb_ref,
                      e_ref, avg_ref, o_ref, st_ref, lhs_ref, y_ref):
    nb, tc, dr = v_ref.shape
    n_half = dr // LANES
    n_pair = nb * n_half
    n_sub = tc // SCAN_SUB
    quantities = (w_ref, kk_ref, b_ref, k_ref, r_ref)

    @pl.when(pl.program_id(0) == 0)
    def _():
        st_ref[...] = jnp.zeros(st_ref.shape, F32)

    for qi, q_ref in enumerate(quantities):
        for p in range(n_pair):
            b, hb = divmod(p, n_half)
            xt = q_ref[b, :, hb * LANES:(hb + 1) * LANES].T
            for s in range(n_sub):
                cat = jnp.concatenate([xt[0:HEAD_DIM, s * SCAN_SUB:(s + 1) * SCAN_SUB],
                                       xt[HEAD_DIM:, s * SCAN_SUB:(s + 1) * SCAN_SUB]], axis=1)
                hi = cat.astype(BF16)
                lo = (cat - hi.astype(F32)).astype(BF16)
                row = (qi * n_pair + p) * HEAD_DIM
                lhs_ref[s, row:row + HEAD_DIM, :] = jnp.concatenate([hi, lo], axis=1)

    sub_iota = lax.broadcasted_iota(jnp.int32, (8, LANES), 0)
    for s in range(n_sub):
        def steps8(t8, carry, s=s):
            base = pl.multiple_of(s * SCAN_SUB + t8 * 8, 8)
            y8 = [jnp.zeros((8, LANES), F32) for _ in range(n_pair)]
            for j in range(8):
                z = jnp.dot(lhs_ref[s], e_ref[t8 * 8 + j], preferred_element_type=F32)
                for p in range(n_pair):
                    b, hb = divmod(p, n_half)
                    col = lambda qi: z[(qi * n_pair + p) * HEAD_DIM:(qi * n_pair + p + 1) * HEAD_DIM, :]
                    st = st_ref[p]
                    sa = jnp.sum(st * col(1), axis=0, keepdims=True)
                    v_row = v_ref[b, pl.ds(base, 8), hb * LANES:(hb + 1) * LANES][j:j + 1, :]
                    st = st * col(0) - col(2) * sa + col(3) * v_row
                    y = jnp.sum(st * col(4), axis=0, keepdims=True)
                    y8[p] = jnp.where(sub_iota == j, y, y8[p])
                    st_ref[p] = st
            for p in range(n_pair):
                b, hb = divmod(p, n_half)
                y_ref[b, pl.ds(base, 8), hb * LANES:(hb + 1) * LANES] = y8[p]
            return carry
        lax.fori_loop(0, SCAN_SUB // 8, steps8, 0)

    avg = avg_ref[...]
    for b in range(nb):
        y = y_ref[b]
        mu = _dot_hi(y, avg)
        d = y - mu
        var = _dot_hi(d * d, avg)
        yn = d * lax.rsqrt(var + RWKV_GN_EPS) * lng_ref[...] + lnb_ref[...]
        o_ref[b] = (yn + bonus_ref[b]) * g_ref[b]


def rwkv_scan(w, kk, bq, k, r, v, bonus, g, ln_g, ln_b):
    B, T, dr = v.shape
    tc = min(SCAN_CHUNK, T)
    n_pair = B * dr // LANES
    blk = pl.BlockSpec((B, tc, dr), lambda i: (0, i, 0))
    row = pl.BlockSpec((1, dr), lambda i: (0, 0))
    e = _scan_select_matrices()
    return pl.pallas_call(
        _rwkv_scan_kernel,
        grid=(T // tc,),
        in_specs=[blk] * 8 + [row, row, pl.BlockSpec(e.shape, lambda i: (0, 0, 0)),
                              pl.BlockSpec((dr, dr), lambda i: (0, 0))],
        out_specs=blk,
        out_shape=jax.ShapeDtypeStruct((B, T, dr), F32),
        scratch_shapes=[pltpu.VMEM((n_pair, HEAD_DIM, LANES), F32),
                        pltpu.VMEM((tc // SCAN_SUB, 5 * n_pair * HEAD_DIM, 4 * SCAN_SUB), BF16),
                        pltpu.VMEM((B, tc, dr), F32)],
        compiler_params=_cparams("arbitrary"),
        name="rwkv_scan",
    )(w, kk, bq, k, r, v, bonus, g, ln_g.reshape(1, dr), ln_b.reshape(1, dr), e, _group_avg_matrix(dr, HEAD_DIM))


def rwkv_mixer(p, mu, w0, w_up, a0, a_up, g_up, k_k, k_a, r_k, ln_g, ln_b):
    w, kk, bq, k, v, r, bonus, g = rwkv_pre(p, mu, w0, w_up, a0, a_up, g_up, k_k, k_a, r_k.reshape(-1))
    return rwkv_scan(w, kk, bq, k, r, v, bonus, g, ln_g, ln_b)


def _dot_nt(a, b, **kw):
    return lax.dot_general(a, b, (((1,), (1,)), ((), ())), preferred_element_type=F32, **kw)


def _compress_kernel(kz_ref, vz_ref, kpos_ref, kw1_ref, kw2_ref, vpos_ref, vw1_ref, vw2_ref,
                     kc_ref, vc_ref, shift_ref):
    n = kz_ref.shape[2]
    half = kz_ref.shape[3]

    def one(z_ref, pos_ref, w1_ref, w2_ref, o_ref):
        z = z_ref[0, 0]
        top = _dot_bf(z + pos_ref[0:1, :], w1_ref[0:half, :])
        bot = _dot_bf(z + pos_ref[1:2, :], w1_ref[half:2 * half, :])
        shift_ref[0:n, :] = bot
        shift_ref[n:n + 8, :] = jnp.zeros((8, bot.shape[1]), F32)
        pre = top + shift_ref[1:n + 1, :]
        hid = pre * jax.nn.sigmoid(pre)
        o_ref[0, 0] = _dot_bf(hid, w2_ref[...])

    one(kz_ref, kpos_ref, kw1_ref, kw2_ref, kc_ref)
    one(vz_ref, vpos_ref, vw1_ref, vw2_ref, vc_ref)


def nsa_compress(k_cmp, v_cmp, ck_pos, ck_w1, ck_w2, cv_pos, cv_w1, cv_w2):
    B, G, T, dk = k_cmp.shape
    n = T // CMP_STRIDE
    half = CMP_STRIDE * dk
    zspec = pl.BlockSpec((1, 1, n, half), lambda b, g: (b, g, 0, 0))
    full = lambda a: pl.BlockSpec(a.shape, lambda b, g: (0,) * a.ndim)
    ospec = pl.BlockSpec((1, 1, n, dk), lambda b, g: (b, g, 0, 0))
    args = [ck_pos.reshape(2, half), ck_w1.astype(BF16), ck_w2.astype(BF16),
            cv_pos.reshape(2, half), cv_w1.astype(BF16), cv_w2.astype(BF16)]
    return pl.pallas_call(
        _compress_kernel,
        grid=(B, G),
        in_specs=[zspec, zspec] + [full(a) for a in args],
        out_specs=[ospec, ospec],
        out_shape=[jax.ShapeDtypeStruct((B, G, n, dk), F32)] * 2,
        scratch_shapes=[pltpu.VMEM((n + 8, CMP_HIDDEN), F32)],
        compiler_params=_cparams("parallel", "parallel"),
        name="nsa_compress",
    )(k_cmp.reshape(B, G, n, half), v_cmp.reshape(B, G, n, half), *args)


NSA_TQ = 128
NSA_TK = 512


def _nsa_kernel(q_ref, gate_ref, kc_ref, vc_ref, ks_ref, vs_ref, kw_ref, vw_ref, ov_ref, ex_ref,
                o_ref, m_ref, l_ref, acc_ref):
    qi = pl.program_id(2)
    R, tq, dk = q_ref.shape[2], q_ref.shape[3], q_ref.shape[4]
    n_c = kc_ref.shape[2]
    n_blk = ov_ref.shape[1]
    T = ks_ref.shape[2]
    tk = ex_ref.shape[2]
    t0 = qi * tq
    q4 = q_ref[0, 0].reshape(R * tq, dk)
    t_col = t0 + lax.broadcasted_iota(jnp.int32, (tq, 1), 0)

    kc = kc_ref[0, 0]
    vc = vc_ref[0, 0]
    s = _dot_nt(q4, kc, precision=HIGHEST).reshape(R, tq, n_c)
    c_end = lax.broadcasted_iota(jnp.int32, (tq, n_c), 1) * CMP_STRIDE + (CMP_BLOCK - 1)
    valid_c = (c_end <= t_col)[None]
    s = jnp.where(valid_c, s, NEG_INF)
    e = jnp.exp(s - jnp.max(s, axis=-1, keepdims=True))
    any_c = (t_col >= CMP_BLOCK - 1).astype(F32)[None]
    p_c = e / jnp.sum(e, axis=-1, keepdims=True) * any_c
    o_c = _dot_bf(p_c.reshape(R * tq, n_c), vc)

    imp = _dot_hi(jnp.sum(p_c, axis=0), ov_ref[...])
    blk = lax.broadcasted_iota(jnp.int32, (tq, n_blk), 1)
    cur = t_col // SEL_BLOCK
    forced = (blk == 0) | (blk == cur) | (blk == cur - 1)
    x = jnp.where(forced, FORCE_SCORE, jnp.where(blk <= cur, imp, -1.0))
    x = jnp.where(blk < T // SEL_BLOCK, x, -3e38)
    sel = jnp.zeros((tq, n_blk), F32)
    blk_f = blk.astype(F32)
    for _ in range(N_SEL):
        mx = jnp.max(x, axis=-1, keepdims=True)
        idx = jnp.min(jnp.where(x == mx, blk_f, float(n_blk)), axis=-1, keepdims=True)
        hit = blk_f == idx
        sel = jnp.where(hit, 1.0, sel)
        x = jnp.where(hit, -jnp.inf, x)
    sel_bf = sel.astype(BF16)

    m_ref[...] = jnp.full(m_ref.shape, NEG_INF, F32)
    l_ref[...] = jnp.zeros(l_ref.shape, F32)
    acc_ref[...] = jnp.zeros(acc_ref.shape, F32)
    q4b = q4.astype(BF16)
    n_tiles = (t0 + tq - 1) // tk + 1

    def sel_tile(kj, carry):
        off = pl.multiple_of(kj * tk, tk)
        k_t = ks_ref[0, 0, pl.ds(off, tk), :]
        v_t = vs_ref[0, 0, pl.ds(off, tk), :]
        picked = jnp.dot(sel_bf, ex_ref[kj], preferred_element_type=F32)
        key_pos = off + lax.broadcasted_iota(jnp.int32, (tq, tk), 1)
        bias = jnp.where((picked > 0.5) & (key_pos <= t_col), 0.0, NEG_INF)
        s = _dot_nt(q4b, k_t).reshape(R, tq, tk) + bias[None]
        m_old = m_ref[...]
        m_new = jnp.maximum(m_old, jnp.max(s, axis=-1, keepdims=True))
        alpha = jnp.exp(m_old - m_new)
        p = jnp.exp(s - m_new)
        l_ref[...] = l_ref[...] * alpha + jnp.sum(p, axis=-1, keepdims=True)
        pv = jnp.dot(p.reshape(R * tq, tk).astype(BF16), v_t, preferred_element_type=F32)
        acc_ref[...] = acc_ref[...] * alpha + pv.reshape(R, tq, dk)
        m_ref[...] = m_new
        return carry

    lax.fori_loop(0, n_tiles, sel_tile, 0)
    o_s = acc_ref[...] / l_ref[...]

    span = WINDOW + tq
    w0 = pl.multiple_of(jnp.maximum(t0 - WINDOW, 0), tq)
    k_w = kw_ref[0, 0, pl.ds(w0, span), :]
    v_w = vw_ref[0, 0, pl.ds(w0, span), :]
    dist = t_col - (w0 + lax.broadcasted_iota(jnp.int32, (tq, span), 1))
    bias_w = jnp.where((dist >= 0) & (dist < WINDOW), 0.0, NEG_INF)
    s = _dot_nt(q4b, k_w).reshape(R, tq, span) + bias_w[None]
    e = jnp.exp(s - jnp.max(s, axis=-1, keepdims=True))
    p_w = e / jnp.sum(e, axis=-1, keepdims=True)
    o_w = jnp.dot(p_w.reshape(R * tq, span).astype(BF16), v_w, preferred_element_type=F32).reshape(R, tq, dk)

    gate = jax.nn.sigmoid(gate_ref[0, 0])
    o_c = o_c.reshape(R, tq, dk)
    outs = []
    for r in range(R):
        outs.append(gate[:, r:r + 1] * o_c[r] + gate[:, R + r:R + r + 1] * o_s[r]
                    + gate[:, 2 * R + r:2 * R + r + 1] * o_w[r])
    o_ref[0] = jnp.concatenate(outs, axis=-1)


def _overlap_matrix(n_c, n_blk):
    c0 = np.arange(n_c)[:, None] * CMP_STRIDE
    s0 = np.arange(n_blk)[None, :] * SEL_BLOCK
    return jnp.asarray(((c0 < s0 + SEL_BLOCK) & (c0 + CMP_BLOCK > s0)).astype(np.float32))


def _block_expand_matrices(T, n_blk, tk):
    n_t = T // tk
    blk_of_key = (np.arange(n_t)[:, None] * tk + np.arange(tk)[None, :]) // SEL_BLOCK
    ex = (np.arange(n_blk)[None, :, None] == blk_of_key[:, None, :]).astype(np.float32)
    return jnp.asarray(ex, BF16)


def nsa_attention(q, gate, kc, vc, k_sel, v_sel, k_win, v_win):
    B, G, R, T, dk = q.shape
    n_c = kc.shape[2]
    tq = min(NSA_TQ, T)
    tk = min(NSA_TK, T)
    n_blk = max(T // SEL_BLOCK, LANES)
    ov = _overlap_matrix(n_c, n_blk)
    ex = _block_expand_matrices(T, n_blk, tk)
    res = lambda a: pl.BlockSpec((1, 1) + a.shape[2:], lambda b, g, i: (b, g, 0, 0))
    return pl.pallas_call(
        _nsa_kernel,
        grid=(B, G, T // tq),
        in_specs=[pl.BlockSpec((1, 1, R, tq, dk), lambda b, g, i: (b, g, 0, i, 0)),
                  pl.BlockSpec((1, 1, tq, gate.shape[3]), lambda b, g, i: (b, g, i, 0)),
                  res(kc), res(vc), res(k_sel), res(v_sel), res(k_win), res(v_win),
                  pl.BlockSpec(ov.shape, lambda b, g, i: (0, 0)),
                  pl.BlockSpec(ex.shape, lambda b, g, i: (0, 0, 0))],
        out_specs=pl.BlockSpec((1, tq, R * dk), lambda b, g, i: (b, i, g)),
        out_shape=jax.ShapeDtypeStruct((B, T, G * R * dk), F32),
        scratch_shapes=[pltpu.VMEM((R, tq, 1), F32), pltpu.VMEM((R, tq, 1), F32), pltpu.VMEM((R, tq, dk), F32)],
        compiler_params=_cparams("parallel", "parallel", "arbitrary"),
        name="nsa_attention",
    )(q, gate, kc, vc, k_sel, v_sel, k_win, v_win, ov, ex)


def nsa_mixer(p_q, p_kv, p_gate, ck_pos, ck_w1, ck_w2, cv_pos, cv_w1, cv_w2):
    B, T, _ = p_q.shape
    G, R, dk = NSA_KV_HEADS, NSA_GROUP, HEAD_DIM
    kv = p_kv.reshape(B, T, 6, G, dk).transpose(2, 0, 3, 1, 4)
    kc, vc = nsa_compress(kv[0], kv[1], ck_pos, ck_w1, ck_w2, cv_pos, cv_w1, cv_w2)
    q = (p_q * (dk ** -0.5)).reshape(B, T, G, R, dk).transpose(0, 2, 3, 1, 4)
    gate = p_gate[..., :3 * G * R].reshape(B, T, 3, G, R).transpose(0, 3, 1, 2, 4).reshape(B, G, T, 3 * R)
    gate = jnp.pad(gate, ((0, 0), (0, 0), (0, 0), (0, LANES - 3 * R)))
    kvb = kv[2:].astype(BF16)
    return nsa_attention(q, gate, kc, vc, kvb[0], kvb[1], kvb[2], kvb[3])


def _proj_out_kernel(x_ref, g1_ref, sh_ref, sc_ref, ng_ref, *refs):
    n = (len(refs) - 2) // 2
    x1_ref, h2_ref = refs[2 * n:]
    acc = None
    for m_ref, w_ref in zip(refs[:n], refs[n:2 * n]):
        d = jnp.dot(m_ref[0].astype(BF16), w_ref[...], preferred_element_type=F32)
        acc = d if acc is None else acc + d
    x1 = x_ref[0] + g1_ref[0] * acc
    x1_ref[0] = x1
    h2_ref[0] = _modulated_norm(x1, ng_ref[...], sc_ref[0], sh_ref[0]).astype(BF16)


def proj_out(x, g1, sh2, sc2, norm_g, mixes, weights, tm=512):
    B, T, D = x.shape
    tm = min(tm, T)
    vec = pl.BlockSpec((1, 1, D), lambda b, i: (b, 0, 0))
    tile = lambda n: pl.BlockSpec((1, tm, n), lambda b, i: (b, i, 0))
    return pl.pallas_call(
        _proj_out_kernel,
        grid=(B, T // tm),
        in_specs=[tile(D), vec, vec, vec, pl.BlockSpec((1, D), lambda b, i: (0, 0))]
                 + [tile(m.shape[2]) for m in mixes]
                 + [pl.BlockSpec(w.shape, lambda b, i: (0, 0)) for w in weights],
        out_specs=[tile(D), tile(D)],
        out_shape=[jax.ShapeDtypeStruct((B, T, D), F32), jax.ShapeDtypeStruct((B, T, D), BF16)],
        compiler_params=_cparams("parallel", "parallel"),
        name="proj_out",
    )(x, g1.reshape(B, 1, D), sh2.reshape(B, 1, D), sc2.reshape(B, 1, D), norm_g.reshape(1, D), *mixes, *weights)


_CAND_BLOCKS = [(0, 16), (1, 8), (None, 8)] + [(i, 8) for i in range(2, 8)]
_CAND_LIMIT = {2: 5, 3: 4, 4: 3, 5: 2, 6: 2, 7: 2}


def _cand_constants(tm):
    flat, neg = [], []
    for i, rows in _CAND_BLOCKS:
        for r in range(rows):
            if i is None:
                flat.append((8 + r) * PEER_TOPK)
                neg.append(0.0)
            else:
                flat.append(i * PEER_TOPK + r)
                neg.append(0.0 if r < _CAND_LIMIT.get(i, rows) else -np.inf)
    flat = np.tile(np.asarray(flat, np.float32)[:, None], (1, tm))
    neg = np.tile(np.asarray(neg, np.float32)[:, None], (1, tm))
    return jnp.asarray(flat), jnp.asarray(neg)


def _topk_rows(x, row_id, k, n_rows):
    vals, idxs = [], []
    for _ in range(k):
        mx = jnp.max(x, axis=0, keepdims=True)
        idx = jnp.min(jnp.where(x == mx, row_id, float(n_rows)), axis=0, keepdims=True)
        x = jnp.where(row_id == idx, -jnp.inf, x)
        vals.append(mx)
        idxs.append(idx)
    return jnp.concatenate(vals, axis=0), jnp.concatenate(idxs, axis=0)


def _peer_select_kernel(h_ref, wq_ref, k1_ref, k2_ref, flat_ref, neg_ref, a_ref, b_ref, g_ref):
    tm = h_ref.shape[0]
    nk = k1_ref.shape[0]
    half = k1_ref.shape[1]
    q = jnp.dot(h_ref[...], wq_ref[...], preferred_element_type=F32)
    key_id = lax.broadcasted_iota(jnp.int32, (nk, tm), 0).astype(F32)
    flat = flat_ref[...]
    neg = neg_ref[...]
    k1 = k1_ref[...]
    k2 = k2_ref[...]
    a_all, b_all, g_all = [], [], []
    for h in range(PEER_HEADS):
        q1 = q[:, (2 * h) * half:(2 * h + 1) * half]
        q2 = q[:, (2 * h + 1) * half:(2 * h + 2) * half]
        v1, i1 = _topk_rows(_dot_nt(k1, q1, precision=HIGHEST), key_id, PEER_TOPK, nk)
        v2, i2 = _topk_rows(_dot_nt(k2, q2, precision=HIGHEST), key_id, PEER_TOPK, nk)
        vals, ai, bj = [], [], []
        for i, rows in _CAND_BLOCKS:
            if i is None:
                vals.append(v1[8:16] + v2[0:1])
                ai.append(i1[8:16])
                bj.append(jnp.broadcast_to(i2[0:1], (8, tm)))
            else:
                vals.append(v1[i:i + 1] + v2[0:rows])
                ai.append(jnp.broadcast_to(i1[i:i + 1], (rows, tm)))
                bj.append(i2[0:rows])
        cand = jnp.concatenate(vals, axis=0) + neg
        ai = jnp.concatenate(ai, axis=0)
        bj = jnp.concatenate(bj, axis=0)
        sc, sa, sb = [], [], []
        for _ in range(PEER_TOPK):
            mx = jnp.max(cand, axis=0, keepdims=True)
            fid = jnp.min(jnp.where(cand == mx, flat, 1e9), axis=0, keepdims=True)
            hit = flat == fid
            sc.append(mx)
            sa.append(jnp.sum(jnp.where(hit, ai, 0.0), axis=0, keepdims=True))
            sb.append(jnp.sum(jnp.where(hit, bj, 0.0), axis=0, keepdims=True))
            cand = jnp.where(hit, -jnp.inf, cand)
        sc = jnp.concatenate(sc, axis=0)
        e = jnp.exp(sc - sc[0:1])
        g_all.append(e / jnp.sum(e, axis=0, keepdims=True))
        a_all.append(jnp.concatenate(sa, axis=0))
        b_all.append(jnp.concatenate(sb, axis=0))
    a_ref[...] = jnp.concatenate(a_all, axis=0).T
    b_ref[...] = jnp.concatenate(b_all, axis=0).T
    g_ref[...] = jnp.concatenate(g_all, axis=0).T


def peer_select(h2, wq, k1, k2, tm=256):
    N, D = h2.shape
    tm = min(tm, N)
    flat, neg = _cand_constants(tm)
    full = lambda a: pl.BlockSpec(a.shape, lambda i: (0,) * a.ndim)
    out = jax.ShapeDtypeStruct((N, PEER_HEADS * PEER_TOPK), F32)
    ospec = pl.BlockSpec((tm, PEER_HEADS * PEER_TOPK), lambda i: (i, 0))
    return pl.pallas_call(
        _peer_select_kernel,
        grid=(N // tm,),
        in_specs=[pl.BlockSpec((tm, D), lambda i: (i, 0)), full(wq), full(k1), full(k2), full(flat), full(neg)],
        out_specs=[ospec] * 3, out_shape=[out] * 3,
        compiler_params=_cparams("parallel"),
        name="peer_select",
    )(h2, wq, k1, k2, flat, neg)


def _gelu(x):
    return 0.5 * x * (1.0 + lax.erf(x * 0.7071067811865476))


def _peer_expert_kernel(*refs, final):
    if final:
        h_ref, a_ref, b_ref, g_ref, u_ref, v_ref, x_ref, g2_ref, fg_ref, o_ref, w3_ref, acc_ref = refs
    else:
        h_ref, a_ref, b_ref, g_ref, u_ref, v_ref, x_ref, g2_ref, o_ref, w3_ref, acc_ref = refs
    j = pl.program_id(1)
    tm = h_ref.shape[0]
    te = u_ref.shape[0]
    nk = PEER_KEYS
    n_a = te // nk

    @pl.when(j == 0)
    def _():
        acc_ref[...] = jnp.zeros(acc_ref.shape, F32)
        key_id = lax.broadcasted_iota(jnp.int32, (nk, a_ref.shape[1]), 0).astype(F32)

        def tokens8(t8, carry):
            base = pl.multiple_of(t8 * 8, 8)
            a8 = a_ref[pl.ds(base, 8), :]
            b8 = b_ref[pl.ds(base, 8), :]
            g8 = g_ref[pl.ds(base, 8), :]
            for r in range(8):
                g_row = g8[r:r + 1]
                g_hi = g_row.astype(BF16).astype(F32)
                g_lo = g_row - g_hi
                eq_a = key_id == a8[r:r + 1]
                x = jnp.concatenate([jnp.where(eq_a, g_hi, 0.0).astype(BF16),
                                     jnp.where(eq_a, g_lo, 0.0).astype(BF16)], axis=1)
                y1 = jnp.where(key_id == b8[r:r + 1], 1.0, 0.0).astype(BF16)
                y = jnp.concatenate([y1, y1], axis=1)
                w3_ref[pl.ds(base + r, nk, stride=tm), :] = _dot_nt(x, y)
            return carry
        lax.fori_loop(0, tm // 8, tokens8, 0)

    hid = _dot_nt(h_ref[...], u_ref[...])
    acts = []
    for al in range(n_a):
        row0 = pl.multiple_of((j * n_a + al) * tm, tm)
        w_a = w3_ref[pl.ds(row0, tm), :]
        acts.append((_gelu(hid[:, al * nk:(al + 1) * nk]) * w_a).astype(BF16))
    acc_ref[...] += jnp.dot(jnp.concatenate(acts, axis=1), v_ref[...], preferred_element_type=F32)

    @pl.when(j == pl.num_programs(1) - 1)
    def _():
        y = x_ref[...] + g2_ref[0] * acc_ref[...]
        if final:
            ms = jnp.mean(y * y, axis=-1, keepdims=True)
            y = y * lax.rsqrt(ms + RMS_EPS) * fg_ref[...]
        o_ref[...] = y


def peer_experts(h2, a_idx, b_idx, gw, u, v, x1, g2, tokens_per_batch, final_g=None, tm=256, te=1024):
    N, D = x1.shape
    E = u.shape[0]
    tm = min(tm, N)
    nb = g2.shape[0]
    tok = lambda n: pl.BlockSpec((tm, n), lambda i, j: (i, 0))
    in_specs = [tok(D), tok(a_idx.shape[1]), tok(a_idx.shape[1]), tok(a_idx.shape[1]),
                pl.BlockSpec((te, D), lambda i, j: (j, 0)), pl.BlockSpec((te, D), lambda i, j: (j, 0)),
                tok(D), pl.BlockSpec((1, 1, D), lambda i, j: (i * tm // tokens_per_batch, 0, 0))]
    args = [h2, a_idx, b_idx, gw, u, v, x1, g2.reshape(nb, 1, D)]
    if final_g is not None:
        in_specs.append(pl.BlockSpec((1, D), lambda i, j: (0, 0)))
        args.append(final_g.reshape(1, D))
    return pl.pallas_call(
        functools.partial(_peer_expert_kernel, final=final_g is not None),
        grid=(N // tm, E // te),
        in_specs=in_specs,
        out_specs=tok(D),
        out_shape=jax.ShapeDtypeStruct((N, D), F32),
        scratch_shapes=[pltpu.VMEM((PEER_KEYS * tm, PEER_KEYS), F32), pltpu.VMEM((tm, D), F32)],
        compiler_params=_cparams("parallel", "arbitrary"),
        name="peer_experts",
    )(*args)


def peer_layer(x1, h2, wq, k1, k2, u, v, g2, final_g=None):
    B, T, D = x1.shape
    h2f = h2.reshape(B * T, D)
    a_idx, b_idx, gw = peer_select(h2f, wq, k1, k2)
    out = peer_experts(h2f, a_idx, b_idx, gw, u, v, x1.reshape(B * T, D), g2, T, final_g)
    return out.reshape(B, T, D)


def kernel(x, c, ada_w, ada_b, norm1_g, norm2_g, w_in, conv_dw_w, conv_dw_b, conv_gn_g, conv_gn_b,
           rwkv_mu, rwkv_w0, rwkv_w_up, rwkv_a0, rwkv_a_up, rwkv_g_up, rwkv_k_k, rwkv_k_a, rwkv_r_k,
           rwkv_ln_g, rwkv_ln_b, nsa_ck_pos, nsa_ck_w1, nsa_ck_w2, nsa_cv_pos, nsa_cv_w1, nsa_cv_w2,
           w_out, peer_wq, peer_k1, peer_k2, peer_u, peer_v, final_g):
    depth, D = norm1_g.shape
    d_conv = conv_dw_b.shape[1]
    d_rwkv = rwkv_w0.shape[1]
    n_conv = 2 * d_conv
    n_rwkv = rwkv_mu.shape[1]
    d_nsa = D - d_conv - d_rwkv
    n_kv = 6 * NSA_KV_HEADS * HEAD_DIM
    n_gate = 3 * NSA_KV_HEADS * NSA_GROUP
    cuts = np.cumsum([0, n_conv, n_rwkv, d_nsa, n_kv, n_gate])
    mod = ada_mod(c, ada_w, ada_b)
    for l in range(depth):
        sh1, sc1, g1, sh2, sc2, g2 = [mod[l, :, i * D:(i + 1) * D] for i in range(6)]
        w_l = w_in[l].astype(BF16)
        pieces = [w_l[:, cuts[i]:cuts[i + 1]] for i in range(5)]
        pieces[4] = jnp.pad(pieces[4], ((0, 0), (0, LANES - n_gate)))
        p_conv, p_rwkv, p_q, p_kv, p_gate = proj_in(x, sh1, sc1, norm1_g[l], pieces)
        o_conv = conv_mixer(p_conv, conv_dw_w[l], conv_dw_b[l], conv_gn_g[l], conv_gn_b[l])
        o_rwkv = rwkv_mixer(p_rwkv, rwkv_mu[l], rwkv_w0[l], rwkv_w_up[l], rwkv_a0[l], rwkv_a_up[l],
                            rwkv_g_up[l], rwkv_k_k[l], rwkv_k_a[l], rwkv_r_k[l], rwkv_ln_g[l], rwkv_ln_b[l])
        o_nsa = nsa_mixer(p_q, p_kv, p_gate, nsa_ck_pos[l], nsa_ck_w1[l], nsa_ck_w2[l],
                          nsa_cv_pos[l], nsa_cv_w1[l], nsa_cv_w2[l])
        wo = w_out[l].astype(BF16)
        wo_pieces = [wo[:d_conv], wo[d_conv:d_conv + d_rwkv], wo[d_conv + d_rwkv:]]
        x1, h2 = proj_out(x, g1, sh2, sc2, norm2_g[l], [o_conv, o_rwkv, o_nsa], wo_pieces)
        x = peer_layer(x1, h2, peer_wq[l].astype(BF16), peer_k1[l], peer_k2[l],
                       peer_u[l].astype(BF16), peer_v[l].astype(BF16), g2,
                       final_g if l == depth - 1 else None)
    return x
```

```python
import functools

import jax
import jax.numpy as jnp
import numpy as np
from jax import lax
from jax.experimental import pallas as pl
from jax.experimental.pallas import tpu as pltpu

F32 = jnp.float32
BF16 = jnp.bfloat16
HIGHEST = lax.Precision.HIGHEST

HEAD_DIM = 64
CONV_WIDTH = 31
CONV_EPS = 1e-5
RWKV_GN_EPS = 64e-5
RMS_EPS = 1e-6
LORA_W = 64
LORA_A = 64
LORA_G = 128
NSA_KV_HEADS = 2
NSA_GROUP = 4
CMP_BLOCK = 32
CMP_STRIDE = 16
CMP_HIDDEN = 128
SEL_BLOCK = 64
N_SEL = 16
WINDOW = 512
FORCE_SCORE = 1e4
NEG_INF = -1e30
PEER_HEADS = 8
PEER_KEYS = 128
PEER_TOPK = 16

LANES = 128
VMEM_LIMIT = 56 * 1024 * 1024


def _dot_hi(a, b):
    return jnp.dot(a, b, precision=HIGHEST, preferred_element_type=F32)


def _dot_bf(a, b):
    return jnp.dot(a.astype(BF16), b.astype(BF16), preferred_element_type=F32)


def _cparams(*sem):
    return pltpu.CompilerParams(dimension_semantics=sem, vmem_limit_bytes=VMEM_LIMIT)


def _group_avg_matrix(n, group):
    i = np.arange(n)
    return jnp.asarray((i[:, None] // group == i[None, :] // group).astype(np.float32) / group)


def _ada_kernel(c_ref, w_ref, b_ref, o_ref):
    c = c_ref[...]
    cs = c * jax.nn.sigmoid(c)
    o_ref[0] = _dot_hi(cs, w_ref[0]) + b_ref[0]


def ada_mod(c, ada_w, ada_b):
    L, D, N = ada_w.shape
    B = c.shape[0]
    bp = 8
    cp = jnp.zeros((bp, D), F32).at[:B].set(c)
    tn = N // 4
    out = pl.pallas_call(
        _ada_kernel,
        grid=(L, N // tn),
        in_specs=[pl.BlockSpec((bp, D), lambda l, j: (0, 0)),
                  pl.BlockSpec((1, D, tn), lambda l, j: (l, 0, j)),
                  pl.BlockSpec((1, 1, tn), lambda l, j: (l, 0, j))],
        out_specs=pl.BlockSpec((1, bp, tn), lambda l, j: (l, 0, j)),
        out_shape=jax.ShapeDtypeStruct((L, bp, N), F32),
        compiler_params=_cparams("parallel", "parallel"),
        name="ada_mod",
    )(cp, ada_w, ada_b.reshape(L, 1, N))
    return out[:, :B]


def _modulated_norm(x, g, sc, sh):
    ms = jnp.mean(x * x, axis=-1, keepdims=True)
    return x * lax.rsqrt(ms + RMS_EPS) * g * (1.0 + sc) + sh


def _proj_in_kernel(x_ref, sh_ref, sc_ref, g_ref, *refs):
    n = len(refs) // 2
    h = _modulated_norm(x_ref[0], g_ref[...], sc_ref[0], sh_ref[0]).astype(BF16)
    for w_ref, o_ref in zip(refs[:n], refs[n:]):
        o_ref[0] = jnp.dot(h, w_ref[...], preferred_element_type=F32)


def proj_in(x, sh, sc, g, weights, tm=512):
    B, T, D = x.shape
    tm = min(tm, T)
    vec = pl.BlockSpec((1, 1, D), lambda b, i: (b, 0, 0))
    in_specs = [pl.BlockSpec((1, tm, D), lambda b, i: (b, i, 0)), vec, vec,
                pl.BlockSpec((1, D), lambda b, i: (0, 0))]
    in_specs += [pl.BlockSpec(w.shape, lambda b, i: (0, 0)) for w in weights]
    out_specs = [pl.BlockSpec((1, tm, w.shape[1]), lambda b, i: (b, i, 0)) for w in weights]
    out_shape = [jax.ShapeDtypeStruct((B, T, w.shape[1]), F32) for w in weights]
    return pl.pallas_call(
        _proj_in_kernel,
        grid=(B, T // tm),
        in_specs=in_specs, out_specs=out_specs, out_shape=out_shape,
        compiler_params=_cparams("parallel", "parallel"),
        name="proj_in",
    )(x, sh.reshape(B, 1, D), sc.reshape(B, 1, D), g.reshape(1, D), *weights)


CONV_HALO = 32


def _conv_kernel(p_ref, w_ref, b_ref, gg_ref, gb_ref, m_ref, o_ref, ext_ref):
    i = pl.program_id(1)
    tt, dc = o_ref.shape[1], o_ref.shape[2]

    @pl.when(i == 0)
    def _():
        ext_ref[0:CONV_HALO, :] = jnp.zeros((CONV_HALO, dc), F32)

    @pl.when(i > 0)
    def _():
        ext_ref[0:CONV_HALO, :] = ext_ref[tt:tt + CONV_HALO, :]

    p = p_ref[0]
    ext_ref[CONV_HALO:CONV_HALO + tt, :] = p[:, :dc] * jax.nn.sigmoid(p[:, dc:])
    off = CONV_HALO - (CONV_WIDTH - 1)
    acc = jnp.zeros((tt, dc), F32) + b_ref[...]
    for j in range(CONV_WIDTH):
        acc = acc + ext_ref[off + j:off + j + tt, :] * w_ref[j:j + 1, :]
    m = m_ref[...]
    mu = _dot_hi(acc, m)
    d = acc - mu
    var = _dot_hi(d * d, m)
    y = d * lax.rsqrt(var + CONV_EPS) * gg_ref[...] + gb_ref[...]
    o_ref[0] = y * jax.nn.sigmoid(y)


def conv_mixer(p, dw_w, dw_b, gn_g, gn_b, tt=512):
    B, T, two_dc = p.shape
    dc = two_dc // 2
    tt = min(tt, T)
    wpad = jnp.zeros((32, dc), F32).at[:CONV_WIDTH].set(dw_w.reshape(CONV_WIDTH, dc))
    row = pl.BlockSpec((1, dc), lambda b, i: (0, 0))
    return pl.pallas_call(
        _conv_kernel,
        grid=(B, T // tt),
        in_specs=[pl.BlockSpec((1, tt, two_dc), lambda b, i: (b, i, 0)),
                  pl.BlockSpec((32, dc), lambda b, i: (0, 0)), row, row, row,
                  pl.BlockSpec((dc, dc), lambda b, i: (0, 0))],
        out_specs=pl.BlockSpec((1, tt, dc), lambda b, i: (b, i, 0)),
        out_shape=jax.ShapeDtypeStruct((B, T, dc), F32),
        scratch_shapes=[pltpu.VMEM((tt + CONV_HALO, dc), F32)],
        compiler_params=_cparams("parallel", "arbitrary"),
        name="conv_mixer",
    )(p, wpad, dw_b.reshape(1, dc), gn_g.reshape(1, dc), gn_b.reshape(1, dc),
      _group_avg_matrix(dc, HEAD_DIM))


def _group_sum_matrix(n, group):
    i = np.arange(n)
    return jnp.asarray((i[:, None] // group == i[None, :] // group).astype(np.float32))


def _rwkv_pre_kernel(p_ref, halo_ref, mu_ref, w0_ref, wup_ref, a0_ref, aup_ref, gup_ref, kk_ref, ka_ref,
                     rk_ref, ones_ref, w_o, kk_o, b_o, k_o, v_o, r_o, bonus_o, g_o, ext_ref):
    i = pl.program_id(1)
    tt = p_ref.shape[1]
    dr = w_o.shape[2]
    p = p_ref[0]
    first = (i > 0).astype(F32)
    ext_ref[0:8, :] = halo_ref[0] * first
    ext_ref[8:8 + tt, :] = p
    prev = ext_ref[7:7 + tt, :]
    xs = p + (prev - p) * mu_ref[...]
    r = xs[:, 0:dr]
    k = xs[:, dr:2 * dr]
    v = xs[:, 2 * dr:3 * dr]
    o = 3 * dr
    xw = xs[:, o:o + LORA_W]
    xa = xs[:, o + LORA_W:o + LORA_W + LORA_A]
    xg = xs[:, o + LORA_W + LORA_A:o + LORA_W + LORA_A + LORA_G]
    z = w0_ref[...] + _dot_hi(jnp.tanh(xw), wup_ref[...])
    w_log = -jax.nn.softplus(-z) - 0.5
    decay = jnp.exp(-jnp.exp(w_log))
    a = jax.nn.sigmoid(a0_ref[...] + _dot_hi(xa, aup_ref[...]))
    g = _dot_hi(jax.nn.sigmoid(xg), gup_ref[...])
    ones = ones_ref[...]
    kk = k * kk_ref[...]
    nrm = jnp.sqrt(_dot_hi(kk * kk, ones))
    kk = kk / jnp.maximum(nrm, 1e-12)
    k_eff = k * (1.0 + (a - 1.0) * ka_ref[...])
    bonus = _dot_hi(r * k_eff * rk_ref[...], ones) * v
    w_o[0] = decay
    kk_o[0] = kk
    b_o[0] = kk * a
    k_o[0] = k_eff
    v_o[0] = v
    r_o[0] = r
    bonus_o[0] = bonus
    g_o[0] = g


def rwkv_pre(p, mu, w0, w_up, a0, a_up, g_up, k_k, k_a, r_k, tt=512):
    B, T, n_in = p.shape
    dr = w0.shape[0]
    tt = min(tt, T)
    nb8 = tt // 8
    full = lambda a: pl.BlockSpec(a.shape, lambda b, i: (0,) * a.ndim)
    args = [mu.reshape(1, n_in), w0.reshape(1, dr), w_up, a0.reshape(1, dr), a_up, g_up,
            k_k.reshape(1, dr), k_a.reshape(1, dr), r_k.reshape(1, dr), _group_sum_matrix(dr, HEAD_DIM)]
    out = jax.ShapeDtypeStruct((B, T, dr), F32)
    ospec = pl.BlockSpec((1, tt, dr), lambda b, i: (b, i, 0))
    return pl.pallas_call(
        _rwkv_pre_kernel,
        grid=(B, T // tt),
        in_specs=[pl.BlockSpec((1, tt, n_in), lambda b, i: (b, i, 0)),
                  pl.BlockSpec((1, 8, n_in), lambda b, i: (b, jnp.maximum(i * nb8 - 1, 0), 0))]
                 + [full(a) for a in args],
        out_specs=[ospec] * 8, out_shape=[out] * 8,
        scratch_shapes=[pltpu.VMEM((tt + 8, n_in), F32)],
        compiler_params=_cparams("parallel", "parallel"),
        name="rwkv_pre",
    )(p, p, *args)


SCAN_SUB = 64
SCAN_CHUNK = 128


def _scan_select_matrices():
    e = np.zeros((SCAN_SUB // 2, 4 * SCAN_SUB, 2 * LANES), np.float32)
    for t in range(SCAN_SUB):
        c = (t % 2) * LANES
        for part in range(2):
            e[t // 2, 2 * part * SCAN_SUB + t, c:c + HEAD_DIM] = 1.0
            e[t // 2, (2 * part + 1) * SCAN_SUB + t, c + HEAD_DIM:c + LANES] = 1.0
    return jnp.asarray(e, BF16)


def _rwkv_scan_kernel(w_ref, kk_ref, b_ref, k_ref, r_ref, v_ref, bonus_ref, g_ref, lng_ref, lnb_ref,
                      e_ref, avg_ref, o_ref, st_ref, lhs_ref, y_ref):
    nb, tc, dr = v_ref.shape
    n_half = dr // LANES
    n_pair = nb * n_half
    n_sub = tc // SCAN_SUB
    quantities = (w_ref, kk_ref, b_ref, k_ref, r_ref)

    @pl.when(pl.program_id(0) == 0)
    def _():
        st_ref[...] = jnp.zeros(st_ref.shape, F32)

    for qi, q_ref in enumerate(quantities):
        for p in range(n_pair):
            b, hb = divmod(p, n_half)
            xt = q_ref[b, :, hb * LANES:(hb + 1) * LANES].T
            for s in range(n_sub):
                cat = jnp.concatenate([xt[0:HEAD_DIM, s * SCAN_SUB:(s + 1) * SCAN_SUB],
                                       xt[HEAD_DIM:, s * SCAN_SUB:(s + 1) * SCAN_SUB]], axis=1)
                hi = cat.astype(BF16)
                lo = (cat - hi.astype(F32)).astype(BF16)
                row = (qi * n_pair + p) * HEAD_DIM
                lhs_ref[s, row:row + HEAD_DIM, :] = jnp.concatenate([hi, lo], axis=1)

    sub_iota = lax.broadcasted_iota(jnp.int32, (8, LANES), 0)
    for s in range(n_sub):
        def steps8(t8, carry, s=s):
            base = pl.multiple_of(s * SCAN_SUB + t8 * 8, 8)
            y8 = [jnp.zeros((8, LANES), F32) for _ in range(n_pair)]
            for j in range(8):
                if j % 2 == 0:
                    z2 = jnp.dot(lhs_ref[s], e_ref[t8 * 4 + j // 2], preferred_element_type=F32)
                z = z2[:, (j % 2) * LANES:(j % 2 + 1) * LANES]
                for p in range(n_pair):
                    b, hb = divmod(p, n_half)
                    col = lambda qi: z[(qi * n_pair + p) * HEAD_DIM:(qi * n_pair + p + 1) * HEAD_DIM, :]
                    st = st_ref[p]
                    sa = jnp.sum(st * col(1), axis=0, keepdims=True)
                    v_row = v_ref[b, pl.ds(base, 8), hb * LANES:(hb + 1) * LANES][j:j + 1, :]
                    st = st * col(0) - col(2) * sa + col(3) * v_row
                    y = jnp.sum(st * col(4), axis=0, keepdims=True)
                    y8[p] = jnp.where(sub_iota == j, y, y8[p])
                    st_ref[p] = st
            for p in range(n_pair):
                b, hb = divmod(p, n_half)
                y_ref[b, pl.ds(base, 8), hb * LANES:(hb + 1) * LANES] = y8[p]
            return carry
        lax.fori_loop(0, SCAN_SUB // 8, steps8, 0)

    avg = avg_ref[...]
    for b in range(nb):
        y = y_ref[b]
        mu = _dot_hi(y, avg)
        d = y - mu
        var = _dot_hi(d * d, avg)
        yn = d * lax.rsqrt(var + RWKV_GN_EPS) * lng_ref[...] + lnb_ref[...]
        o_ref[b] = (yn + bonus_ref[b]) * g_ref[b]


def rwkv_scan(w, kk, bq, k, r, v, bonus, g, ln_g, ln_b):
    B, T, dr = v.shape
    tc = min(SCAN_CHUNK, T)
    n_pair = B * dr // LANES
    blk = pl.BlockSpec((B, tc, dr), lambda i: (0, i, 0))
    row = pl.BlockSpec((1, dr), lambda i: (0, 0))
    e = _scan_select_matrices()
    return pl.pallas_call(
        _rwkv_scan_kernel,
        grid=(T // tc,),
        in_specs=[blk] * 8 + [row, row, pl.BlockSpec(e.shape, lambda i: (0, 0, 0)),
                              pl.BlockSpec((dr, dr), lambda i: (0, 0))],
        out_specs=blk,
        out_shape=jax.ShapeDtypeStruct((B, T, dr), F32),
        scratch_shapes=[pltpu.VMEM((n_pair, HEAD_DIM, LANES), F32),
                        pltpu.VMEM((tc // SCAN_SUB, 5 * n_pair * HEAD_DIM, 4 * SCAN_SUB), BF16),
                        pltpu.VMEM((B, tc, dr), F32)],
        compiler_params=_cparams("arbitrary"),
        name="rwkv_scan",
    )(w, kk, bq, k, r, v, bonus, g, ln_g.reshape(1, dr), ln_b.reshape(1, dr), e, _group_avg_matrix(dr, HEAD_DIM))


def rwkv_mixer(p, mu, w0, w_up, a0, a_up, g_up, k_k, k_a, r_k, ln_g, ln_b):
    w, kk, bq, k, v, r, bonus, g = rwkv_pre(p, mu, w0, w_up, a0, a_up, g_up, k_k, k_a, r_k.reshape(-1))
    return rwkv_scan(w, kk, bq, k, r, v, bonus, g, ln_g, ln_b)


def _dot_nt(a, b, **kw):
    return lax.dot_general(a, b, (((1,), (1,)), ((), ())), preferred_element_type=F32, **kw)


def _compress_kernel(kz_ref, vz_ref, kpos_ref, kw1_ref, kw2_ref, vpos_ref, vw1_ref, vw2_ref,
                     kc_ref, vc_ref, shift_ref):
    n = kz_ref.shape[2]
    half = kz_ref.shape[3]

    def one(z_ref, pos_ref, w1_ref, w2_ref, o_ref):
        z = z_ref[0, 0]
        top = _dot_bf(z + pos_ref[0:1, :], w1_ref[0:half, :])
        bot = _dot_bf(z + pos_ref[1:2, :], w1_ref[half:2 * half, :])
        shift_ref[0:n, :] = bot
        shift_ref[n:n + 8, :] = jnp.zeros((8, bot.shape[1]), F32)
        pre = top + shift_ref[1:n + 1, :]
        hid = pre * jax.nn.sigmoid(pre)
        o_ref[0, 0] = _dot_bf(hid, w2_ref[...])

    one(kz_ref, kpos_ref, kw1_ref, kw2_ref, kc_ref)
    one(vz_ref, vpos_ref, vw1_ref, vw2_ref, vc_ref)


def nsa_compress(k_cmp, v_cmp, ck_pos, ck_w1, ck_w2, cv_pos, cv_w1, cv_w2):
    B, G, T, dk = k_cmp.shape
    n = T // CMP_STRIDE
    half = CMP_STRIDE * dk
    zspec = pl.BlockSpec((1, 1, n, half), lambda b, g: (b, g, 0, 0))
    full = lambda a: pl.BlockSpec(a.shape, lambda b, g: (0,) * a.ndim)
    ospec = pl.BlockSpec((1, 1, n, dk), lambda b, g: (b, g, 0, 0))
    args = [ck_pos.reshape(2, half), ck_w1.astype(BF16), ck_w2.astype(BF16),
            cv_pos.reshape(2, half), cv_w1.astype(BF16), cv_w2.astype(BF16)]
    return pl.pallas_call(
        _compress_kernel,
        grid=(B, G),
        in_specs=[zspec, zspec] + [full(a) for a in args],
        out_specs=[ospec, ospec],
        out_shape=[jax.ShapeDtypeStruct((B, G, n, dk), F32)] * 2,
        scratch_shapes=[pltpu.VMEM((n + 8, CMP_HIDDEN), F32)],
        compiler_params=_cparams("parallel", "parallel"),
        name="nsa_compress",
    )(k_cmp.reshape(B, G, n, half), v_cmp.reshape(B, G, n, half), *args)


NSA_TQ = 128
NSA_TK = 1024


def _nsa_kernel(q_ref, gate_ref, kc_ref, vc_ref, ks_ref, vs_ref, kw_ref, vw_ref, ov_ref, ex_ref,
                o_ref, m_ref, l_ref, acc_ref):
    qi = pl.program_id(2)
    R, tq, dk = q_ref.shape[2], q_ref.shape[3], q_ref.shape[4]
    n_c = kc_ref.shape[2]
    n_blk = ov_ref.shape[1]
    T = ks_ref.shape[2]
    tk = ex_ref.shape[2]
    t0 = qi * tq
    q4 = q_ref[0, 0].reshape(R * tq, dk)
    t_col = t0 + lax.broadcasted_iota(jnp.int32, (tq, 1), 0)

    kc = kc_ref[0, 0]
    vc = vc_ref[0, 0]
    s = _dot_nt(q4, kc, precision=HIGHEST).reshape(R, tq, n_c)
    c_end = lax.broadcasted_iota(jnp.int32, (tq, n_c), 1) * CMP_STRIDE + (CMP_BLOCK - 1)
    valid_c = (c_end <= t_col)[None]
    s = jnp.where(valid_c, s, NEG_INF)
    e = jnp.exp(s - jnp.max(s, axis=-1, keepdims=True))
    any_c = (t_col >= CMP_BLOCK - 1).astype(F32)[None]
    p_c = e / jnp.sum(e, axis=-1, keepdims=True) * any_c
    o_c = _dot_bf(p_c.reshape(R * tq, n_c), vc)

    imp = _dot_hi(jnp.sum(p_c, axis=0), ov_ref[...])
    blk = lax.broadcasted_iota(jnp.int32, (tq, n_blk), 1)
    cur = t_col // SEL_BLOCK
    forced = (blk == 0) | (blk == cur) | (blk == cur - 1)
    x = jnp.where(forced, FORCE_SCORE, jnp.where(blk <= cur, imp, -1.0))
    x = jnp.where(blk < T // SEL_BLOCK, x, -3e38)
    sel = jnp.zeros((tq, n_blk), F32)
    blk_f = blk.astype(F32)
    for _ in range(N_SEL):
        mx = jnp.max(x, axis=-1, keepdims=True)
        idx = jnp.min(jnp.where(x == mx, blk_f, float(n_blk)), axis=-1, keepdims=True)
        hit = blk_f == idx
        sel = jnp.where(hit, 1.0, sel)
        x = jnp.where(hit, -jnp.inf, x)
    sel_bf = sel.astype(BF16)

    m_ref[...] = jnp.full(m_ref.shape, NEG_INF, F32)
    l_ref[...] = jnp.zeros(l_ref.shape, F32)
    acc_ref[...] = jnp.zeros(acc_ref.shape, F32)
    q4b = q4.astype(BF16)
    n_tiles = (t0 + tq - 1) // tk + 1

    def sel_tile(kj, carry):
        off = pl.multiple_of(kj * tk, tk)
        k_t = ks_ref[0, 0, pl.ds(off, tk), :]
        v_t = vs_ref[0, 0, pl.ds(off, tk), :]
        picked = jnp.dot(sel_bf, ex_ref[kj], preferred_element_type=F32)
        key_pos = off + lax.broadcasted_iota(jnp.int32, (tq, tk), 1)
        bias = jnp.where((picked > 0.5) & (key_pos <= t_col), 0.0, NEG_INF)
        s = _dot_nt(q4b, k_t).reshape(R, tq, tk) + bias[None]
        m_old = m_ref[...]
        m_new = jnp.maximum(m_old, jnp.max(s, axis=-1, keepdims=True))
        alpha = jnp.exp(m_old - m_new)
        p = jnp.exp(s - m_new)
        l_ref[...] = l_ref[...] * alpha + jnp.sum(p, axis=-1, keepdims=True)
        pv = jnp.dot(p.reshape(R * tq, tk).astype(BF16), v_t, preferred_element_type=F32)
        acc_ref[...] = acc_ref[...] * alpha + pv.reshape(R, tq, dk)
        m_ref[...] = m_new
        return carry

    lax.fori_loop(0, n_tiles, sel_tile, 0)
    o_s = acc_ref[...] / l_ref[...]

    span = WINDOW + tq
    w0 = pl.multiple_of(jnp.maximum(t0 - WINDOW, 0), tq)
    k_w = kw_ref[0, 0, pl.ds(w0, span), :]
    v_w = vw_ref[0, 0, pl.ds(w0, span), :]
    dist = t_col - (w0 + lax.broadcasted_iota(jnp.int32, (tq, span), 1))
    bias_w = jnp.where((dist >= 0) & (dist < WINDOW), 0.0, NEG_INF)
    s = _dot_nt(q4b, k_w).reshape(R, tq, span) + bias_w[None]
    e = jnp.exp(s - jnp.max(s, axis=-1, keepdims=True))
    p_w = e / jnp.sum(e, axis=-1, keepdims=True)
    o_w = jnp.dot(p_w.reshape(R * tq, span).astype(BF16), v_w, preferred_element_type=F32).reshape(R, tq, dk)

    gate = jax.nn.sigmoid(gate_ref[0, 0])
    o_c = o_c.reshape(R, tq, dk)
    outs = []
    for r in range(R):
        outs.append(gate[:, r:r + 1] * o_c[r] + gate[:, R + r:R + r + 1] * o_s[r]
                    + gate[:, 2 * R + r:2 * R + r + 1] * o_w[r])
    o_ref[0] = jnp.concatenate(outs, axis=-1)


def _overlap_matrix(n_c, n_blk):
    c0 = np.arange(n_c)[:, None] * CMP_STRIDE
    s0 = np.arange(n_blk)[None, :] * SEL_BLOCK
    return jnp.asarray(((c0 < s0 + SEL_BLOCK) & (c0 + CMP_BLOCK > s0)).astype(np.float32))


def _block_expand_matrices(T, n_blk, tk):
    n_t = T // tk
    blk_of_key = (np.arange(n_t)[:, None] * tk + np.arange(tk)[None, :]) // SEL_BLOCK
    ex = (np.arange(n_blk)[None, :, None] == blk_of_key[:, None, :]).astype(np.float32)
    return jnp.asarray(ex, BF16)


def nsa_attention(q, gate, kc, vc, k_sel, v_sel, k_win, v_win):
    B, G, R, T, dk = q.shape
    n_c = kc.shape[2]
    tq = min(NSA_TQ, T)
    tk = min(NSA_TK, T)
    n_blk = max(T // SEL_BLOCK, LANES)
    ov = _overlap_matrix(n_c, n_blk)
    ex = _block_expand_matrices(T, n_blk, tk)
    res = lambda a: pl.BlockSpec((1, 1) + a.shape[2:], lambda b, g, i: (b, g, 0, 0))
    return pl.pallas_call(
        _nsa_kernel,
        grid=(B, G, T // tq),
        in_specs=[pl.BlockSpec((1, 1, R, tq, dk), lambda b, g, i: (b, g, 0, i, 0)),
                  pl.BlockSpec((1, 1, tq, gate.shape[3]), lambda b, g, i: (b, g, i, 0)),
                  res(kc), res(vc), res(k_sel), res(v_sel), res(k_win), res(v_win),
                  pl.BlockSpec(ov.shape, lambda b, g, i: (0, 0)),
                  pl.BlockSpec(ex.shape, lambda b, g, i: (0, 0, 0))],
        out_specs=pl.BlockSpec((1, tq, R * dk), lambda b, g, i: (b, i, g)),
        out_shape=jax.ShapeDtypeStruct((B, T, G * R * dk), F32),
        scratch_shapes=[pltpu.VMEM((R, tq, 1), F32), pltpu.VMEM((R, tq, 1), F32), pltpu.VMEM((R, tq, dk), F32)],
        compiler_params=_cparams("parallel", "parallel", "arbitrary"),
        name="nsa_attention",
    )(q, gate, kc, vc, k_sel, v_sel, k_win, v_win, ov, ex)


def nsa_mixer(p_q, p_kv, p_gate, ck_pos, ck_w1, ck_w2, cv_pos, cv_w1, cv_w2):
    B, T, _ = p_q.shape
    G, R, dk = NSA_KV_HEADS, NSA_GROUP, HEAD_DIM
    kv = p_kv.reshape(B, T, 6, G, dk).transpose(2, 0, 3, 1, 4)
    kc, vc = nsa_compress(kv[0], kv[1], ck_pos, ck_w1, ck_w2, cv_pos, cv_w1, cv_w2)
    q = (p_q * (dk ** -0.5)).reshape(B, T, G, R, dk).transpose(0, 2, 3, 1, 4)
    gate = p_gate[..., :3 * G * R].reshape(B, T, 3, G, R).transpose(0, 3, 1, 2, 4).reshape(B, G, T, 3 * R)
    gate = jnp.pad(gate, ((0, 0), (0, 0), (0, 0), (0, LANES - 3 * R)))
    kvb = kv[2:].astype(BF16)
    return nsa_attention(q, gate, kc, vc, kvb[0], kvb[1], kvb[2], kvb[3])


def _split_bf16(a):
    hi = a.astype(BF16)
    return hi, (a - hi.astype(F32)).astype(BF16)


def _dot_3pass(a, b):
    ah, al = _split_bf16(a)
    bh, bl = _split_bf16(b)
    d = lambda x, y: jnp.dot(x, y, preferred_element_type=F32)
    return d(ah, bh) + d(ah, bl) + d(al, bh)


def _softmax_cols(s, bias, R, tq):
    ps = []
    for r in range(R):
        s_r = s[:, r * tq:(r + 1) * tq] + bias
        e = jnp.exp(s_r - jnp.max(s_r, axis=0, keepdims=True))
        ps.append((e * (1.0 / jnp.sum(e, axis=0, keepdims=True))).astype(BF16))
    return jnp.concatenate(ps, axis=1)


def _nsa_kernel(q_ref, gate_ref, kc_ref, vct_ref, ks_ref, vst_ref, kw_ref, vwt_ref, ovt_ref, ext_ref,
                o_ref, m_ref, l_ref, acc_ref):
    qi = pl.program_id(2)
    dk = q_ref.shape[3]
    tq = gate_ref.shape[3]
    R = q_ref.shape[4] // tq
    n_c = kc_ref.shape[2]
    n_blk = ovt_ref.shape[0]
    T = ks_ref.shape[2]
    tk = ext_ref.shape[1]
    t0 = qi * tq
    qt = q_ref[0, 0, 0]
    qt_bf = qt.astype(BF16)
    t_row = t0 + lax.broadcasted_iota(jnp.int32, (1, tq), 1)
    lanes = lambda r: slice(r * tq, (r + 1) * tq)

    s = _dot_3pass(kc_ref[0, 0], qt)
    c_end = lax.broadcasted_iota(jnp.int32, (n_c, tq), 0) * CMP_STRIDE + (CMP_BLOCK - 1)
    valid_c = c_end <= t_row
    any_c = (t_row >= CMP_BLOCK - 1).astype(F32)
    p_sum = jnp.zeros((n_c, tq), F32)
    ps = []
    for r in range(R):
        s_r = jnp.where(valid_c, s[:, lanes(r)], NEG_INF)
        e = jnp.exp(s_r - jnp.max(s_r, axis=0, keepdims=True))
        p_r = e * (any_c / jnp.sum(e, axis=0, keepdims=True))
        p_sum = p_sum + p_r
        ps.append(p_r.astype(BF16))
    o_c = jnp.dot(vct_ref[0, 0].astype(BF16), jnp.concatenate(ps, axis=1), preferred_element_type=F32)

    imp = _dot_hi(ovt_ref[...], p_sum)
    blk = lax.broadcasted_iota(jnp.int32, (n_blk, tq), 0).astype(F32)
    cur = (t_row // SEL_BLOCK).astype(F32)
    forced = (blk == 0.0) | (blk == cur) | (blk == cur - 1.0)
    x = jnp.where(forced, FORCE_SCORE, jnp.where(blk <= cur, imp, -1.0))
    x = jnp.where(blk < float(T // SEL_BLOCK), x, -3e38)
    sel = jnp.zeros((n_blk, tq), F32)
    for _ in range(N_SEL):
        mx = jnp.max(x, axis=0, keepdims=True)
        idx = jnp.min(jnp.where(x == mx, blk, float(n_blk)), axis=0, keepdims=True)
        hit = blk == idx
        sel = jnp.where(hit, 1.0, sel)
        x = jnp.where(hit, -jnp.inf, x)
    sel_bf = sel.astype(BF16)

    m_ref[...] = jnp.full(m_ref.shape, NEG_INF, F32)
    l_ref[...] = jnp.zeros(l_ref.shape, F32)
    acc_ref[...] = jnp.zeros(acc_ref.shape, F32)
    n_tiles = (t0 + tq - 1) // tk + 1

    def sel_tile(kj, carry):
        off = pl.multiple_of(kj * tk, tk)
        s = jnp.dot(ks_ref[0, 0, pl.ds(off, tk), :], qt_bf, preferred_element_type=F32)
        picked = jnp.dot(ext_ref[kj], sel_bf, preferred_element_type=F32)
        key_pos = off + lax.broadcasted_iota(jnp.int32, (tk, tq), 0)
        bias = jnp.where((picked > 0.5) & (key_pos <= t_row), 0.0, NEG_INF)
        ps, alphas = [], []
        for r in range(R):
            s_r = s[:, lanes(r)] + bias
            m_old = m_ref[:, lanes(r)]
            m_new = jnp.maximum(m_old, jnp.max(s_r, axis=0, keepdims=True))
            alpha = jnp.exp(m_old - m_new)
            p_r = jnp.exp(s_r - m_new)
            l_ref[:, lanes(r)] = l_ref[:, lanes(r)] * alpha + jnp.sum(p_r, axis=0, keepdims=True)
            m_ref[:, lanes(r)] = m_new
            ps.append(p_r.astype(BF16))
            alphas.append(alpha)
        pv = jnp.dot(vst_ref[0, 0, :, pl.ds(off, tk)], jnp.concatenate(ps, axis=1), preferred_element_type=F32)
        acc_ref[...] = acc_ref[...] * jnp.concatenate(alphas, axis=1) + pv
        return carry

    lax.fori_loop(0, n_tiles, sel_tile, 0)
    o_s = acc_ref[...] * (1.0 / l_ref[...])

    span = WINDOW + tq
    w0 = pl.multiple_of(jnp.maximum(t0 - WINDOW, 0), tq)
    s = jnp.dot(kw_ref[0, 0, pl.ds(w0, span), :], qt_bf, preferred_element_type=F32)
    dist = t_row - (w0 + lax.broadcasted_iota(jnp.int32, (span, tq), 0))
    bias_w = jnp.where((dist >= 0) & (dist < WINDOW), 0.0, NEG_INF)
    o_w = jnp.dot(vwt_ref[0, 0, :, pl.ds(w0, span)], _softmax_cols(s, bias_w, R, tq), preferred_element_type=F32)

    gate = jax.nn.sigmoid(gate_ref[0, 0])
    outs = []
    for r in range(R):
        outs.append(gate[r:r + 1] * o_c[:, lanes(r)] + gate[R + r:R + r + 1] * o_s[:, lanes(r)]
                    + gate[2 * R + r:2 * R + r + 1] * o_w[:, lanes(r)])
    o_ref[0] = jnp.concatenate(outs, axis=0).T


def _overlap_matrix_t(n_c, n_blk):
    c0 = np.arange(n_c)[None, :] * CMP_STRIDE
    s0 = np.arange(n_blk)[:, None] * SEL_BLOCK
    return jnp.asarray(((c0 < s0 + SEL_BLOCK) & (c0 + CMP_BLOCK > s0)).astype(np.float32))


def _block_expand_matrices_t(T, n_blk, tk):
    n_t = T // tk
    blk_of_key = (np.arange(n_t)[:, None] * tk + np.arange(tk)[None, :]) // SEL_BLOCK
    ex = (blk_of_key[:, :, None] == np.arange(n_blk)[None, None, :]).astype(np.float32)
    return jnp.asarray(ex, BF16)


def nsa_attention(qt, gate_t, kc, vct, k_sel, vt_sel, k_win, vt_win):
    B, G, nq, dk, L = qt.shape
    T = k_sel.shape[2]
    tq = T // nq
    R = L // tq
    n_c = kc.shape[2]
    tk = min(NSA_TK, T)
    n_blk = max(T // SEL_BLOCK, LANES)
    ovt = _overlap_matrix_t(n_c, n_blk)
    ext = _block_expand_matrices_t(T, n_blk, tk)
    res = lambda a: pl.BlockSpec((1, 1) + a.shape[2:], lambda b, g, i: (b, g, 0, 0))
    return pl.pallas_call(
        _nsa_kernel,
        grid=(B, G, nq),
        in_specs=[pl.BlockSpec((1, 1, 1, dk, L), lambda b, g, i: (b, g, i, 0, 0)),
                  pl.BlockSpec((1, 1, gate_t.shape[2], tq), lambda b, g, i: (b, g, 0, i)),
                  res(kc), res(vct), res(k_sel), res(vt_sel), res(k_win), res(vt_win),
                  pl.BlockSpec(ovt.shape, lambda b, g, i: (0, 0)),
                  pl.BlockSpec(ext.shape, lambda b, g, i: (0, 0, 0))],
        out_specs=pl.BlockSpec((1, tq, R * dk), lambda b, g, i: (b, i, g)),
        out_shape=jax.ShapeDtypeStruct((B, T, G * R * dk), F32),
        scratch_shapes=[pltpu.VMEM((1, L), F32), pltpu.VMEM((1, L), F32), pltpu.VMEM((dk, L), F32)],
        compiler_params=_cparams("parallel", "parallel", "arbitrary"),
        name="nsa_attention",
    )(qt, gate_t, kc, vct, k_sel, vt_sel, k_win, vt_win, ovt, ext)


def nsa_mixer(p_q, p_kv, p_gate, ck_pos, ck_w1, ck_w2, cv_pos, cv_w1, cv_w2):
    B, T, _ = p_q.shape
    G, R, dk = NSA_KV_HEADS, NSA_GROUP, HEAD_DIM
    tq = min(NSA_TQ, T)
    kv6 = p_kv.reshape(B, T, 6, G, dk)
    rows = lambda i: kv6[:, :, i].transpose(0, 2, 1, 3)
    cols = lambda i: kv6[:, :, i].transpose(0, 2, 3, 1)
    kc, vc = nsa_compress(rows(0), rows(1), ck_pos, ck_w1, ck_w2, cv_pos, cv_w1, cv_w2)
    qt = (p_q * (dk ** -0.5)).reshape(B, T // tq, tq, G, R, dk).transpose(0, 3, 1, 5, 4, 2)
    qt = qt.reshape(B, G, T // tq, dk, R * tq)
    gate_t = p_gate[..., :3 * G * R].reshape(B, T, 3, G, R).transpose(0, 3, 2, 4, 1).reshape(B, G, 3 * R, T)
    gate_t = jnp.pad(gate_t, ((0, 0), (0, 0), (0, 16 - 3 * R), (0, 0)))
    return nsa_attention(qt, gate_t, kc, vc.transpose(0, 1, 3, 2), rows(2).astype(BF16), cols(3).astype(BF16),
                         rows(4).astype(BF16), cols(5).astype(BF16))


def _proj_out_kernel(x_ref, g1_ref, sh_ref, sc_ref, ng_ref, *refs):
    n = (len(refs) - 2) // 2
    x1_ref, h2_ref = refs[2 * n:]
    acc = None
    for m_ref, w_ref in zip(refs[:n], refs[n:2 * n]):
        d = jnp.dot(m_ref[0].astype(BF16), w_ref[...], preferred_element_type=F32)
        acc = d if acc is None else acc + d
    x1 = x_ref[0] + g1_ref[0] * acc
    x1_ref[0] = x1
    h2_ref[0] = _modulated_norm(x1, ng_ref[...], sc_ref[0], sh_ref[0]).astype(BF16)


def proj_out(x, g1, sh2, sc2, norm_g, mixes, weights, tm=512):
    B, T, D = x.shape
    tm = min(tm, T)
    vec = pl.BlockSpec((1, 1, D), lambda b, i: (b, 0, 0))
    tile = lambda n: pl.BlockSpec((1, tm, n), lambda b, i: (b, i, 0))
    return pl.pallas_call(
        _proj_out_kernel,
        grid=(B, T // tm),
        in_specs=[tile(D), vec, vec, vec, pl.BlockSpec((1, D), lambda b, i: (0, 0))]
                 + [tile(m.shape[2]) for m in mixes]
                 + [pl.BlockSpec(w.shape, lambda b, i: (0, 0)) for w in weights],
        out_specs=[tile(D), tile(D)],
        out_shape=[jax.ShapeDtypeStruct((B, T, D), F32), jax.ShapeDtypeStruct((B, T, D), BF16)],
        compiler_params=_cparams("parallel", "parallel"),
        name="proj_out",
    )(x, g1.reshape(B, 1, D), sh2.reshape(B, 1, D), sc2.reshape(B, 1, D), norm_g.reshape(1, D), *mixes, *weights)


_CAND_BLOCKS = [(0, 16), (1, 8), (None, 8)] + [(i, 8) for i in range(2, 8)]
_CAND_LIMIT = {2: 5, 3: 4, 4: 3, 5: 2, 6: 2, 7: 2}


def _cand_constants(tm):
    flat, neg = [], []
    for i, rows in _CAND_BLOCKS:
        for r in range(rows):
            if i is None:
                flat.append((8 + r) * PEER_TOPK)
                neg.append(0.0)
            else:
                flat.append(i * PEER_TOPK + r)
                neg.append(0.0 if r < _CAND_LIMIT.get(i, rows) else -np.inf)
    flat = np.tile(np.asarray(flat, np.float32)[:, None], (1, tm))
    neg = np.tile(np.asarray(neg, np.float32)[:, None], (1, tm))
    return jnp.asarray(flat), jnp.asarray(neg)


def _topk_rows(x, row_id, k, n_rows):
    vals, idxs = [], []
    for _ in range(k):
        mx = jnp.max(x, axis=0, keepdims=True)
        idx = jnp.min(jnp.where(x == mx, row_id, float(n_rows)), axis=0, keepdims=True)
        x = jnp.where(row_id == idx, -jnp.inf, x)
        vals.append(mx)
        idxs.append(idx)
    return jnp.concatenate(vals, axis=0), jnp.concatenate(idxs, axis=0)


def _peer_select_kernel(h_ref, wq_ref, k1_ref, k2_ref, flat_ref, neg_ref, a_ref, b_ref, g_ref):
    tm = h_ref.shape[0]
    nk = k1_ref.shape[0]
    half = k1_ref.shape[1]
    q = jnp.dot(h_ref[...], wq_ref[...], preferred_element_type=F32)
    key_id = lax.broadcasted_iota(jnp.int32, (nk, tm), 0).astype(F32)
    flat = flat_ref[...]
    neg = neg_ref[...]
    k1 = k1_ref[...]
    k2 = k2_ref[...]
    a_all, b_all, g_all = [], [], []
    for h in range(PEER_HEADS):
        q1 = q[:, (2 * h) * half:(2 * h + 1) * half]
        q2 = q[:, (2 * h + 1) * half:(2 * h + 2) * half]
        v1, i1 = _topk_rows(_dot_nt(k1, q1, precision=HIGHEST), key_id, PEER_TOPK, nk)
        v2, i2 = _topk_rows(_dot_nt(k2, q2, precision=HIGHEST), key_id, PEER_TOPK, nk)
        vals, ai, bj = [], [], []
        for i, rows in _CAND_BLOCKS:
            if i is None:
                vals.append(v1[8:16] + v2[0:1])
                ai.append(i1[8:16])
                bj.append(jnp.broadcast_to(i2[0:1], (8, tm)))
            else:
                vals.append(v1[i:i + 1] + v2[0:rows])
                ai.append(jnp.broadcast_to(i1[i:i + 1], (rows, tm)))
                bj.append(i2[0:rows])
        cand = jnp.concatenate(vals, axis=0) + neg
        ai = jnp.concatenate(ai, axis=0)
        bj = jnp.concatenate(bj, axis=0)
        sc, sa, sb = [], [], []
        for _ in range(PEER_TOPK):
            mx = jnp.max(cand, axis=0, keepdims=True)
            fid = jnp.min(jnp.where(cand == mx, flat, 1e9), axis=0, keepdims=True)
            hit = flat == fid
            sc.append(mx)
            sa.append(jnp.sum(jnp.where(hit, ai, 0.0), axis=0, keepdims=True))
            sb.append(jnp.sum(jnp.where(hit, bj, 0.0), axis=0, keepdims=True))
            cand = jnp.where(hit, -jnp.inf, cand)
        sc = jnp.concatenate(sc, axis=0)
        e = jnp.exp(sc - sc[0:1])
        g_all.append(e / jnp.sum(e, axis=0, keepdims=True))
        a_all.append(jnp.concatenate(sa, axis=0))
        b_all.append(jnp.concatenate(sb, axis=0))
    a_ref[...] = jnp.concatenate(a_all, axis=0).T
    b_ref[...] = jnp.concatenate(b_all, axis=0).T
    g_ref[...] = jnp.concatenate(g_all, axis=0).T


def peer_select(h2, wq, k1, k2, tm=256):
    N, D = h2.shape
    tm = min(tm, N)
    flat, neg = _cand_constants(tm)
    full = lambda a: pl.BlockSpec(a.shape, lambda i: (0,) * a.ndim)
    out = jax.ShapeDtypeStruct((N, PEER_HEADS * PEER_TOPK), F32)
    ospec = pl.BlockSpec((tm, PEER_HEADS * PEER_TOPK), lambda i: (i, 0))
    return pl.pallas_call(
        _peer_select_kernel,
        grid=(N // tm,),
        in_specs=[pl.BlockSpec((tm, D), lambda i: (i, 0)), full(wq), full(k1), full(k2), full(flat), full(neg)],
        out_specs=[ospec] * 3, out_shape=[out] * 3,
        compiler_params=_cparams("parallel"),
        name="peer_select",
    )(h2, wq, k1, k2, flat, neg)


def _gelu(x):
    return 0.5 * x * (1.0 + lax.erf(x * 0.7071067811865476))


def _peer_expert_kernel(*refs, final):
    if final:
        h_ref, a_ref, b_ref, g_ref, u_ref, v_ref, x_ref, g2_ref, fg_ref, o_ref, w3_ref, acc_ref = refs
    else:
        h_ref, a_ref, b_ref, g_ref, u_ref, v_ref, x_ref, g2_ref, o_ref, w3_ref, acc_ref = refs
    j = pl.program_id(1)
    tm = h_ref.shape[0]
    te = u_ref.shape[0]
    nk = PEER_KEYS
    n_a = te // nk

    @pl.when(j == 0)
    def _():
        acc_ref[...] = jnp.zeros(acc_ref.shape, F32)
        key_id = lax.broadcasted_iota(jnp.int32, (nk, a_ref.shape[1]), 0).astype(F32)

        def tokens8(t8, carry):
            base = pl.multiple_of(t8 * 8, 8)
            a8 = a_ref[pl.ds(base, 8), :]
            b8 = b_ref[pl.ds(base, 8), :]
            g8 = g_ref[pl.ds(base, 8), :]
            ws = []
            for r in range(8):
                g_row = g8[r:r + 1]
                g_hi = g_row.astype(BF16).astype(F32)
                g_lo = g_row - g_hi
                eq_a = key_id == a8[r:r + 1]
                x = jnp.concatenate([jnp.where(eq_a, g_hi, 0.0).astype(BF16),
                                     jnp.where(eq_a, g_lo, 0.0).astype(BF16)], axis=1)
                y1 = jnp.where(key_id == b8[r:r + 1], 1.0, 0.0).astype(BF16)
                y = jnp.concatenate([y1, y1], axis=1)
                ws.append(_dot_nt(x, y))
            w3_ref[:, pl.ds(base, 8), :] = jnp.swapaxes(jnp.stack(ws, axis=0), 0, 1)
            return carry
        lax.fori_loop(0, tm // 8, tokens8, 0)

    hid = _dot_nt(h_ref[...], u_ref[...])
    acts = []
    for al in range(n_a):
        w_a = w3_ref[j * n_a + al]
        acts.append((_gelu(hid[:, al * nk:(al + 1) * nk]) * w_a).astype(BF16))
    acc_ref[...] += jnp.dot(jnp.concatenate(acts, axis=1), v_ref[...], preferred_element_type=F32)

    @pl.when(j == pl.num_programs(1) - 1)
    def _():
        y = x_ref[...] + g2_ref[0] * acc_ref[...]
        if final:
            ms = jnp.mean(y * y, axis=-1, keepdims=True)
            y = y * lax.rsqrt(ms + RMS_EPS) * fg_ref[...]
        o_ref[...] = y


def peer_experts(h2, a_idx, b_idx, gw, u, v, x1, g2, tokens_per_batch, final_g=None, tm=256, te=2048):
    N, D = x1.shape
    E = u.shape[0]
    tm = min(tm, N)
    nb = g2.shape[0]
    tok = lambda n: pl.BlockSpec((tm, n), lambda i, j: (i, 0))
    in_specs = [tok(D), tok(a_idx.shape[1]), tok(a_idx.shape[1]), tok(a_idx.shape[1]),
                pl.BlockSpec((te, D), lambda i, j: (j, 0)), pl.BlockSpec((te, D), lambda i, j: (j, 0)),
                tok(D), pl.BlockSpec((1, 1, D), lambda i, j: (i * tm // tokens_per_batch, 0, 0))]
    args = [h2, a_idx, b_idx, gw, u, v, x1, g2.reshape(nb, 1, D)]
    if final_g is not None:
        in_specs.append(pl.BlockSpec((1, D), lambda i, j: (0, 0)))
        args.append(final_g.reshape(1, D))
    return pl.pallas_call(
        functools.partial(_peer_expert_kernel, final=final_g is not None),
        grid=(N // tm, E // te),
        in_specs=in_specs,
        out_specs=tok(D),
        out_shape=jax.ShapeDtypeStruct((N, D), F32),
        scratch_shapes=[pltpu.VMEM((PEER_KEYS, tm, PEER_KEYS), F32), pltpu.VMEM((tm, D), F32)],
        compiler_params=_cparams("parallel", "arbitrary"),
        name="peer_experts",
    )(*args)


def peer_layer(x1, h2, wq, k1, k2, u, v, g2, final_g=None):
    B, T, D = x1.shape
    h2f = h2.reshape(B * T, D)
    a_idx, b_idx, gw = peer_select(h2f, wq, k1, k2)
    out = peer_experts(h2f, a_idx, b_idx, gw, u, v, x1.reshape(B * T, D), g2, T, final_g)
    return out.reshape(B, T, D)


def kernel(x, c, ada_w, ada_b, norm1_g, norm2_g, w_in, conv_dw_w, conv_dw_b, conv_gn_g, conv_gn_b,
           rwkv_mu, rwkv_w0, rwkv_w_up, rwkv_a0, rwkv_a_up, rwkv_g_up, rwkv_k_k, rwkv_k_a, rwkv_r_k,
           rwkv_ln_g, rwkv_ln_b, nsa_ck_pos, nsa_ck_w1, nsa_ck_w2, nsa_cv_pos, nsa_cv_w1, nsa_cv_w2,
           w_out, peer_wq, peer_k1, peer_k2, peer_u, peer_v, final_g):
    depth, D = norm1_g.shape
    d_conv = conv_dw_b.shape[1]
    d_rwkv = rwkv_w0.shape[1]
    n_conv = 2 * d_conv
    n_rwkv = rwkv_mu.shape[1]
    d_nsa = D - d_conv - d_rwkv
    n_kv = 6 * NSA_KV_HEADS * HEAD_DIM
    n_gate = 3 * NSA_KV_HEADS * NSA_GROUP
    cuts = np.cumsum([0, n_conv, n_rwkv, d_nsa, n_kv, n_gate])
    mod = ada_mod(c, ada_w, ada_b)
    for l in range(depth):
        sh1, sc1, g1, sh2, sc2, g2 = [mod[l, :, i * D:(i + 1) * D] for i in range(6)]
        w_l = w_in[l].astype(BF16)
        pieces = [w_l[:, cuts[i]:cuts[i + 1]] for i in range(5)]
        pieces[4] = jnp.pad(pieces[4], ((0, 0), (0, LANES - n_gate)))
        p_conv, p_rwkv, p_q, p_kv, p_gate = proj_in(x, sh1, sc1, norm1_g[l], pieces)
        o_conv = conv_mixer(p_conv, conv_dw_w[l], conv_dw_b[l], conv_gn_g[l], conv_gn_b[l])
        o_rwkv = rwkv_mixer(p_rwkv, rwkv_mu[l], rwkv_w0[l], rwkv_w_up[l], rwkv_a0[l], rwkv_a_up[l],
                            rwkv_g_up[l], rwkv_k_k[l], rwkv_k_a[l], rwkv_r_k[l], rwkv_ln_g[l], rwkv_ln_b[l])
        o_nsa = nsa_mixer(p_q, p_kv, p_gate, nsa_ck_pos[l], nsa_ck_w1[l], nsa_ck_w2[l],
                          nsa_cv_pos[l], nsa_cv_w1[l], nsa_cv_w2[l])
        wo = w_out[l].astype(BF16)
        wo_pieces = [wo[:d_conv], wo[d_conv:d_conv + d_rwkv], wo[d_conv + d_rwkv:]]
        x1, h2 = proj_out(x, g1, sh2, sc2, norm2_g[l], [o_conv, o_rwkv, o_nsa], wo_pieces)
        x = peer_layer(x1, h2, peer_wq[l].astype(BF16), peer_k1[l], peer_k2[l],
                       peer_u[l].astype(BF16), peer_v[l].astype(BF16), g2,
                       final_g if l == depth - 1 else None)
    return x
```

```python
import functools

import jax
import jax.numpy as jnp
import numpy as np
from jax import lax
from jax.experimental import pallas as pl
from jax.experimental.pallas import tpu as pltpu

F32 = jnp.float32
BF16 = jnp.bfloat16
HIGHEST = lax.Precision.HIGHEST

HEAD_DIM = 64
CONV_WIDTH = 31
CONV_EPS = 1e-5
RWKV_GN_EPS = 64e-5
RMS_EPS = 1e-6
LORA_W = 64
LORA_A = 64
LORA_G = 128
NSA_KV_HEADS = 2
NSA_GROUP = 4
CMP_BLOCK = 32
CMP_STRIDE = 16
CMP_HIDDEN = 128
SEL_BLOCK = 64
N_SEL = 16
WINDOW = 512
FORCE_SCORE = 1e4
NEG_INF = -1e30
LOG2E = 1.4426950408889634
PEER_HEADS = 8
PEER_KEYS = 128
PEER_TOPK = 16

LANES = 128
VMEM_LIMIT = 56 * 1024 * 1024


def _dot_hi(a, b):
    return jnp.dot(a, b, precision=HIGHEST, preferred_element_type=F32)


def _dot_bf(a, b):
    return jnp.dot(a.astype(BF16), b.astype(BF16), preferred_element_type=F32)


def _cparams(*sem):
    return pltpu.CompilerParams(dimension_semantics=sem, vmem_limit_bytes=VMEM_LIMIT)


def _group_avg_matrix(n, group):
    i = np.arange(n)
    return jnp.asarray((i[:, None] // group == i[None, :] // group).astype(np.float32) / group)


def _ada_kernel(c_ref, w_ref, b_ref, o_ref):
    c = c_ref[...]
    cs = c * jax.nn.sigmoid(c)
    o_ref[0] = _dot_hi(cs, w_ref[0]) + b_ref[0]


def ada_mod(c, ada_w, ada_b):
    L, D, N = ada_w.shape
    B = c.shape[0]
    bp = 8
    cp = jnp.zeros((bp, D), F32).at[:B].set(c)
    tn = N // 4
    out = pl.pallas_call(
        _ada_kernel,
        grid=(L, N // tn),
        in_specs=[pl.BlockSpec((bp, D), lambda l, j: (0, 0)),
                  pl.BlockSpec((1, D, tn), lambda l, j: (l, 0, j)),
                  pl.BlockSpec((1, 1, tn), lambda l, j: (l, 0, j))],
        out_specs=pl.BlockSpec((1, bp, tn), lambda l, j: (l, 0, j)),
        out_shape=jax.ShapeDtypeStruct((L, bp, N), F32),
        compiler_params=_cparams("parallel", "parallel"),
        name="ada_mod",
    )(cp, ada_w, ada_b.reshape(L, 1, N))
    return out[:, :B]


def _modulated_norm(x, g, sc, sh):
    ms = jnp.mean(x * x, axis=-1, keepdims=True)
    return x * lax.rsqrt(ms + RMS_EPS) * g * (1.0 + sc) + sh


def _proj_in_kernel(x_ref, sh_ref, sc_ref, g_ref, *refs):
    n = len(refs) // 2
    h = _modulated_norm(x_ref[0], g_ref[...], sc_ref[0], sh_ref[0]).astype(BF16)
    for w_ref, o_ref in zip(refs[:n], refs[n:]):
        o_ref[0] = jnp.dot(h, w_ref[...], preferred_element_type=F32)


def proj_in(x, sh, sc, g, weights, tm=512):
    B, T, D = x.shape
    tm = min(tm, T)
    vec = pl.BlockSpec((1, 1, D), lambda b, i: (b, 0, 0))
    in_specs = [pl.BlockSpec((1, tm, D), lambda b, i: (b, i, 0)), vec, vec,
                pl.BlockSpec((1, D), lambda b, i: (0, 0))]
    in_specs += [pl.BlockSpec(w.shape, lambda b, i: (0, 0)) for w in weights]
    out_specs = [pl.BlockSpec((1, tm, w.shape[1]), lambda b, i: (b, i, 0)) for w in weights]
    out_shape = [jax.ShapeDtypeStruct((B, T, w.shape[1]), F32) for w in weights]
    return pl.pallas_call(
        _proj_in_kernel,
        grid=(B, T // tm),
        in_specs=in_specs, out_specs=out_specs, out_shape=out_shape,
        compiler_params=_cparams("parallel", "parallel"),
        name="proj_in",
    )(x, sh.reshape(B, 1, D), sc.reshape(B, 1, D), g.reshape(1, D), *weights)


CONV_HALO = 32


def _conv_kernel(p_ref, w_ref, b_ref, gg_ref, gb_ref, m_ref, o_ref, ext_ref):
    i = pl.program_id(1)
    tt, dc = o_ref.shape[1], o_ref.shape[2]

    @pl.when(i == 0)
    def _():
        ext_ref[0:CONV_HALO, :] = jnp.zeros((CONV_HALO, dc), F32)

    @pl.when(i > 0)
    def _():
        ext_ref[0:CONV_HALO, :] = ext_ref[tt:tt + CONV_HALO, :]

    p = p_ref[0]
    ext_ref[CONV_HALO:CONV_HALO + tt, :] = p[:, :dc] * jax.nn.sigmoid(p[:, dc:])
    off = CONV_HALO - (CONV_WIDTH - 1)
    acc = jnp.zeros((tt, dc), F32) + b_ref[...]
    for j in range(CONV_WIDTH):
        acc = acc + ext_ref[off + j:off + j + tt, :] * w_ref[j:j + 1, :]
    m = m_ref[...]
    mu = _dot_hi(acc, m)
    d = acc - mu
    var = _dot_hi(d * d, m)
    y = d * lax.rsqrt(var + CONV_EPS) * gg_ref[...] + gb_ref[...]
    o_ref[0] = y * jax.nn.sigmoid(y)


def conv_mixer(p, dw_w, dw_b, gn_g, gn_b, tt=512):
    B, T, two_dc = p.shape
    dc = two_dc // 2
    tt = min(tt, T)
    wpad = jnp.zeros((32, dc), F32).at[:CONV_WIDTH].set(dw_w.reshape(CONV_WIDTH, dc))
    row = pl.BlockSpec((1, dc), lambda b, i: (0, 0))
    return pl.pallas_call(
        _conv_kernel,
        grid=(B, T // tt),
        in_specs=[pl.BlockSpec((1, tt, two_dc), lambda b, i: (b, i, 0)),
                  pl.BlockSpec((32, dc), lambda b, i: (0, 0)), row, row, row,
                  pl.BlockSpec((dc, dc), lambda b, i: (0, 0))],
        out_specs=pl.BlockSpec((1, tt, dc), lambda b, i: (b, i, 0)),
        out_shape=jax.ShapeDtypeStruct((B, T, dc), F32),
        scratch_shapes=[pltpu.VMEM((tt + CONV_HALO, dc), F32)],
        compiler_params=_cparams("parallel", "arbitrary"),
        name="conv_mixer",
    )(p, wpad, dw_b.reshape(1, dc), gn_g.reshape(1, dc), gn_b.reshape(1, dc),
      _group_avg_matrix(dc, HEAD_DIM))


def _group_sum_matrix(n, group):
    i = np.arange(n)
    return jnp.asarray((i[:, None] // group == i[None, :] // group).astype(np.float32))


def _rwkv_pre_kernel(p_ref, halo_ref, mu_ref, w0_ref, wup_ref, a0_ref, aup_ref, gup_ref, kk_ref, ka_ref,
                     rk_ref, ones_ref, w_o, kk_o, b_o, k_o, v_o, r_o, bonus_o, g_o, ext_ref):
    i = pl.program_id(1)
    tt = p_ref.shape[1]
    dr = w_o.shape[2]
    p = p_ref[0]
    first = (i > 0).astype(F32)
    ext_ref[0:8, :] = halo_ref[0] * first
    ext_ref[8:8 + tt, :] = p
    prev = ext_ref[7:7 + tt, :]
    xs = p + (prev - p) * mu_ref[...]
    r = xs[:, 0:dr]
    k = xs[:, dr:2 * dr]
    v = xs[:, 2 * dr:3 * dr]
    o = 3 * dr
    xw = xs[:, o:o + LORA_W]
    xa = xs[:, o + LORA_W:o + LORA_W + LORA_A]
    xg = xs[:, o + LORA_W + LORA_A:o + LORA_W + LORA_A + LORA_G]
    z = w0_ref[...] + _dot_hi(jnp.tanh(xw), wup_ref[...])
    w_log = -jax.nn.softplus(-z) - 0.5
    decay = jnp.exp(-jnp.exp(w_log))
    a = jax.nn.sigmoid(a0_ref[...] + _dot_hi(xa, aup_ref[...]))
    g = _dot_hi(jax.nn.sigmoid(xg), gup_ref[...])
    ones = ones_ref[...]
    kk = k * kk_ref[...]
    nrm = jnp.sqrt(_dot_hi(kk * kk, ones))
    kk = kk / jnp.maximum(nrm, 1e-12)
    k_eff = k * (1.0 + (a - 1.0) * ka_ref[...])
    bonus = _dot_hi(r * k_eff * rk_ref[...], ones) * v
    w_o[0] = decay
    kk_o[0] = kk
    b_o[0] = kk * a
    k_o[0] = k_eff
    v_o[0] = v
    r_o[0] = r
    bonus_o[0] = bonus
    g_o[0] = g


def rwkv_pre(p, mu, w0, w_up, a0, a_up, g_up, k_k, k_a, r_k, tt=512):
    B, T, n_in = p.shape
    dr = w0.shape[0]
    tt = min(tt, T)
    nb8 = tt // 8
    full = lambda a: pl.BlockSpec(a.shape, lambda b, i: (0,) * a.ndim)
    args = [mu.reshape(1, n_in), w0.reshape(1, dr), w_up, a0.reshape(1, dr), a_up, g_up,
            k_k.reshape(1, dr), k_a.reshape(1, dr), r_k.reshape(1, dr), _group_sum_matrix(dr, HEAD_DIM)]
    out = jax.ShapeDtypeStruct((B, T, dr), F32)
    ospec = pl.BlockSpec((1, tt, dr), lambda b, i: (b, i, 0))
    return pl.pallas_call(
        _rwkv_pre_kernel,
        grid=(B, T // tt),
        in_specs=[pl.BlockSpec((1, tt, n_in), lambda b, i: (b, i, 0)),
                  pl.BlockSpec((1, 8, n_in), lambda b, i: (b, jnp.maximum(i * nb8 - 1, 0), 0))]
                 + [full(a) for a in args],
        out_specs=[ospec] * 8, out_shape=[out] * 8,
        scratch_shapes=[pltpu.VMEM((tt + 8, n_in), F32)],
        compiler_params=_cparams("parallel", "parallel"),
        name="rwkv_pre",
    )(p, p, *args)


SCAN_SUB = 64
SCAN_CHUNK = 128


def _scan_select_matrices():
    e = np.zeros((SCAN_SUB // 2, 4 * SCAN_SUB, 2 * LANES), np.float32)
    for t in range(SCAN_SUB):
        c = (t % 2) * LANES
        for part in range(2):
            e[t // 2, 2 * part * SCAN_SUB + t, c:c + HEAD_DIM] = 1.0
            e[t // 2, (2 * part + 1) * SCAN_SUB + t, c + HEAD_DIM:c + LANES] = 1.0
    return jnp.asarray(e, BF16)


def _rwkv_scan_kernel(w_ref, kk_ref, b_ref, k_ref, r_ref, v_ref, bonus_ref, g_ref, lng_ref, lnb_ref,
                      e_ref, avg_ref, o_ref, st_ref, lhs_ref, y_ref):
    nb, tc, dr = v_ref.shape
    n_half = dr // LANES
    n_pair = nb * n_half
    n_sub = tc // SCAN_SUB
    quantities = (w_ref, kk_ref, b_ref, k_ref, r_ref)

    @pl.when(pl.program_id(0) == 0)
    def _():
        st_ref[...] = jnp.zeros(st_ref.shape, F32)

    for qi, q_ref in enumerate(quantities):
        for p in range(n_pair):
            b, hb = divmod(p, n_half)
            xt = q_ref[b, :, hb * LANES:(hb + 1) * LANES].T
            for s in range(n_sub):
                cat = jnp.concatenate([xt[0:HEAD_DIM, s * SCAN_SUB:(s + 1) * SCAN_SUB],
                                       xt[HEAD_DIM:, s * SCAN_SUB:(s + 1) * SCAN_SUB]], axis=1)
                hi = cat.astype(BF16)
                lo = (cat - hi.astype(F32)).astype(BF16)
                row = (qi * n_pair + p) * HEAD_DIM
                lhs_ref[s, row:row + HEAD_DIM, :] = jnp.concatenate([hi, lo], axis=1)

    sub_iota = lax.broadcasted_iota(jnp.int32, (8, LANES), 0)
    for s in range(n_sub):
        def steps8(t8, carry, s=s):
            base = pl.multiple_of(s * SCAN_SUB + t8 * 8, 8)
            y8 = [jnp.zeros((8, LANES), F32) for _ in range(n_pair)]
            for j in range(8):
                if j % 2 == 0:
                    z2 = jnp.dot(lhs_ref[s], e_ref[t8 * 4 + j // 2], preferred_element_type=F32)
                z = z2[:, (j % 2) * LANES:(j % 2 + 1) * LANES]
                for p in range(n_pair):
                    b, hb = divmod(p, n_half)
                    col = lambda qi: z[(qi * n_pair + p) * HEAD_DIM:(qi * n_pair + p + 1) * HEAD_DIM, :]
                    st = st_ref[p]
                    sa = jnp.sum(st * col(1), axis=0, keepdims=True)
                    v_row = v_ref[b, pl.ds(base, 8), hb * LANES:(hb + 1) * LANES][j:j + 1, :]
                    st = st * col(0) - col(2) * sa + col(3) * v_row
                    y = jnp.sum(st * col(4), axis=0, keepdims=True)
                    y8[p] = jnp.where(sub_iota == j, y, y8[p])
                    st_ref[p] = st
            for p in range(n_pair):
                b, hb = divmod(p, n_half)
                y_ref[b, pl.ds(base, 8), hb * LANES:(hb + 1) * LANES] = y8[p]
            return carry
        lax.fori_loop(0, SCAN_SUB // 8, steps8, 0)

    avg = avg_ref[...]
    for b in range(nb):
        y = y_ref[b]
        mu = _dot_hi(y, avg)
        d = y - mu
        var = _dot_hi(d * d, avg)
        yn = d * lax.rsqrt(var + RWKV_GN_EPS) * lng_ref[...] + lnb_ref[...]
        o_ref[b] = (yn + bonus_ref[b]) * g_ref[b]


def rwkv_scan(w, kk, bq, k, r, v, bonus, g, ln_g, ln_b):
    B, T, dr = v.shape
    tc = min(SCAN_CHUNK, T)
    n_pair = B * dr // LANES
    blk = pl.BlockSpec((B, tc, dr), lambda i: (0, i, 0))
    row = pl.BlockSpec((1, dr), lambda i: (0, 0))
    e = _scan_select_matrices()
    return pl.pallas_call(
        _rwkv_scan_kernel,
        grid=(T // tc,),
        in_specs=[blk] * 8 + [row, row, pl.BlockSpec(e.shape, lambda i: (0, 0, 0)),
                              pl.BlockSpec((dr, dr), lambda i: (0, 0))],
        out_specs=blk,
        out_shape=jax.ShapeDtypeStruct((B, T, dr), F32),
        scratch_shapes=[pltpu.VMEM((n_pair, HEAD_DIM, LANES), F32),
                        pltpu.VMEM((tc // SCAN_SUB, 5 * n_pair * HEAD_DIM, 4 * SCAN_SUB), BF16),
                        pltpu.VMEM((B, tc, dr), F32)],
        compiler_params=_cparams("arbitrary"),
        name="rwkv_scan",
    )(w, kk, bq, k, r, v, bonus, g, ln_g.reshape(1, dr), ln_b.reshape(1, dr), e, _group_avg_matrix(dr, HEAD_DIM))


def rwkv_mixer(p, mu, w0, w_up, a0, a_up, g_up, k_k, k_a, r_k, ln_g, ln_b):
    w, kk, bq, k, v, r, bonus, g = rwkv_pre(p, mu, w0, w_up, a0, a_up, g_up, k_k, k_a, r_k.reshape(-1))
    return rwkv_scan(w, kk, bq, k, r, v, bonus, g, ln_g, ln_b)


def _dot_nt(a, b, **kw):
    return lax.dot_general(a, b, (((1,), (1,)), ((), ())), preferred_element_type=F32, **kw)


def _compress_kernel(kz_ref, vz_ref, kpos_ref, kw1_ref, kw2_ref, vpos_ref, vw1_ref, vw2_ref,
                     kc_ref, vc_ref, shift_ref):
    n = kz_ref.shape[2]
    half = kz_ref.shape[3]

    def one(z_ref, pos_ref, w1_ref, w2_ref, o_ref):
        z = z_ref[0, 0]
        top = _dot_bf(z + pos_ref[0:1, :], w1_ref[0:half, :])
        bot = _dot_bf(z + pos_ref[1:2, :], w1_ref[half:2 * half, :])
        shift_ref[0:n, :] = bot
        shift_ref[n:n + 8, :] = jnp.zeros((8, bot.shape[1]), F32)
        pre = top + shift_ref[1:n + 1, :]
        hid = pre * jax.nn.sigmoid(pre)
        o_ref[0, 0] = _dot_bf(hid, w2_ref[...])

    one(kz_ref, kpos_ref, kw1_ref, kw2_ref, kc_ref)
    one(vz_ref, vpos_ref, vw1_ref, vw2_ref, vc_ref)


def nsa_compress(k_cmp, v_cmp, ck_pos, ck_w1, ck_w2, cv_pos, cv_w1, cv_w2):
    B, G, T, dk = k_cmp.shape
    n = T // CMP_STRIDE
    half = CMP_STRIDE * dk
    zspec = pl.BlockSpec((1, 1, n, half), lambda b, g: (b, g, 0, 0))
    full = lambda a: pl.BlockSpec(a.shape, lambda b, g: (0,) * a.ndim)
    ospec = pl.BlockSpec((1, 1, n, dk), lambda b, g: (b, g, 0, 0))
    args = [ck_pos.reshape(2, half), ck_w1.astype(BF16), ck_w2.astype(BF16),
            cv_pos.reshape(2, half), cv_w1.astype(BF16), cv_w2.astype(BF16)]
    return pl.pallas_call(
        _compress_kernel,
        grid=(B, G),
        in_specs=[zspec, zspec] + [full(a) for a in args],
        out_specs=[ospec, ospec],
        out_shape=[jax.ShapeDtypeStruct((B, G, n, dk), F32)] * 2,
        scratch_shapes=[pltpu.VMEM((n + 8, CMP_HIDDEN), F32)],
        compiler_params=_cparams("parallel", "parallel"),
        name="nsa_compress",
    )(k_cmp.reshape(B, G, n, half), v_cmp.reshape(B, G, n, half), *args)


NSA_TQ = 128
NSA_TK = 1024


def _split_bf16(a):
    hi = a.astype(BF16)
    return hi, (a - hi.astype(F32)).astype(BF16)


def _dot_3pass(a, b):
    ah, al = _split_bf16(a)
    bh, bl = _split_bf16(b)
    d = lambda x, y: jnp.dot(x, y, preferred_element_type=F32)
    return d(ah, bh) + d(ah, bl) + d(al, bh)


def _nsa_kernel(q_ref, gate_ref, kc_ref, vct_ref, ks_ref, vst_ref, kw_ref, vwt_ref, ovt_ref,
                o_ref, m_ref, acc_ref, ow_ref, sel_ref, s0_ref, s1_ref, p_ref):
    qi = pl.program_id(2)
    dk = q_ref.shape[3]
    tq = gate_ref.shape[3]
    R = q_ref.shape[4] // tq
    n_c = kc_ref.shape[2]
    n_blk = ovt_ref.shape[0]
    T = ks_ref.shape[2]
    tk = s0_ref.shape[0]
    t0 = qi * tq
    qt = q_ref[0, 0, 0]
    qt_bf = qt.astype(BF16)
    t_row = t0 + lax.broadcasted_iota(jnp.int32, (1, tq), 1)
    lanes = lambda r: slice(r * tq, (r + 1) * tq)

    s = _dot_3pass(kc_ref[0, 0], qt)
    c_end = lax.broadcasted_iota(jnp.int32, (n_c, tq), 0) * CMP_STRIDE + (CMP_BLOCK - 1)
    valid_c = c_end <= t_row
    any_c = (t_row >= CMP_BLOCK - 1).astype(F32)
    p_sum = jnp.zeros((n_c, tq), F32)
    ps = []
    for r in range(R):
        s_r = jnp.where(valid_c, s[:, lanes(r)], NEG_INF)
        e = jnp.exp2(s_r - jnp.max(s_r, axis=0, keepdims=True))
        p_r = e * (any_c / jnp.sum(e, axis=0, keepdims=True))
        p_sum = p_sum + p_r
        ps.append(p_r.astype(BF16))
    o_c = jnp.dot(vct_ref[0, 0].astype(BF16), jnp.concatenate(ps, axis=1), preferred_element_type=F32)

    imp = _dot_hi(ovt_ref[...], p_sum)
    blk = lax.broadcasted_iota(jnp.int32, (n_blk, tq), 0).astype(F32)
    cur = (t_row // SEL_BLOCK).astype(F32)
    forced = (blk == 0.0) | (blk == cur) | (blk == cur - 1.0)
    x = jnp.where(forced, FORCE_SCORE, jnp.where(blk <= cur, imp, -1.0))
    x = jnp.where(blk < float(T // SEL_BLOCK), x, -3e38)
    sel = jnp.zeros((n_blk, tq), F32)
    for _ in range(N_SEL):
        mx = jnp.max(x, axis=0, keepdims=True)
        idx = jnp.min(jnp.where(x == mx, blk, float(n_blk)), axis=0, keepdims=True)
        hit = blk == idx
        sel = jnp.where(hit, 1.0, sel)
        x = jnp.where(hit, -jnp.inf, x)

    span = WINDOW + tq
    w0 = pl.multiple_of(jnp.maximum(t0 - WINDOW, 0), tq)
    s = jnp.dot(kw_ref[0, 0, pl.ds(w0, span), :], qt_bf, preferred_element_type=F32)
    dist = t_row - (w0 + lax.broadcasted_iota(jnp.int32, (span, tq), 0))
    bias_w = jnp.where((dist >= 0) & (dist < WINDOW), 0.0, NEG_INF)
    ps = []
    for r in range(R):
        s_r = s[:, lanes(r)] + bias_w
        ps.append(jnp.exp2(s_r - jnp.max(s_r, axis=0, keepdims=True)).astype(BF16))
    ow_ref[...] = jnp.dot(vwt_ref[0, 0, :, pl.ds(w0, span)], jnp.concatenate(ps, axis=1),
                          preferred_element_type=F32)

    bpt = tk // SEL_BLOCK
    n_kt = T // tk
    n_full = (t0 + tq - 1) // tk
    sel_ref[...] = jnp.where(sel > 0.5, 0.0, NEG_INF)
    m_ref[...] = jnp.full(m_ref.shape, NEG_INF, F32)
    acc_ref[...] = jnp.zeros(acc_ref.shape, F32)
    q_pad = jnp.zeros((ks_ref.shape[3] - dk - bpt, R * tq), BF16)

    def scores(kt):
        kt = jnp.minimum(kt, n_kt - 1)
        rows = sel_ref[pl.ds(pl.multiple_of(kt * bpt, bpt), bpt), :].astype(BF16)
        q_aug = jnp.concatenate([qt_bf, jnp.concatenate([rows] * R, axis=1), q_pad], axis=0)
        off = pl.multiple_of(kt * tk, tk)
        return jnp.dot(ks_ref[0, 0, pl.ds(off, tk), :], q_aug, preferred_element_type=F32)

    def softmax_tile(s_ref, kt, causal):
        if causal:
            key_pos = kt * tk + lax.broadcasted_iota(jnp.int32, (tk, tq), 0)
            bias = jnp.where(key_pos <= t_row, 0.0, NEG_INF)
        for r in range(R):
            m_old = m_ref[:, lanes(r)]
            if causal:
                m_new = jnp.maximum(m_old, jnp.max(s_ref[:, lanes(r)] + bias, axis=0, keepdims=True))
                p_r = jnp.exp2((s_ref[:, lanes(r)] - m_new) + bias)
            else:
                m_new = jnp.maximum(m_old, jnp.max(s_ref[:, lanes(r)], axis=0, keepdims=True))
                p_r = jnp.exp2(s_ref[:, lanes(r)] - m_new)
            m_ref[:, lanes(r)] = m_new
            p_ref[:, lanes(r)] = p_r.astype(BF16)
            acc_ref[:, lanes(r)] = acc_ref[:, lanes(r)] * jnp.exp2(m_old - m_new)
        off = pl.multiple_of(kt * tk, tk)
        acc_ref[...] += jnp.dot(vst_ref[0, 0, :, pl.ds(off, tk)], p_ref[...], preferred_element_type=F32)

    s0_ref[...] = scores(0)

    def tile_pair(i, carry):
        s1_ref[...] = scores(2 * i + 1)
        softmax_tile(s0_ref, 2 * i, False)
        s0_ref[...] = scores(2 * i + 2)
        softmax_tile(s1_ref, 2 * i + 1, False)
        return carry

    lax.fori_loop(0, n_full // 2, tile_pair, 0)
    odd = n_full % 2 == 1

    @pl.when(odd)
    def _():
        s1_ref[...] = scores(n_full)
        softmax_tile(s0_ref, n_full - 1, False)
        softmax_tile(s1_ref, n_full, True)

    @pl.when(jnp.logical_not(odd))
    def _():
        softmax_tile(s0_ref, n_full, True)

    acc = acc_ref[...]
    o_s = acc[0:dk] * (1.0 / acc[dk:dk + 1])
    o_w = ow_ref[...]
    o_w = o_w[0:dk] * (1.0 / o_w[dk:dk + 1])

    gate = jax.nn.sigmoid(gate_ref[0, 0])
    outs = []
    for r in range(R):
        outs.append(gate[r:r + 1] * o_c[:, lanes(r)] + gate[R + r:R + r + 1] * o_s[:, lanes(r)]
                    + gate[2 * R + r:2 * R + r + 1] * o_w[:, lanes(r)])
    o_ref[0] = jnp.concatenate(outs, axis=0).T


def _overlap_matrix_t(n_c, n_blk):
    c0 = np.arange(n_c)[None, :] * CMP_STRIDE
    s0 = np.arange(n_blk)[:, None] * SEL_BLOCK
    return jnp.asarray(((c0 < s0 + SEL_BLOCK) & (c0 + CMP_BLOCK > s0)).astype(np.float32))


def nsa_attention(qt, gate_t, kc, vct, k_sel, vt_sel, k_win, vt_win):
    B, G, nq, dk, L = qt.shape
    T = k_win.shape[2]
    tq = T // nq
    R = L // tq
    n_c = kc.shape[2]
    tk = min(NSA_TK, T)
    n_blk = max(T // SEL_BLOCK, LANES)
    ovt = _overlap_matrix_t(n_c, n_blk)
    res = lambda a: pl.BlockSpec((1, 1) + a.shape[2:], lambda b, g, i: (b, g, 0, 0))
    return pl.pallas_call(
        _nsa_kernel,
        grid=(B, G, nq),
        in_specs=[pl.BlockSpec((1, 1, 1, dk, L), lambda b, g, i: (b, g, i, 0, 0)),
                  pl.BlockSpec((1, 1, gate_t.shape[2], tq), lambda b, g, i: (b, g, 0, i)),
                  res(kc), res(vct), res(k_sel), res(vt_sel), res(k_win), res(vt_win),
                  pl.BlockSpec(ovt.shape, lambda b, g, i: (0, 0))],
        out_specs=pl.BlockSpec((1, tq, R * dk), lambda b, g, i: (b, i, g)),
        out_shape=jax.ShapeDtypeStruct((B, T, G * R * dk), F32),
        scratch_shapes=[pltpu.VMEM((1, L), F32), pltpu.VMEM((dk + 16, L), F32),
                        pltpu.VMEM((dk + 16, L), F32), pltpu.VMEM((n_blk, tq), F32),
                        pltpu.VMEM((tk, L), F32), pltpu.VMEM((tk, L), F32), pltpu.VMEM((tk, L), BF16)],
        compiler_params=_cparams("parallel", "parallel", "arbitrary"),
        name="nsa_attention",
    )(qt, gate_t, kc, vct, k_sel, vt_sel, k_win, vt_win, ovt)


def nsa_mixer(p_q, p_kv, p_gate, ck_pos, ck_w1, ck_w2, cv_pos, cv_w1, cv_w2):
    B, T, _ = p_q.shape
    G, R, dk = NSA_KV_HEADS, NSA_GROUP, HEAD_DIM
    tq = min(NSA_TQ, T)
    kv6 = p_kv.reshape(B, T, 6, G, dk)
    rows = lambda i: kv6[:, :, i].transpose(0, 2, 1, 3)
    cols = lambda i: kv6[:, :, i].transpose(0, 2, 3, 1)
    kc, vc = nsa_compress(rows(0), rows(1), ck_pos, ck_w1, ck_w2, cv_pos, cv_w1, cv_w2)
    qt = (p_q * (dk ** -0.5 * LOG2E)).reshape(B, T // tq, tq, G, R, dk).transpose(0, 3, 1, 5, 4, 2)
    qt = qt.reshape(B, G, T // tq, dk, R * tq)
    gate_t = p_gate[..., :3 * G * R].reshape(B, T, 3, G, R).transpose(0, 3, 2, 4, 1).reshape(B, G, 3 * R, T)
    gate_t = jnp.pad(gate_t, ((0, 0), (0, 0), (0, 16 - 3 * R), (0, 0)))
    tk = min(NSA_TK, T)
    blk_onehot = (jnp.arange(T)[:, None] // SEL_BLOCK % (tk // SEL_BLOCK) == jnp.arange(dk)[None, :]).astype(BF16)
    k_sel = jnp.concatenate([rows(2).astype(BF16), jnp.broadcast_to(blk_onehot, (B, G, T, dk))], axis=-1)
    ones_rows = jnp.zeros((B, G, 16, T), BF16).at[:, :, 0].set(1.0)
    with_ones = lambda vt: jnp.concatenate([vt.astype(BF16), ones_rows], axis=2)
    return nsa_attention(qt, gate_t, kc, vc.transpose(0, 1, 3, 2), k_sel, with_ones(cols(3)),
                         rows(4).astype(BF16), with_ones(cols(5)))


def _proj_out_kernel(x_ref, g1_ref, sh_ref, sc_ref, ng_ref, *refs):
    n = (len(refs) - 2) // 2
    x1_ref, h2_ref = refs[2 * n:]
    acc = None
    for m_ref, w_ref in zip(refs[:n], refs[n:2 * n]):
        d = jnp.dot(m_ref[0].astype(BF16), w_ref[...], preferred_element_type=F32)
        acc = d if acc is None else acc + d
    x1 = x_ref[0] + g1_ref[0] * acc
    x1_ref[0] = x1
    h2_ref[0] = _modulated_norm(x1, ng_ref[...], sc_ref[0], sh_ref[0]).astype(BF16)


def proj_out(x, g1, sh2, sc2, norm_g, mixes, weights, tm=512):
    B, T, D = x.shape
    tm = min(tm, T)
    vec = pl.BlockSpec((1, 1, D), lambda b, i: (b, 0, 0))
    tile = lambda n: pl.BlockSpec((1, tm, n), lambda b, i: (b, i, 0))
    return pl.pallas_call(
        _proj_out_kernel,
        grid=(B, T // tm),
        in_specs=[tile(D), vec, vec, vec, pl.BlockSpec((1, D), lambda b, i: (0, 0))]
                 + [tile(m.shape[2]) for m in mixes]
                 + [pl.BlockSpec(w.shape, lambda b, i: (0, 0)) for w in weights],
        out_specs=[tile(D), tile(D)],
        out_shape=[jax.ShapeDtypeStruct((B, T, D), F32), jax.ShapeDtypeStruct((B, T, D), BF16)],
        compiler_params=_cparams("parallel", "parallel"),
        name="proj_out",
    )(x, g1.reshape(B, 1, D), sh2.reshape(B, 1, D), sc2.reshape(B, 1, D), norm_g.reshape(1, D), *mixes, *weights)


_CAND_BLOCKS = [(0, 16), (1, 8), (None, 8)] + [(i, 8) for i in range(2, 8)]
_CAND_LIMIT = {2: 5, 3: 4, 4: 3, 5: 2, 6: 2, 7: 2}


def _cand_constants(tm):
    flat, neg = [], []
    for i, rows in _CAND_BLOCKS:
        for r in range(rows):
            if i is None:
                flat.append((8 + r) * PEER_TOPK)
                neg.append(0.0)
            else:
                flat.append(i * PEER_TOPK + r)
                neg.append(0.0 if r < _CAND_LIMIT.get(i, rows) else -np.inf)
    flat = np.tile(np.asarray(flat, np.float32)[:, None], (1, tm))
    neg = np.tile(np.asarray(neg, np.float32)[:, None], (1, tm))
    return jnp.asarray(flat), jnp.asarray(neg)


def _topk_rows(x, row_id, k, n_rows):
    vals, idxs = [], []
    for _ in range(k):
        mx = jnp.max(x, axis=0, keepdims=True)
        idx = jnp.min(jnp.where(x == mx, row_id, float(n_rows)), axis=0, keepdims=True)
        x = jnp.where(row_id == idx, -jnp.inf, x)
        vals.append(mx)
        idxs.append(idx)
    return jnp.concatenate(vals, axis=0), jnp.concatenate(idxs, axis=0)


def _peer_select_kernel(h_ref, wq_ref, k1_ref, k2_ref, flat_ref, neg_ref, a_ref, b_ref, g_ref):
    tm = h_ref.shape[0]
    nk = k1_ref.shape[0]
    half = k1_ref.shape[1]
    q = jnp.dot(h_ref[...], wq_ref[...], preferred_element_type=F32)
    key_id = lax.broadcasted_iota(jnp.int32, (nk, tm), 0).astype(F32)
    flat = flat_ref[...]
    neg = neg_ref[...]
    k1 = k1_ref[...]
    k2 = k2_ref[...]
    a_all, b_all, g_all = [], [], []
    for h in range(PEER_HEADS):
        q1 = q[:, (2 * h) * half:(2 * h + 1) * half]
        q2 = q[:, (2 * h + 1) * half:(2 * h + 2) * half]
        v1, i1 = _topk_rows(_dot_nt(k1, q1, precision=HIGHEST), key_id, PEER_TOPK, nk)
        v2, i2 = _topk_rows(_dot_nt(k2, q2, precision=HIGHEST), key_id, PEER_TOPK, nk)
        vals, ai, bj = [], [], []
        for i, rows in _CAND_BLOCKS:
            if i is None:
                vals.append(v1[8:16] + v2[0:1])
                ai.append(i1[8:16])
                bj.append(jnp.broadcast_to(i2[0:1], (8, tm)))
            else:
                vals.append(v1[i:i + 1] + v2[0:rows])
                ai.append(jnp.broadcast_to(i1[i:i + 1], (rows, tm)))
                bj.append(i2[0:rows])
        cand = jnp.concatenate(vals, axis=0) + neg
        ai = jnp.concatenate(ai, axis=0)
        bj = jnp.concatenate(bj, axis=0)
        sc, sa, sb = [], [], []
        for _ in range(PEER_TOPK):
            mx = jnp.max(cand, axis=0, keepdims=True)
            fid = jnp.min(jnp.where(cand == mx, flat, 1e9), axis=0, keepdims=True)
            hit = flat == fid
            sc.append(mx)
            sa.append(jnp.sum(jnp.where(hit, ai, 0.0), axis=0, keepdims=True))
            sb.append(jnp.sum(jnp.where(hit, bj, 0.0), axis=0, keepdims=True))
            cand = jnp.where(hit, -jnp.inf, cand)
        sc = jnp.concatenate(sc, axis=0)
        e = jnp.exp(sc - sc[0:1])
        g_all.append(e / jnp.sum(e, axis=0, keepdims=True))
        a_all.append(jnp.concatenate(sa, axis=0))
        b_all.append(jnp.concatenate(sb, axis=0))
    a_ref[...] = jnp.concatenate(a_all, axis=0).T
    b_ref[...] = jnp.concatenate(b_all, axis=0).T
    g_ref[...] = jnp.concatenate(g_all, axis=0).T


def peer_select(h2, wq, k1, k2, tm=256):
    N, D = h2.shape
    tm = min(tm, N)
    flat, neg = _cand_constants(tm)
    full = lambda a: pl.BlockSpec(a.shape, lambda i: (0,) * a.ndim)
    out = jax.ShapeDtypeStruct((N, PEER_HEADS * PEER_TOPK), F32)
    ospec = pl.BlockSpec((tm, PEER_HEADS * PEER_TOPK), lambda i: (i, 0))
    return pl.pallas_call(
        _peer_select_kernel,
        grid=(N // tm,),
        in_specs=[pl.BlockSpec((tm, D), lambda i: (i, 0)), full(wq), full(k1), full(k2), full(flat), full(neg)],
        out_specs=[ospec] * 3, out_shape=[out] * 3,
        compiler_params=_cparams("parallel"),
        name="peer_select",
    )(h2, wq, k1, k2, flat, neg)


PEER_BUILD_UNROLL = 4


def _gelu(x):
    return 0.5 * x * (1.0 + lax.erf(x * 0.7071067811865476))


def _peer_expert_kernel(*refs, final):
    if final:
        h_ref, a_ref, b_ref, g_ref, u_ref, v_ref, x_ref, g2_ref, fg_ref, o_ref, w3_ref, acc_ref = refs
    else:
        h_ref, a_ref, b_ref, g_ref, u_ref, v_ref, x_ref, g2_ref, o_ref, w3_ref, acc_ref = refs
    j = pl.program_id(1)
    tm = h_ref.shape[0]
    te = u_ref.shape[0]
    nk = PEER_KEYS
    n_a = te // nk

    @pl.when(j == 0)
    def _():
        acc_ref[...] = jnp.zeros(acc_ref.shape, F32)
        key_id = lax.broadcasted_iota(jnp.int32, (nk, a_ref.shape[1]), 0).astype(F32)

        def tokens(tb, carry):
            for grp in range(PEER_BUILD_UNROLL):
                base = pl.multiple_of((tb * PEER_BUILD_UNROLL + grp) * 8, 8)
                a8 = a_ref[pl.ds(base, 8), :]
                b8 = b_ref[pl.ds(base, 8), :]
                g8 = g_ref[pl.ds(base, 8), :]
                ws = []
                for r in range(8):
                    g_row = g8[r:r + 1]
                    g_hi = g_row.astype(BF16).astype(F32)
                    g_lo = g_row - g_hi
                    eq_a = key_id == a8[r:r + 1]
                    x = jnp.concatenate([jnp.where(eq_a, g_hi, 0.0).astype(BF16),
                                         jnp.where(eq_a, g_lo, 0.0).astype(BF16)], axis=1)
                    y1 = jnp.where(key_id == b8[r:r + 1], 1.0, 0.0).astype(BF16)
                    y = jnp.concatenate([y1, y1], axis=1)
                    ws.append(_dot_nt(x, y))
                w3_ref[:, pl.ds(base, 8), :] = jnp.swapaxes(jnp.stack(ws, axis=0), 0, 1)
            return carry
        lax.fori_loop(0, tm // (8 * PEER_BUILD_UNROLL), tokens, 0)

    hid = _dot_nt(h_ref[...], u_ref[...])
    acts = []
    for al in range(n_a):
        w_a = w3_ref[j * n_a + al]
        acts.append((_gelu(hid[:, al * nk:(al + 1) * nk]) * w_a).astype(BF16))
    acc_ref[...] += jnp.dot(jnp.concatenate(acts, axis=1), v_ref[...], preferred_element_type=F32)

    @pl.when(j == pl.num_programs(1) - 1)
    def _():
        y = x_ref[...] + g2_ref[0] * acc_ref[...]
        if final:
            ms = jnp.mean(y * y, axis=-1, keepdims=True)
            y = y * lax.rsqrt(ms + RMS_EPS) * fg_ref[...]
        o_ref[...] = y


def peer_experts(h2, a_idx, b_idx, gw, u, v, x1, g2, tokens_per_batch, final_g=None, tm=256, te=2048):
    N, D = x1.shape
    E = u.shape[0]
    tm = min(tm, N)
    nb = g2.shape[0]
    tok = lambda n: pl.BlockSpec((tm, n), lambda i, j: (i, 0))
    in_specs = [tok(D), tok(a_idx.shape[1]), tok(a_idx.shape[1]), tok(a_idx.shape[1]),
                pl.BlockSpec((te, D), lambda i, j: (j, 0)), pl.BlockSpec((te, D), lambda i, j: (j, 0)),
                tok(D), pl.BlockSpec((1, 1, D), lambda i, j: (i * tm // tokens_per_batch, 0, 0))]
    args = [h2, a_idx, b_idx, gw, u, v, x1, g2.reshape(nb, 1, D)]
    if final_g is not None:
        in_specs.append(pl.BlockSpec((1, D), lambda i, j: (0, 0)))
        args.append(final_g.reshape(1, D))
    return pl.pallas_call(
        functools.partial(_peer_expert_kernel, final=final_g is not None),
        grid=(N // tm, E // te),
        in_specs=in_specs,
        out_specs=tok(D),
        out_shape=jax.ShapeDtypeStruct((N, D), F32),
        scratch_shapes=[pltpu.VMEM((PEER_KEYS, tm, PEER_KEYS), F32), pltpu.VMEM((tm, D), F32)],
        compiler_params=_cparams("parallel", "arbitrary"),
        name="peer_experts",
    )(*args)


def peer_layer(x1, h2, wq, k1, k2, u, v, g2, final_g=None):
    B, T, D = x1.shape
    h2f = h2.reshape(B * T, D)
    a_idx, b_idx, gw = peer_select(h2f, wq, k1, k2)
    out = peer_experts(h2f, a_idx, b_idx, gw, u, v, x1.reshape(B * T, D), g2, T, final_g)
    return out.reshape(B, T, D)


def kernel(x, c, ada_w, ada_b, norm1_g, norm2_g, w_in, conv_dw_w, conv_dw_b, conv_gn_g, conv_gn_b,
           rwkv_mu, rwkv_w0, rwkv_w_up, rwkv_a0, rwkv_a_up, rwkv_g_up, rwkv_k_k, rwkv_k_a, rwkv_r_k,
           rwkv_ln_g, rwkv_ln_b, nsa_ck_pos, nsa_ck_w1, nsa_ck_w2, nsa_cv_pos, nsa_cv_w1, nsa_cv_w2,
           w_out, peer_wq, peer_k1, peer_k2, peer_u, peer_v, final_g):
    depth, D = norm1_g.shape
    d_conv = conv_dw_b.shape[1]
    d_rwkv = rwkv_w0.shape[1]
    n_conv = 2 * d_conv
    n_rwkv = rwkv_mu.shape[1]
    d_nsa = D - d_conv - d_rwkv
    n_kv = 6 * NSA_KV_HEADS * HEAD_DIM
    n_gate = 3 * NSA_KV_HEADS * NSA_GROUP
    cuts = np.cumsum([0, n_conv, n_rwkv, d_nsa, n_kv, n_gate])
    mod = ada_mod(c, ada_w, ada_b)
    for l in range(depth):
        sh1, sc1, g1, sh2, sc2, g2 = [mod[l, :, i * D:(i + 1) * D] for i in range(6)]
        w_l = w_in[l].astype(BF16)
        pieces = [w_l[:, cuts[i]:cuts[i + 1]] for i in range(5)]
        pieces[4] = jnp.pad(pieces[4], ((0, 0), (0, LANES - n_gate)))
        p_conv, p_rwkv, p_q, p_kv, p_gate = proj_in(x, sh1, sc1, norm1_g[l], pieces)
        o_conv = conv_mixer(p_conv, conv_dw_w[l], conv_dw_b[l], conv_gn_g[l], conv_gn_b[l])
        o_rwkv = rwkv_mixer(p_rwkv, rwkv_mu[l], rwkv_w0[l], rwkv_w_up[l], rwkv_a0[l], rwkv_a_up[l],
                            rwkv_g_up[l], rwkv_k_k[l], rwkv_k_a[l], rwkv_r_k[l], rwkv_ln_g[l], rwkv_ln_b[l])
        o_nsa = nsa_mixer(p_q, p_kv, p_gate, nsa_ck_pos[l], nsa_ck_w1[l], nsa_ck_w2[l],
                          nsa_cv_pos[l], nsa_cv_w1[l], nsa_cv_w2[l])
        wo = w_out[l].astype(BF16)
        wo_pieces = [wo[:d_conv], wo[d_conv:d_conv + d_rwkv], wo[d_conv + d_rwkv:]]
        x1, h2 = proj_out(x, g1, sh2, sc2, norm2_g[l], [o_conv, o_rwkv, o_nsa], wo_pieces)
        x = peer_layer(x1, h2, peer_wq[l].astype(BF16), peer_k1[l], peer_k2[l],
                       peer_u[l].astype(BF16), peer_v[l].astype(BF16), g2,
                       final_g if l == depth - 1 else None)
    return x
```

```python
import functools

import jax
import jax.numpy as jnp
import numpy as np
from jax import lax
from jax.experimental import pallas as pl
from jax.experimental.pallas import tpu as pltpu

F32 = jnp.float32
BF16 = jnp.bfloat16
HIGHEST = lax.Precision.HIGHEST

HEAD_DIM = 64
CONV_WIDTH = 31
CONV_EPS = 1e-5
RWKV_GN_EPS = 64e-5
RMS_EPS = 1e-6
LORA_W = 64
LORA_A = 64
LORA_G = 128
NSA_KV_HEADS = 2
NSA_GROUP = 4
CMP_BLOCK = 32
CMP_STRIDE = 16
CMP_HIDDEN = 128
SEL_BLOCK = 64
N_SEL = 16
WINDOW = 512
FORCE_SCORE = 1e4
NEG_INF = -1e30
LOG2E = 1.4426950408889634
PEER_HEADS = 8
PEER_KEYS = 128
PEER_TOPK = 16

LANES = 128
VMEM_LIMIT = 56 * 1024 * 1024


def _dot_hi(a, b):
    return jnp.dot(a, b, precision=HIGHEST, preferred_element_type=F32)


def _dot_bf(a, b):
    return jnp.dot(a.astype(BF16), b.astype(BF16), preferred_element_type=F32)


def _cparams(*sem):
    return pltpu.CompilerParams(dimension_semantics=sem, vmem_limit_bytes=VMEM_LIMIT)


def _group_avg_matrix(n, group):
    i = np.arange(n)
    return jnp.asarray((i[:, None] // group == i[None, :] // group).astype(np.float32) / group)


def _ada_kernel(c_ref, w_ref, b_ref, o_ref):
    c = c_ref[...]
    cs = c * jax.nn.sigmoid(c)
    o_ref[0] = _dot_hi(cs, w_ref[0]) + b_ref[0]


def ada_mod(c, ada_w, ada_b):
    L, D, N = ada_w.shape
    B = c.shape[0]
    bp = 8
    cp = jnp.zeros((bp, D), F32).at[:B].set(c)
    tn = N // 4
    out = pl.pallas_call(
        _ada_kernel,
        grid=(L, N // tn),
        in_specs=[pl.BlockSpec((bp, D), lambda l, j: (0, 0)),
                  pl.BlockSpec((1, D, tn), lambda l, j: (l, 0, j)),
                  pl.BlockSpec((1, 1, tn), lambda l, j: (l, 0, j))],
        out_specs=pl.BlockSpec((1, bp, tn), lambda l, j: (l, 0, j)),
        out_shape=jax.ShapeDtypeStruct((L, bp, N), F32),
        compiler_params=_cparams("parallel", "parallel"),
        name="ada_mod",
    )(cp, ada_w, ada_b.reshape(L, 1, N))
    return out[:, :B]


def _modulated_norm(x, g, sc, sh):
    ms = jnp.mean(x * x, axis=-1, keepdims=True)
    return x * lax.rsqrt(ms + RMS_EPS) * g * (1.0 + sc) + sh


def _proj_in_kernel(x_ref, sh_ref, sc_ref, g_ref, *refs):
    n = len(refs) // 2
    h = _modulated_norm(x_ref[0], g_ref[...], sc_ref[0], sh_ref[0]).astype(BF16)
    for w_ref, o_ref in zip(refs[:n], refs[n:]):
        o_ref[0] = jnp.dot(h, w_ref[...], preferred_element_type=F32)


def proj_in(x, sh, sc, g, weights, tm=512):
    B, T, D = x.shape
    tm = min(tm, T)
    vec = pl.BlockSpec((1, 1, D), lambda b, i: (b, 0, 0))
    in_specs = [pl.BlockSpec((1, tm, D), lambda b, i: (b, i, 0)), vec, vec,
                pl.BlockSpec((1, D), lambda b, i: (0, 0))]
    in_specs += [pl.BlockSpec(w.shape, lambda b, i: (0, 0)) for w in weights]
    out_specs = [pl.BlockSpec((1, tm, w.shape[1]), lambda b, i: (b, i, 0)) for w in weights]
    out_shape = [jax.ShapeDtypeStruct((B, T, w.shape[1]), F32) for w in weights]
    return pl.pallas_call(
        _proj_in_kernel,
        grid=(B, T // tm),
        in_specs=in_specs, out_specs=out_specs, out_shape=out_shape,
        compiler_params=_cparams("parallel", "parallel"),
        name="proj_in",
    )(x, sh.reshape(B, 1, D), sc.reshape(B, 1, D), g.reshape(1, D), *weights)


CONV_HALO = 32


def _conv_kernel(p_ref, w_ref, b_ref, gg_ref, gb_ref, m_ref, o_ref, ext_ref):
    i = pl.program_id(1)
    tt, dc = o_ref.shape[1], o_ref.shape[2]

    @pl.when(i == 0)
    def _():
        ext_ref[0:CONV_HALO, :] = jnp.zeros((CONV_HALO, dc), F32)

    @pl.when(i > 0)
    def _():
        ext_ref[0:CONV_HALO, :] = ext_ref[tt:tt + CONV_HALO, :]

    p = p_ref[0]
    ext_ref[CONV_HALO:CONV_HALO + tt, :] = p[:, :dc] * jax.nn.sigmoid(p[:, dc:])
    off = CONV_HALO - (CONV_WIDTH - 1)
    acc = jnp.zeros((tt, dc), F32) + b_ref[...]
    for j in range(CONV_WIDTH):
        acc = acc + ext_ref[off + j:off + j + tt, :] * w_ref[j:j + 1, :]
    m = m_ref[...]
    mu = _dot_hi(acc, m)
    d = acc - mu
    var = _dot_hi(d * d, m)
    y = d * lax.rsqrt(var + CONV_EPS) * gg_ref[...] + gb_ref[...]
    o_ref[0] = y * jax.nn.sigmoid(y)


def conv_mixer(p, dw_w, dw_b, gn_g, gn_b, tt=512):
    B, T, two_dc = p.shape
    dc = two_dc // 2
    tt = min(tt, T)
    wpad = jnp.zeros((32, dc), F32).at[:CONV_WIDTH].set(dw_w.reshape(CONV_WIDTH, dc))
    row = pl.BlockSpec((1, dc), lambda b, i: (0, 0))
    return pl.pallas_call(
        _conv_kernel,
        grid=(B, T // tt),
        in_specs=[pl.BlockSpec((1, tt, two_dc), lambda b, i: (b, i, 0)),
                  pl.BlockSpec((32, dc), lambda b, i: (0, 0)), row, row, row,
                  pl.BlockSpec((dc, dc), lambda b, i: (0, 0))],
        out_specs=pl.BlockSpec((1, tt, dc), lambda b, i: (b, i, 0)),
        out_shape=jax.ShapeDtypeStruct((B, T, dc), F32),
        scratch_shapes=[pltpu.VMEM((tt + CONV_HALO, dc), F32)],
        compiler_params=_cparams("parallel", "arbitrary"),
        name="conv_mixer",
    )(p, wpad, dw_b.reshape(1, dc), gn_g.reshape(1, dc), gn_b.reshape(1, dc),
      _group_avg_matrix(dc, HEAD_DIM))


def _group_sum_matrix(n, group):
    i = np.arange(n)
    return jnp.asarray((i[:, None] // group == i[None, :] // group).astype(np.float32))


def _rwkv_pre_kernel(p_ref, halo_ref, mu_ref, w0_ref, wup_ref, a0_ref, aup_ref, gup_ref, kk_ref, ka_ref,
                     rk_ref, ones_ref, w_o, kk_o, b_o, k_o, v_o, r_o, bonus_o, g_o, ext_ref):
    i = pl.program_id(1)
    tt = p_ref.shape[1]
    dr = w_o.shape[2]
    p = p_ref[0]
    first = (i > 0).astype(F32)
    ext_ref[0:8, :] = halo_ref[0] * first
    ext_ref[8:8 + tt, :] = p
    prev = ext_ref[7:7 + tt, :]
    xs = p + (prev - p) * mu_ref[...]
    r = xs[:, 0:dr]
    k = xs[:, dr:2 * dr]
    v = xs[:, 2 * dr:3 * dr]
    o = 3 * dr
    xw = xs[:, o:o + LORA_W]
    xa = xs[:, o + LORA_W:o + LORA_W + LORA_A]
    xg = xs[:, o + LORA_W + LORA_A:o + LORA_W + LORA_A + LORA_G]
    z = w0_ref[...] + _dot_hi(jnp.tanh(xw), wup_ref[...])
    w_log = -jax.nn.softplus(-z) - 0.5
    decay = jnp.exp(-jnp.exp(w_log))
    a = jax.nn.sigmoid(a0_ref[...] + _dot_hi(xa, aup_ref[...]))
    g = _dot_hi(jax.nn.sigmoid(xg), gup_ref[...])
    ones = ones_ref[...]
    kk = k * kk_ref[...]
    nrm = jnp.sqrt(_dot_hi(kk * kk, ones))
    kk = kk / jnp.maximum(nrm, 1e-12)
    k_eff = k * (1.0 + (a - 1.0) * ka_ref[...])
    bonus = _dot_hi(r * k_eff * rk_ref[...], ones) * v
    w_o[0] = decay
    kk_o[0] = kk
    b_o[0] = kk * a
    k_o[0] = k_eff
    v_o[0] = v
    r_o[0] = r
    bonus_o[0] = bonus
    g_o[0] = g


def rwkv_pre(p, mu, w0, w_up, a0, a_up, g_up, k_k, k_a, r_k, tt=512):
    B, T, n_in = p.shape
    dr = w0.shape[0]
    tt = min(tt, T)
    nb8 = tt // 8
    full = lambda a: pl.BlockSpec(a.shape, lambda b, i: (0,) * a.ndim)
    args = [mu.reshape(1, n_in), w0.reshape(1, dr), w_up, a0.reshape(1, dr), a_up, g_up,
            k_k.reshape(1, dr), k_a.reshape(1, dr), r_k.reshape(1, dr), _group_sum_matrix(dr, HEAD_DIM)]
    out = jax.ShapeDtypeStruct((B, T, dr), F32)
    ospec = pl.BlockSpec((1, tt, dr), lambda b, i: (b, i, 0))
    return pl.pallas_call(
        _rwkv_pre_kernel,
        grid=(B, T // tt),
        in_specs=[pl.BlockSpec((1, tt, n_in), lambda b, i: (b, i, 0)),
                  pl.BlockSpec((1, 8, n_in), lambda b, i: (b, jnp.maximum(i * nb8 - 1, 0), 0))]
                 + [full(a) for a in args],
        out_specs=[ospec] * 8, out_shape=[out] * 8,
        scratch_shapes=[pltpu.VMEM((tt + 8, n_in), F32)],
        compiler_params=_cparams("parallel", "parallel"),
        name="rwkv_pre",
    )(p, p, *args)


SCAN_SUB = 64
SCAN_CHUNK = 128


def _scan_select_matrices():
    e = np.zeros((SCAN_SUB // 2, 4 * SCAN_SUB, 2 * LANES), np.float32)
    for t in range(SCAN_SUB):
        c = (t % 2) * LANES
        for part in range(2):
            e[t // 2, 2 * part * SCAN_SUB + t, c:c + HEAD_DIM] = 1.0
            e[t // 2, (2 * part + 1) * SCAN_SUB + t, c + HEAD_DIM:c + LANES] = 1.0
    return jnp.asarray(e, BF16)


def _rwkv_scan_kernel(w_ref, kk_ref, b_ref, k_ref, r_ref, v_ref, bonus_ref, g_ref, lng_ref, lnb_ref,
                      e_ref, avg_ref, o_ref, st_ref, lhs_ref, y_ref):
    nb, tc, dr = v_ref.shape
    n_half = dr // LANES
    n_pair = nb * n_half
    n_sub = tc // SCAN_SUB
    quantities = (w_ref, kk_ref, b_ref, k_ref, r_ref)

    @pl.when(pl.program_id(0) == 0)
    def _():
        st_ref[...] = jnp.zeros(st_ref.shape, F32)

    for qi, q_ref in enumerate(quantities):
        for p in range(n_pair):
            b, hb = divmod(p, n_half)
            xt = q_ref[b, :, hb * LANES:(hb + 1) * LANES].T
            for s in range(n_sub):
                cat = jnp.concatenate([xt[0:HEAD_DIM, s * SCAN_SUB:(s + 1) * SCAN_SUB],
                                       xt[HEAD_DIM:, s * SCAN_SUB:(s + 1) * SCAN_SUB]], axis=1)
                hi = cat.astype(BF16)
                lo = (cat - hi.astype(F32)).astype(BF16)
                row = (qi * n_pair + p) * HEAD_DIM
                lhs_ref[s, row:row + HEAD_DIM, :] = jnp.concatenate([hi, lo], axis=1)

    sub_iota = lax.broadcasted_iota(jnp.int32, (8, LANES), 0)
    for s in range(n_sub):
        def steps8(t8, carry, s=s):
            base = pl.multiple_of(s * SCAN_SUB + t8 * 8, 8)
            y8 = [jnp.zeros((8, LANES), F32) for _ in range(n_pair)]
            for j in range(8):
                if j % 2 == 0:
                    z2 = jnp.dot(lhs_ref[s], e_ref[t8 * 4 + j // 2], preferred_element_type=F32)
                z = z2[:, (j % 2) * LANES:(j % 2 + 1) * LANES]
                for p in range(n_pair):
                    b, hb = divmod(p, n_half)
                    col = lambda qi: z[(qi * n_pair + p) * HEAD_DIM:(qi * n_pair + p + 1) * HEAD_DIM, :]
                    st = st_ref[p]
                    sa = jnp.sum(st * col(1), axis=0, keepdims=True)
                    v_row = v_ref[b, pl.ds(base, 8), hb * LANES:(hb + 1) * LANES][j:j + 1, :]
                    st = st * col(0) - col(2) * sa + col(3) * v_row
                    y = jnp.sum(st * col(4), axis=0, keepdims=True)
                    y8[p] = jnp.where(sub_iota == j, y, y8[p])
                    st_ref[p] = st
            for p in range(n_pair):
                b, hb = divmod(p, n_half)
                y_ref[b, pl.ds(base, 8), hb * LANES:(hb + 1) * LANES] = y8[p]
            return carry
        lax.fori_loop(0, SCAN_SUB // 8, steps8, 0)

    avg = avg_ref[...]
    for b in range(nb):
        y = y_ref[b]
        mu = _dot_hi(y, avg)
        d = y - mu
        var = _dot_hi(d * d, avg)
        yn = d * lax.rsqrt(var + RWKV_GN_EPS) * lng_ref[...] + lnb_ref[...]
        o_ref[b] = (yn + bonus_ref[b]) * g_ref[b]


def rwkv_scan(w, kk, bq, k, r, v, bonus, g, ln_g, ln_b):
    B, T, dr = v.shape
    tc = min(SCAN_CHUNK, T)
    n_pair = B * dr // LANES
    blk = pl.BlockSpec((B, tc, dr), lambda i: (0, i, 0))
    row = pl.BlockSpec((1, dr), lambda i: (0, 0))
    e = _scan_select_matrices()
    return pl.pallas_call(
        _rwkv_scan_kernel,
        grid=(T // tc,),
        in_specs=[blk] * 8 + [row, row, pl.BlockSpec(e.shape, lambda i: (0, 0, 0)),
                              pl.BlockSpec((dr, dr), lambda i: (0, 0))],
        out_specs=blk,
        out_shape=jax.ShapeDtypeStruct((B, T, dr), F32),
        scratch_shapes=[pltpu.VMEM((n_pair, HEAD_DIM, LANES), F32),
                        pltpu.VMEM((tc // SCAN_SUB, 5 * n_pair * HEAD_DIM, 4 * SCAN_SUB), BF16),
                        pltpu.VMEM((B, tc, dr), F32)],
        compiler_params=_cparams("arbitrary"),
        name="rwkv_scan",
    )(w, kk, bq, k, r, v, bonus, g, ln_g.reshape(1, dr), ln_b.reshape(1, dr), e, _group_avg_matrix(dr, HEAD_DIM))


def rwkv_mixer(p, mu, w0, w_up, a0, a_up, g_up, k_k, k_a, r_k, ln_g, ln_b):
    w, kk, bq, k, v, r, bonus, g = rwkv_pre(p, mu, w0, w_up, a0, a_up, g_up, k_k, k_a, r_k.reshape(-1))
    return rwkv_scan(w, kk, bq, k, r, v, bonus, g, ln_g, ln_b)


def _dot_nt(a, b, **kw):
    return lax.dot_general(a, b, (((1,), (1,)), ((), ())), preferred_element_type=F32, **kw)


def _compress_kernel(kz_ref, vz_ref, kpos_ref, kw1_ref, kw2_ref, vpos_ref, vw1_ref, vw2_ref,
                     kc_ref, vc_ref, shift_ref):
    n = kz_ref.shape[2]
    half = kz_ref.shape[3]

    def one(z_ref, pos_ref, w1_ref, w2_ref, o_ref):
        z = z_ref[0, 0]
        top = _dot_bf(z + pos_ref[0:1, :], w1_ref[0:half, :])
        bot = _dot_bf(z + pos_ref[1:2, :], w1_ref[half:2 * half, :])
        shift_ref[0:n, :] = bot
        shift_ref[n:n + 8, :] = jnp.zeros((8, bot.shape[1]), F32)
        pre = top + shift_ref[1:n + 1, :]
        hid = pre * jax.nn.sigmoid(pre)
        o_ref[0, 0] = _dot_bf(hid, w2_ref[...])

    one(kz_ref, kpos_ref, kw1_ref, kw2_ref, kc_ref)
    one(vz_ref, vpos_ref, vw1_ref, vw2_ref, vc_ref)


def nsa_compress(k_cmp, v_cmp, ck_pos, ck_w1, ck_w2, cv_pos, cv_w1, cv_w2):
    B, G, T, dk = k_cmp.shape
    n = T // CMP_STRIDE
    half = CMP_STRIDE * dk
    zspec = pl.BlockSpec((1, 1, n, half), lambda b, g: (b, g, 0, 0))
    full = lambda a: pl.BlockSpec(a.shape, lambda b, g: (0,) * a.ndim)
    ospec = pl.BlockSpec((1, 1, n, dk), lambda b, g: (b, g, 0, 0))
    args = [ck_pos.reshape(2, half), ck_w1.astype(BF16), ck_w2.astype(BF16),
            cv_pos.reshape(2, half), cv_w1.astype(BF16), cv_w2.astype(BF16)]
    return pl.pallas_call(
        _compress_kernel,
        grid=(B, G),
        in_specs=[zspec, zspec] + [full(a) for a in args],
        out_specs=[ospec, ospec],
        out_shape=[jax.ShapeDtypeStruct((B, G, n, dk), F32)] * 2,
        scratch_shapes=[pltpu.VMEM((n + 8, CMP_HIDDEN), F32)],
        compiler_params=_cparams("parallel", "parallel"),
        name="nsa_compress",
    )(k_cmp.reshape(B, G, n, half), v_cmp.reshape(B, G, n, half), *args)


NSA_TQ = 128
NSA_TK = 1024


def _split_bf16(a):
    hi = a.astype(BF16)
    return hi, (a - hi.astype(F32)).astype(BF16)


def _dot_3pass(a, b):
    ah, al = _split_bf16(a)
    bh, bl = _split_bf16(b)
    d = lambda x, y: jnp.dot(x, y, preferred_element_type=F32)
    return d(ah, bh) + d(ah, bl) + d(al, bh)


def _nsa_kernel(q_ref, gate_ref, kc_ref, vct_ref, ks_ref, vst_ref, kw_ref, vwt_ref, ovt_ref,
                o_ref, m_ref, acc_ref, ow_ref, sel_ref, s0_ref, s1_ref, p_ref):
    qi = pl.program_id(2)
    dk = q_ref.shape[3]
    tq = gate_ref.shape[3]
    R = q_ref.shape[4] // tq
    n_c = kc_ref.shape[2]
    n_blk = ovt_ref.shape[0]
    T = ks_ref.shape[2]
    tk = s0_ref.shape[0]
    t0 = qi * tq
    qt = q_ref[0, 0, 0]
    qt_bf = qt.astype(BF16)
    t_row = t0 + lax.broadcasted_iota(jnp.int32, (1, tq), 1)
    lanes = lambda r: slice(r * tq, (r + 1) * tq)

    s = _dot_3pass(kc_ref[0, 0], qt)
    c_end = lax.broadcasted_iota(jnp.int32, (n_c, tq), 0) * CMP_STRIDE + (CMP_BLOCK - 1)
    valid_c = c_end <= t_row
    any_c = (t_row >= CMP_BLOCK - 1).astype(F32)
    p_sum = jnp.zeros((n_c, tq), F32)
    ps = []
    for r in range(R):
        s_r = jnp.where(valid_c, s[:, lanes(r)], NEG_INF)
        e = jnp.exp2(s_r - jnp.max(s_r, axis=0, keepdims=True))
        p_r = e * (any_c / jnp.sum(e, axis=0, keepdims=True))
        p_sum = p_sum + p_r
        ps.append(p_r.astype(BF16))
    o_c = jnp.dot(vct_ref[0, 0].astype(BF16), jnp.concatenate(ps, axis=1), preferred_element_type=F32)

    imp = _dot_hi(ovt_ref[...], p_sum)
    blk = lax.broadcasted_iota(jnp.int32, (n_blk, tq), 0).astype(F32)
    cur = (t_row // SEL_BLOCK).astype(F32)
    forced = (blk == 0.0) | (blk == cur) | (blk == cur - 1.0)
    x = jnp.where(forced, FORCE_SCORE, jnp.where(blk <= cur, imp, -1.0))
    x = jnp.where(blk < float(T // SEL_BLOCK), x, -3e38)
    sel = jnp.zeros((n_blk, tq), F32)
    for _ in range(N_SEL):
        mx = jnp.max(x, axis=0, keepdims=True)
        idx = jnp.min(jnp.where(x == mx, blk, float(n_blk)), axis=0, keepdims=True)
        hit = blk == idx
        sel = jnp.where(hit, 1.0, sel)
        x = jnp.where(hit, -jnp.inf, x)

    span = WINDOW + tq
    w0 = pl.multiple_of(jnp.maximum(t0 - WINDOW, 0), tq)
    s = jnp.dot(kw_ref[0, 0, pl.ds(w0, span), :], qt_bf, preferred_element_type=F32)
    dist = t_row - (w0 + lax.broadcasted_iota(jnp.int32, (span, tq), 0))
    bias_w = jnp.where((dist >= 0) & (dist < WINDOW), 0.0, NEG_INF)
    ps = []
    for r in range(R):
        s_r = s[:, lanes(r)] + bias_w
        ps.append(jnp.exp2(s_r - jnp.max(s_r, axis=0, keepdims=True)).astype(BF16))
    ow_ref[...] = jnp.dot(vwt_ref[0, 0, :, pl.ds(w0, span)], jnp.concatenate(ps, axis=1),
                          preferred_element_type=F32)

    bpt = tk // SEL_BLOCK
    n_kt = T // tk
    n_full = (t0 + tq - 1) // tk
    sel_ref[...] = jnp.where(sel > 0.5, 0.0, NEG_INF)
    m_ref[...] = jnp.full(m_ref.shape, NEG_INF, F32)
    acc_ref[...] = jnp.zeros(acc_ref.shape, F32)
    q_pad = jnp.zeros((ks_ref.shape[3] - dk - bpt, R * tq), BF16)

    def scores(kt):
        kt = jnp.minimum(kt, n_kt - 1)
        rows = sel_ref[pl.ds(pl.multiple_of(kt * bpt, bpt), bpt), :].astype(BF16)
        q_aug = jnp.concatenate([qt_bf, jnp.concatenate([rows] * R, axis=1), q_pad], axis=0)
        off = pl.multiple_of(kt * tk, tk)
        return jnp.dot(ks_ref[0, 0, pl.ds(off, tk), :], q_aug, preferred_element_type=F32)

    def softmax_tile(s_ref, kt, causal):
        if causal:
            key_pos = kt * tk + lax.broadcasted_iota(jnp.int32, (tk, tq), 0)
            bias = jnp.where(key_pos <= t_row, 0.0, NEG_INF)
        for r in range(R):
            m_old = m_ref[:, lanes(r)]
            if causal:
                m_new = jnp.maximum(m_old, jnp.max(s_ref[:, lanes(r)] + bias, axis=0, keepdims=True))
                p_r = jnp.exp2((s_ref[:, lanes(r)] - m_new) + bias)
            else:
                m_new = jnp.maximum(m_old, jnp.max(s_ref[:, lanes(r)], axis=0, keepdims=True))
                p_r = jnp.exp2(s_ref[:, lanes(r)] - m_new)
            m_ref[:, lanes(r)] = m_new
            p_ref[:, lanes(r)] = p_r.astype(BF16)
            acc_ref[:, lanes(r)] = acc_ref[:, lanes(r)] * jnp.exp2(m_old - m_new)
        off = pl.multiple_of(kt * tk, tk)
        acc_ref[...] += jnp.dot(vst_ref[0, 0, :, pl.ds(off, tk)], p_ref[...], preferred_element_type=F32)

    s0_ref[...] = scores(0)

    def tile_pair(i, carry):
        s1_ref[...] = scores(2 * i + 1)
        softmax_tile(s0_ref, 2 * i, False)
        s0_ref[...] = scores(2 * i + 2)
        softmax_tile(s1_ref, 2 * i + 1, False)
        return carry

    lax.fori_loop(0, n_full // 2, tile_pair, 0)
    odd = n_full % 2 == 1

    @pl.when(odd)
    def _():
        s1_ref[...] = scores(n_full)
        softmax_tile(s0_ref, n_full - 1, False)
        softmax_tile(s1_ref, n_full, True)

    @pl.when(jnp.logical_not(odd))
    def _():
        softmax_tile(s0_ref, n_full, True)

    acc = acc_ref[...]
    o_s = acc[0:dk] * (1.0 / acc[dk:dk + 1])
    o_w = ow_ref[...]
    o_w = o_w[0:dk] * (1.0 / o_w[dk:dk + 1])

    gate = jax.nn.sigmoid(gate_ref[0, 0])
    outs = []
    for r in range(R):
        outs.append(gate[r:r + 1] * o_c[:, lanes(r)] + gate[R + r:R + r + 1] * o_s[:, lanes(r)]
                    + gate[2 * R + r:2 * R + r + 1] * o_w[:, lanes(r)])
    o_ref[0] = jnp.concatenate(outs, axis=0).T


def _overlap_matrix_t(n_c, n_blk):
    c0 = np.arange(n_c)[None, :] * CMP_STRIDE
    s0 = np.arange(n_blk)[:, None] * SEL_BLOCK
    return jnp.asarray(((c0 < s0 + SEL_BLOCK) & (c0 + CMP_BLOCK > s0)).astype(np.float32))


def nsa_attention(qt, gate_t, kc, vct, k_sel, vt_sel, k_win, vt_win):
    B, G, nq, dk, L = qt.shape
    T = k_win.shape[2]
    tq = T // nq
    R = L // tq
    n_c = kc.shape[2]
    tk = min(NSA_TK, T)
    n_blk = max(T // SEL_BLOCK, LANES)
    ovt = _overlap_matrix_t(n_c, n_blk)
    res = lambda a: pl.BlockSpec((1, 1) + a.shape[2:], lambda b, g, i: (b, g, 0, 0))
    return pl.pallas_call(
        _nsa_kernel,
        grid=(B, G, nq),
        in_specs=[pl.BlockSpec((1, 1, 1, dk, L), lambda b, g, i: (b, g, i, 0, 0)),
                  pl.BlockSpec((1, 1, gate_t.shape[2], tq), lambda b, g, i: (b, g, 0, i)),
                  res(kc), res(vct), res(k_sel), res(vt_sel), res(k_win), res(vt_win),
                  pl.BlockSpec(ovt.shape, lambda b, g, i: (0, 0))],
        out_specs=pl.BlockSpec((1, tq, R * dk), lambda b, g, i: (b, i, g)),
        out_shape=jax.ShapeDtypeStruct((B, T, G * R * dk), F32),
        scratch_shapes=[pltpu.VMEM((1, L), F32), pltpu.VMEM((dk + 16, L), F32),
                        pltpu.VMEM((dk + 16, L), F32), pltpu.VMEM((n_blk, tq), F32),
                        pltpu.VMEM((tk, L), F32), pltpu.VMEM((tk, L), F32), pltpu.VMEM((tk, L), BF16)],
        compiler_params=_cparams("parallel", "parallel", "arbitrary"),
        name="nsa_attention",
    )(qt, gate_t, kc, vct, k_sel, vt_sel, k_win, vt_win, ovt)


def nsa_mixer(p_q, p_kv, p_gate, ck_pos, ck_w1, ck_w2, cv_pos, cv_w1, cv_w2):
    B, T, _ = p_q.shape
    G, R, dk = NSA_KV_HEADS, NSA_GROUP, HEAD_DIM
    tq = min(NSA_TQ, T)
    kv6 = p_kv.reshape(B, T, 6, G, dk)
    rows = lambda i: kv6[:, :, i].transpose(0, 2, 1, 3)
    cols = lambda i: kv6[:, :, i].transpose(0, 2, 3, 1)
    kc, vc = nsa_compress(rows(0), rows(1), ck_pos, ck_w1, ck_w2, cv_pos, cv_w1, cv_w2)
    qt = (p_q * (dk ** -0.5 * LOG2E)).reshape(B, T // tq, tq, G, R, dk).transpose(0, 3, 1, 5, 4, 2)
    qt = qt.reshape(B, G, T // tq, dk, R * tq)
    gate_t = p_gate[..., :3 * G * R].reshape(B, T, 3, G, R).transpose(0, 3, 2, 4, 1).reshape(B, G, 3 * R, T)
    gate_t = jnp.pad(gate_t, ((0, 0), (0, 0), (0, 16 - 3 * R), (0, 0)))
    tk = min(NSA_TK, T)
    blk_onehot = (jnp.arange(T)[:, None] // SEL_BLOCK % (tk // SEL_BLOCK) == jnp.arange(dk)[None, :]).astype(BF16)
    k_sel = jnp.concatenate([rows(2).astype(BF16), jnp.broadcast_to(blk_onehot, (B, G, T, dk))], axis=-1)
    ones_rows = jnp.zeros((B, G, 16, T), BF16).at[:, :, 0].set(1.0)
    with_ones = lambda vt: jnp.concatenate([vt.astype(BF16), ones_rows], axis=2)
    return nsa_attention(qt, gate_t, kc, vc.transpose(0, 1, 3, 2), k_sel, with_ones(cols(3)),
                         rows(4).astype(BF16), with_ones(cols(5)))


def _proj_out_kernel(x_ref, g1_ref, sh_ref, sc_ref, ng_ref, *refs):
    n = (len(refs) - 2) // 2
    x1_ref, h2_ref = refs[2 * n:]
    acc = None
    for m_ref, w_ref in zip(refs[:n], refs[n:2 * n]):
        d = jnp.dot(m_ref[0].astype(BF16), w_ref[...], preferred_element_type=F32)
        acc = d if acc is None else acc + d
    x1 = x_ref[0] + g1_ref[0] * acc
    x1_ref[0] = x1
    h2_ref[0] = _modulated_norm(x1, ng_ref[...], sc_ref[0], sh_ref[0]).astype(BF16)


def proj_out(x, g1, sh2, sc2, norm_g, mixes, weights, tm=512):
    B, T, D = x.shape
    tm = min(tm, T)
    vec = pl.BlockSpec((1, 1, D), lambda b, i: (b, 0, 0))
    tile = lambda n: pl.BlockSpec((1, tm, n), lambda b, i: (b, i, 0))
    return pl.pallas_call(
        _proj_out_kernel,
        grid=(B, T // tm),
        in_specs=[tile(D), vec, vec, vec, pl.BlockSpec((1, D), lambda b, i: (0, 0))]
                 + [tile(m.shape[2]) for m in mixes]
                 + [pl.BlockSpec(w.shape, lambda b, i: (0, 0)) for w in weights],
        out_specs=[tile(D), tile(D)],
        out_shape=[jax.ShapeDtypeStruct((B, T, D), F32), jax.ShapeDtypeStruct((B, T, D), BF16)],
        compiler_params=_cparams("parallel", "parallel"),
        name="proj_out",
    )(x, g1.reshape(B, 1, D), sh2.reshape(B, 1, D), sc2.reshape(B, 1, D), norm_g.reshape(1, D), *mixes, *weights)


_CAND_BLOCKS = [(0, 16), (1, 8), (None, 8), (2, 5), (3, 4), (4, 3), (5, 2), (6, 2), (7, 2)]
_CAND_PAD = 6


def _cand_constants(tm):
    flat, neg = [], []
    for i, rows in _CAND_BLOCKS:
        for r in range(rows):
            flat.append((8 + r) * PEER_TOPK if i is None else i * PEER_TOPK + r)
            neg.append(0.0)
    flat += [1e9] * _CAND_PAD
    neg += [-np.inf] * _CAND_PAD
    flat = np.tile(np.asarray(flat, np.float32)[:, None], (1, tm))
    neg = np.tile(np.asarray(neg, np.float32)[:, None], (1, tm))
    return jnp.asarray(flat), jnp.asarray(neg)


def _topk_rows(x, row_id, k, n_rows):
    vals, idxs = [], []
    for _ in range(k):
        mx = jnp.max(x, axis=0, keepdims=True)
        idx = jnp.min(jnp.where(x == mx, row_id, float(n_rows)), axis=0, keepdims=True)
        x = jnp.where(row_id == idx, -jnp.inf, x)
        vals.append(mx)
        idxs.append(idx)
    return jnp.concatenate(vals, axis=0), jnp.concatenate(idxs, axis=0)


def _peer_select_kernel(h_ref, wq_ref, k1_ref, k2_ref, flat_ref, neg_ref, a_ref, b_ref, g_ref):
    tm = h_ref.shape[0]
    nk = k1_ref.shape[0]
    half = k1_ref.shape[1]
    q = jnp.dot(h_ref[...], wq_ref[...], preferred_element_type=F32)
    key_id = lax.broadcasted_iota(jnp.int32, (nk, tm), 0).astype(F32)
    flat = flat_ref[...]
    neg = neg_ref[...]
    k1 = k1_ref[...]
    k2 = k2_ref[...]
    a_all, b_all, g_all = [], [], []
    for h in range(PEER_HEADS):
        q1 = q[:, (2 * h) * half:(2 * h + 1) * half]
        q2 = q[:, (2 * h + 1) * half:(2 * h + 2) * half]
        v1, i1 = _topk_rows(_dot_nt(k1, q1, precision=HIGHEST), key_id, PEER_TOPK, nk)
        v2, i2 = _topk_rows(_dot_nt(k2, q2, precision=HIGHEST), key_id, PEER_TOPK, nk)
        vals, ai, bj = [], [], []
        for i, rows in _CAND_BLOCKS:
            if i is None:
                vals.append(v1[8:16] + v2[0:1])
                ai.append(i1[8:16])
                bj.append(jnp.broadcast_to(i2[0:1], (8, tm)))
            else:
                vals.append(v1[i:i + 1] + v2[0:rows])
                ai.append(jnp.broadcast_to(i1[i:i + 1], (rows, tm)))
                bj.append(i2[0:rows])
        pad = [jnp.zeros((_CAND_PAD, tm), F32)]
        cand = jnp.concatenate(vals + pad, axis=0) + neg
        ai = jnp.concatenate(ai + pad, axis=0)
        bj = jnp.concatenate(bj + pad, axis=0)
        sc, sa, sb = [], [], []
        for _ in range(PEER_TOPK):
            mx = jnp.max(cand, axis=0, keepdims=True)
            fid = jnp.min(jnp.where(cand == mx, flat, 1e9), axis=0, keepdims=True)
            hit = flat == fid
            sc.append(mx)
            sa.append(jnp.sum(jnp.where(hit, ai, 0.0), axis=0, keepdims=True))
            sb.append(jnp.sum(jnp.where(hit, bj, 0.0), axis=0, keepdims=True))
            cand = jnp.where(hit, -jnp.inf, cand)
        sc = jnp.concatenate(sc, axis=0)
        e = jnp.exp(sc - sc[0:1])
        g_all.append(e / jnp.sum(e, axis=0, keepdims=True))
        a_all.append(jnp.concatenate(sa, axis=0))
        b_all.append(jnp.concatenate(sb, axis=0))
    a_ref[...] = jnp.concatenate(a_all, axis=0).T
    b_ref[...] = jnp.concatenate(b_all, axis=0).T
    g_ref[...] = jnp.concatenate(g_all, axis=0).T


def peer_select(h2, wq, k1, k2, tm=256):
    N, D = h2.shape
    tm = min(tm, N)
    flat, neg = _cand_constants(tm)
    full = lambda a: pl.BlockSpec(a.shape, lambda i: (0,) * a.ndim)
    out = jax.ShapeDtypeStruct((N, PEER_HEADS * PEER_TOPK), F32)
    ospec = pl.BlockSpec((tm, PEER_HEADS * PEER_TOPK), lambda i: (i, 0))
    return pl.pallas_call(
        _peer_select_kernel,
        grid=(N // tm,),
        in_specs=[pl.BlockSpec((tm, D), lambda i: (i, 0)), full(wq), full(k1), full(k2), full(flat), full(neg)],
        out_specs=[ospec] * 3, out_shape=[out] * 3,
        compiler_params=_cparams("parallel"),
        name="peer_select",
    )(h2, wq, k1, k2, flat, neg)


PEER_BUILD_UNROLL = 4


def _gelu(x):
    return 0.5 * x * (1.0 + lax.erf(x * 0.7071067811865476))


def _peer_expert_kernel(*refs, final, n_e):
    if final:
        h_ref, a_ref, b_ref, g_ref, u_ref, v_ref, x_ref, g2_ref, fg_ref, o_ref, w3_ref, acc_ref, ha_ref, hb_ref = refs
    else:
        h_ref, a_ref, b_ref, g_ref, u_ref, v_ref, x_ref, g2_ref, o_ref, w3_ref, acc_ref, ha_ref, hb_ref = refs
    j = pl.program_id(1)
    tm = h_ref.shape[0]
    te = u_ref.shape[0]
    nk = PEER_KEYS
    n_a = te // nk

    def hidden():
        return _dot_nt(h_ref[...], u_ref[...])

    def finish(hid_ref, tile):
        acts = []
        for al in range(n_a):
            w_a = w3_ref[tile * n_a + al]
            acts.append((_gelu(hid_ref[:, al * nk:(al + 1) * nk]) * w_a).astype(BF16))
        acc_ref[...] += jnp.dot(jnp.concatenate(acts, axis=1), v_ref[...], preferred_element_type=F32)

    @pl.when(j == 0)
    def _():
        acc_ref[...] = jnp.zeros(acc_ref.shape, F32)
        ha_ref[...] = hidden()
        key_id = lax.broadcasted_iota(jnp.int32, (nk, a_ref.shape[1]), 0).astype(F32)

        def tokens(tb, carry):
            for grp in range(PEER_BUILD_UNROLL):
                base = pl.multiple_of((tb * PEER_BUILD_UNROLL + grp) * 8, 8)
                a8 = a_ref[pl.ds(base, 8), :]
                b8 = b_ref[pl.ds(base, 8), :]
                g8 = g_ref[pl.ds(base, 8), :]
                ws = []
                for r in range(8):
                    g_row = g8[r:r + 1]
                    g_hi = g_row.astype(BF16).astype(F32)
                    g_lo = g_row - g_hi
                    eq_a = key_id == a8[r:r + 1]
                    x = jnp.concatenate([jnp.where(eq_a, g_hi, 0.0).astype(BF16),
                                         jnp.where(eq_a, g_lo, 0.0).astype(BF16)], axis=1)
                    y1 = jnp.where(key_id == b8[r:r + 1], 1.0, 0.0).astype(BF16)
                    y = jnp.concatenate([y1, y1], axis=1)
                    ws.append(_dot_nt(x, y))
                w3_ref[:, pl.ds(base, 8), :] = jnp.swapaxes(jnp.stack(ws, axis=0), 0, 1)
            return carry
        lax.fori_loop(0, tm // (8 * PEER_BUILD_UNROLL), tokens, 0)

    for parity, (h_write, h_read) in enumerate(((ha_ref, hb_ref), (hb_ref, ha_ref))):
        @pl.when((j > 0) & (j < n_e) & (j % 2 == parity))
        def _(h_write=h_write, h_read=h_read):
            h_write[...] = hidden()
            finish(h_read, j - 1)

    @pl.when(j == n_e)
    def _():
        finish(hb_ref if (n_e - 1) % 2 else ha_ref, n_e - 1)
        y = x_ref[...] + g2_ref[0] * acc_ref[...]
        if final:
            ms = jnp.mean(y * y, axis=-1, keepdims=True)
            y = y * lax.rsqrt(ms + RMS_EPS) * fg_ref[...]
        o_ref[...] = y


def peer_experts(h2, a_idx, b_idx, gw, u, v, x1, g2, tokens_per_batch, final_g=None, tm=256, te=2048):
    N, D = x1.shape
    E = u.shape[0]
    tm = min(tm, N)
    nb = g2.shape[0]
    n_e = E // te
    tok = lambda n: pl.BlockSpec((tm, n), lambda i, j: (i, 0))
    in_specs = [tok(D), tok(a_idx.shape[1]), tok(a_idx.shape[1]), tok(a_idx.shape[1]),
                pl.BlockSpec((te, D), lambda i, j: (jnp.minimum(j, n_e - 1), 0)),
                pl.BlockSpec((te, D), lambda i, j: (jnp.maximum(j - 1, 0), 0)),
                tok(D), pl.BlockSpec((1, 1, D), lambda i, j: (i * tm // tokens_per_batch, 0, 0))]
    args = [h2, a_idx, b_idx, gw, u, v, x1, g2.reshape(nb, 1, D)]
    if final_g is not None:
        in_specs.append(pl.BlockSpec((1, D), lambda i, j: (0, 0)))
        args.append(final_g.reshape(1, D))
    return pl.pallas_call(
        functools.partial(_peer_expert_kernel, final=final_g is not None, n_e=n_e),
        grid=(N // tm, n_e + 1),
        in_specs=in_specs,
        out_specs=tok(D),
        out_shape=jax.ShapeDtypeStruct((N, D), F32),
        scratch_shapes=[pltpu.VMEM((PEER_KEYS, tm, PEER_KEYS), F32), pltpu.VMEM((tm, D), F32),
                        pltpu.VMEM((tm, te), F32), pltpu.VMEM((tm, te), F32)],
        compiler_params=_cparams("parallel", "arbitrary"),
        name="peer_experts",
    )(*args)


def peer_layer(x1, h2, wq, k1, k2, u, v, g2, final_g=None):
    B, T, D = x1.shape
    h2f = h2.reshape(B * T, D)
    a_idx, b_idx, gw = peer_select(h2f, wq, k1, k2)
    out = peer_experts(h2f, a_idx, b_idx, gw, u, v, x1.reshape(B * T, D), g2, T, final_g)
    return out.reshape(B, T, D)


def kernel(x, c, ada_w, ada_b, norm1_g, norm2_g, w_in, conv_dw_w, conv_dw_b, conv_gn_g, conv_gn_b,
           rwkv_mu, rwkv_w0, rwkv_w_up, rwkv_a0, rwkv_a_up, rwkv_g_up, rwkv_k_k, rwkv_k_a, rwkv_r_k,
           rwkv_ln_g, rwkv_ln_b, nsa_ck_pos, nsa_ck_w1, nsa_ck_w2, nsa_cv_pos, nsa_cv_w1, nsa_cv_w2,
           w_out, peer_wq, peer_k1, peer_k2, peer_u, peer_v, final_g):
    depth, D = norm1_g.shape
    d_conv = conv_dw_b.shape[1]
    d_rwkv = rwkv_w0.shape[1]
    n_conv = 2 * d_conv
    n_rwkv = rwkv_mu.shape[1]
    d_nsa = D - d_conv - d_rwkv
    n_kv = 6 * NSA_KV_HEADS * HEAD_DIM
    n_gate = 3 * NSA_KV_HEADS * NSA_GROUP
    cuts = np.cumsum([0, n_conv, n_rwkv, d_nsa, n_kv, n_gate])
    mod = ada_mod(c, ada_w, ada_b)
    for l in range(depth):
        sh1, sc1, g1, sh2, sc2, g2 = [mod[l, :, i * D:(i + 1) * D] for i in range(6)]
        w_l = w_in[l].astype(BF16)
        pieces = [w_l[:, cuts[i]:cuts[i + 1]] for i in range(5)]
        pieces[4] = jnp.pad(pieces[4], ((0, 0), (0, LANES - n_gate)))
        p_conv, p_rwkv, p_q, p_kv, p_gate = proj_in(x, sh1, sc1, norm1_g[l], pieces)
        o_conv = conv_mixer(p_conv, conv_dw_w[l], conv_dw_b[l], conv_gn_g[l], conv_gn_b[l])
        o_rwkv = rwkv_mixer(p_rwkv, rwkv_mu[l], rwkv_w0[l], rwkv_w_up[l], rwkv_a0[l], rwkv_a_up[l],
                            rwkv_g_up[l], rwkv_k_k[l], rwkv_k_a[l], rwkv_r_k[l], rwkv_ln_g[l], rwkv_ln_b[l])
        o_nsa = nsa_mixer(p_q, p_kv, p_gate, nsa_ck_pos[l], nsa_ck_w1[l], nsa_ck_w2[l],
                          nsa_cv_pos[l], nsa_cv_w1[l], nsa_cv_w2[l])
        wo = w_out[l].astype(BF16)
        wo_pieces = [wo[:d_conv], wo[d_conv:d_conv + d_rwkv], wo[d_conv + d_rwkv:]]
        x1, h2 = proj_out(x, g1, sh2, sc2, norm2_g[l], [o_conv, o_rwkv, o_nsa], wo_pieces)
        x = peer_layer(x1, h2, peer_wq[l].astype(BF16), peer_k1[l], peer_k2[l],
                       peer_u[l].astype(BF16), peer_v[l].astype(BF16), g2,
                       final_g if l == depth - 1 else None)
    return x
```

```python
import functools

import jax
import jax.numpy as jnp
import numpy as np
from jax import lax
from jax.experimental import pallas as pl
from jax.experimental.pallas import tpu as pltpu

F32 = jnp.float32
BF16 = jnp.bfloat16
HIGHEST = lax.Precision.HIGHEST

HEAD_DIM = 64
CONV_WIDTH = 31
CONV_EPS = 1e-5
RWKV_GN_EPS = 64e-5
RMS_EPS = 1e-6
LORA_W = 64
LORA_A = 64
LORA_G = 128
NSA_KV_HEADS = 2
NSA_GROUP = 4
CMP_BLOCK = 32
CMP_STRIDE = 16
CMP_HIDDEN = 128
SEL_BLOCK = 64
N_SEL = 16
WINDOW = 512
FORCE_SCORE = 1e4
NEG_INF = -1e30
LOG2E = 1.4426950408889634
PEER_HEADS = 8
PEER_KEYS = 128
PEER_TOPK = 16

LANES = 128
VMEM_LIMIT = 56 * 1024 * 1024


def _dot_hi(a, b):
    return jnp.dot(a, b, precision=HIGHEST, preferred_element_type=F32)


def _dot_bf(a, b):
    return jnp.dot(a.astype(BF16), b.astype(BF16), preferred_element_type=F32)


def _cparams(*sem):
    return pltpu.CompilerParams(dimension_semantics=sem, vmem_limit_bytes=VMEM_LIMIT)


def _group_avg_matrix(n, group):
    i = np.arange(n)
    return jnp.asarray((i[:, None] // group == i[None, :] // group).astype(np.float32) / group)


def _ada_kernel(c_ref, w_ref, b_ref, o_ref):
    c = c_ref[...]
    cs = c * jax.nn.sigmoid(c)
    o_ref[0] = _dot_hi(cs, w_ref[0]) + b_ref[0]


def ada_mod(c, ada_w, ada_b):
    L, D, N = ada_w.shape
    B = c.shape[0]
    bp = 8
    cp = jnp.zeros((bp, D), F32).at[:B].set(c)
    tn = N // 4
    out = pl.pallas_call(
        _ada_kernel,
        grid=(L, N // tn),
        in_specs=[pl.BlockSpec((bp, D), lambda l, j: (0, 0)),
                  pl.BlockSpec((1, D, tn), lambda l, j: (l, 0, j)),
                  pl.BlockSpec((1, 1, tn), lambda l, j: (l, 0, j))],
        out_specs=pl.BlockSpec((1, bp, tn), lambda l, j: (l, 0, j)),
        out_shape=jax.ShapeDtypeStruct((L, bp, N), F32),
        compiler_params=_cparams("parallel", "parallel"),
        name="ada_mod",
    )(cp, ada_w, ada_b.reshape(L, 1, N))
    return out[:, :B]


def _modulated_norm(x, g, sc, sh):
    ms = jnp.mean(x * x, axis=-1, keepdims=True)
    return x * lax.rsqrt(ms + RMS_EPS) * g * (1.0 + sc) + sh


def _proj_in_kernel(x_ref, sh_ref, sc_ref, g_ref, *refs):
    n = len(refs) // 2
    h = _modulated_norm(x_ref[0], g_ref[...], sc_ref[0], sh_ref[0]).astype(BF16)
    for w_ref, o_ref in zip(refs[:n], refs[n:]):
        o_ref[0] = jnp.dot(h, w_ref[...], preferred_element_type=F32)


def proj_in(x, sh, sc, g, weights, tm=512):
    B, T, D = x.shape
    tm = min(tm, T)
    vec = pl.BlockSpec((1, 1, D), lambda b, i: (b, 0, 0))
    in_specs = [pl.BlockSpec((1, tm, D), lambda b, i: (b, i, 0)), vec, vec,
                pl.BlockSpec((1, D), lambda b, i: (0, 0))]
    in_specs += [pl.BlockSpec(w.shape, lambda b, i: (0, 0)) for w in weights]
    out_specs = [pl.BlockSpec((1, tm, w.shape[1]), lambda b, i: (b, i, 0)) for w in weights]
    out_shape = [jax.ShapeDtypeStruct((B, T, w.shape[1]), F32) for w in weights]
    return pl.pallas_call(
        _proj_in_kernel,
        grid=(B, T // tm),
        in_specs=in_specs, out_specs=out_specs, out_shape=out_shape,
        compiler_params=_cparams("parallel", "parallel"),
        name="proj_in",
    )(x, sh.reshape(B, 1, D), sc.reshape(B, 1, D), g.reshape(1, D), *weights)


CONV_HALO = 32


def _conv_kernel(p_ref, w_ref, b_ref, gg_ref, gb_ref, m_ref, o_ref, ext_ref):
    i = pl.program_id(1)
    tt, dc = o_ref.shape[1], o_ref.shape[2]

    @pl.when(i == 0)
    def _():
        ext_ref[0:CONV_HALO, :] = jnp.zeros((CONV_HALO, dc), F32)

    @pl.when(i > 0)
    def _():
        ext_ref[0:CONV_HALO, :] = ext_ref[tt:tt + CONV_HALO, :]

    p = p_ref[0]
    ext_ref[CONV_HALO:CONV_HALO + tt, :] = p[:, :dc] * jax.nn.sigmoid(p[:, dc:])
    off = CONV_HALO - (CONV_WIDTH - 1)
    acc = jnp.zeros((tt, dc), F32) + b_ref[...]
    for j in range(CONV_WIDTH):
        acc = acc + ext_ref[off + j:off + j + tt, :] * w_ref[j:j + 1, :]
    m = m_ref[...]
    mu = _dot_hi(acc, m)
    d = acc - mu
    var = _dot_hi(d * d, m)
    y = d * lax.rsqrt(var + CONV_EPS) * gg_ref[...] + gb_ref[...]
    o_ref[0] = y * jax.nn.sigmoid(y)


def conv_mixer(p, dw_w, dw_b, gn_g, gn_b, tt=512):
    B, T, two_dc = p.shape
    dc = two_dc // 2
    tt = min(tt, T)
    wpad = jnp.zeros((32, dc), F32).at[:CONV_WIDTH].set(dw_w.reshape(CONV_WIDTH, dc))
    row = pl.BlockSpec((1, dc), lambda b, i: (0, 0))
    return pl.pallas_call(
        _conv_kernel,
        grid=(B, T // tt),
        in_specs=[pl.BlockSpec((1, tt, two_dc), lambda b, i: (b, i, 0)),
                  pl.BlockSpec((32, dc), lambda b, i: (0, 0)), row, row, row,
                  pl.BlockSpec((dc, dc), lambda b, i: (0, 0))],
        out_specs=pl.BlockSpec((1, tt, dc), lambda b, i: (b, i, 0)),
        out_shape=jax.ShapeDtypeStruct((B, T, dc), F32),
        scratch_shapes=[pltpu.VMEM((tt + CONV_HALO, dc), F32)],
        compiler_params=_cparams("parallel", "arbitrary"),
        name="conv_mixer",
    )(p, wpad, dw_b.reshape(1, dc), gn_g.reshape(1, dc), gn_b.reshape(1, dc),
      _group_avg_matrix(dc, HEAD_DIM))


def _group_sum_matrix(n, group):
    i = np.arange(n)
    return jnp.asarray((i[:, None] // group == i[None, :] // group).astype(np.float32))


def _rwkv_pre_kernel(p_ref, halo_ref, mu_ref, w0_ref, wup_ref, a0_ref, aup_ref, gup_ref, kk_ref, ka_ref,
                     rk_ref, ones_ref, w_o, kk_o, b_o, k_o, v_o, r_o, bonus_o, g_o, ext_ref):
    i = pl.program_id(1)
    tt = p_ref.shape[1]
    dr = w_o.shape[2]
    p = p_ref[0]
    first = (i > 0).astype(F32)
    ext_ref[0:8, :] = halo_ref[0] * first
    ext_ref[8:8 + tt, :] = p
    prev = ext_ref[7:7 + tt, :]
    xs = p + (prev - p) * mu_ref[...]
    r = xs[:, 0:dr]
    k = xs[:, dr:2 * dr]
    v = xs[:, 2 * dr:3 * dr]
    o = 3 * dr
    xw = xs[:, o:o + LORA_W]
    xa = xs[:, o + LORA_W:o + LORA_W + LORA_A]
    xg = xs[:, o + LORA_W + LORA_A:o + LORA_W + LORA_A + LORA_G]
    z = w0_ref[...] + _dot_hi(jnp.tanh(xw), wup_ref[...])
    w_log = -jax.nn.softplus(-z) - 0.5
    decay = jnp.exp(-jnp.exp(w_log))
    a = jax.nn.sigmoid(a0_ref[...] + _dot_hi(xa, aup_ref[...]))
    g = _dot_hi(jax.nn.sigmoid(xg), gup_ref[...])
    ones = ones_ref[...]
    kk = k * kk_ref[...]
    nrm = jnp.sqrt(_dot_hi(kk * kk, ones))
    kk = kk / jnp.maximum(nrm, 1e-12)
    k_eff = k * (1.0 + (a - 1.0) * ka_ref[...])
    bonus = _dot_hi(r * k_eff * rk_ref[...], ones) * v
    w_o[0] = decay
    kk_o[0] = kk
    b_o[0] = kk * a
    k_o[0] = k_eff
    v_o[0] = v
    r_o[0] = r
    bonus_o[0] = bonus
    g_o[0] = g


def rwkv_pre(p, mu, w0, w_up, a0, a_up, g_up, k_k, k_a, r_k, tt=512):
    B, T, n_in = p.shape
    dr = w0.shape[0]
    tt = min(tt, T)
    nb8 = tt // 8
    full = lambda a: pl.BlockSpec(a.shape, lambda b, i: (0,) * a.ndim)
    args = [mu.reshape(1, n_in), w0.reshape(1, dr), w_up, a0.reshape(1, dr), a_up, g_up,
            k_k.reshape(1, dr), k_a.reshape(1, dr), r_k.reshape(1, dr), _group_sum_matrix(dr, HEAD_DIM)]
    out = jax.ShapeDtypeStruct((B, T, dr), F32)
    ospec = pl.BlockSpec((1, tt, dr), lambda b, i: (b, i, 0))
    return pl.pallas_call(
        _rwkv_pre_kernel,
        grid=(B, T // tt),
        in_specs=[pl.BlockSpec((1, tt, n_in), lambda b, i: (b, i, 0)),
                  pl.BlockSpec((1, 8, n_in), lambda b, i: (b, jnp.maximum(i * nb8 - 1, 0), 0))]
                 + [full(a) for a in args],
        out_specs=[ospec] * 8, out_shape=[out] * 8,
        scratch_shapes=[pltpu.VMEM((tt + 8, n_in), F32)],
        compiler_params=_cparams("parallel", "parallel"),
        name="rwkv_pre",
    )(p, p, *args)


SCAN_SUB = 64
SCAN_CHUNK = 128


def _scan_select_matrices():
    e = np.zeros((SCAN_SUB // 2, 4 * SCAN_SUB, 2 * LANES), np.float32)
    for t in range(SCAN_SUB):
        c = (t % 2) * LANES
        for part in range(2):
            e[t // 2, 2 * part * SCAN_SUB + t, c:c + HEAD_DIM] = 1.0
            e[t // 2, (2 * part + 1) * SCAN_SUB + t, c + HEAD_DIM:c + LANES] = 1.0
    return jnp.asarray(e, BF16)


def _rwkv_scan_kernel(w_ref, kk_ref, b_ref, k_ref, r_ref, v_ref, bonus_ref, g_ref, lng_ref, lnb_ref,
                      e_ref, avg_ref, o_ref, st_ref, lhs_ref, y_ref):
    nb, tc, dr = v_ref.shape
    n_half = dr // LANES
    n_pair = nb * n_half
    n_sub = tc // SCAN_SUB
    quantities = (w_ref, kk_ref, b_ref, k_ref, r_ref)

    @pl.when(pl.program_id(0) == 0)
    def _():
        st_ref[...] = jnp.zeros(st_ref.shape, F32)

    for qi, q_ref in enumerate(quantities):
        for p in range(n_pair):
            b, hb = divmod(p, n_half)
            xt = q_ref[b, :, hb * LANES:(hb + 1) * LANES].T
            for s in range(n_sub):
                cat = jnp.concatenate([xt[0:HEAD_DIM, s * SCAN_SUB:(s + 1) * SCAN_SUB],
                                       xt[HEAD_DIM:, s * SCAN_SUB:(s + 1) * SCAN_SUB]], axis=1)
                hi = cat.astype(BF16)
                lo = (cat - hi.astype(F32)).astype(BF16)
                row = (qi * n_pair + p) * HEAD_DIM
                lhs_ref[s, row:row + HEAD_DIM, :] = jnp.concatenate([hi, lo], axis=1)

    sub_iota = lax.broadcasted_iota(jnp.int32, (8, LANES), 0)
    for s in range(n_sub):
        def steps8(t8, carry, s=s):
            base = pl.multiple_of(s * SCAN_SUB + t8 * 8, 8)
            y8 = [jnp.zeros((8, LANES), F32) for _ in range(n_pair)]
            for j in range(8):
                if j % 2 == 0:
                    z2 = jnp.dot(lhs_ref[s], e_ref[t8 * 4 + j // 2], preferred_element_type=F32)
                z = z2[:, (j % 2) * LANES:(j % 2 + 1) * LANES]
                for p in range(n_pair):
                    b, hb = divmod(p, n_half)
                    col = lambda qi: z[(qi * n_pair + p) * HEAD_DIM:(qi * n_pair + p + 1) * HEAD_DIM, :]
                    st = st_ref[p]
                    sa = jnp.sum(st * col(1), axis=0, keepdims=True)
                    v_row = v_ref[b, pl.ds(base, 8), hb * LANES:(hb + 1) * LANES][j:j + 1, :]
                    st = st * col(0) - col(2) * sa + col(3) * v_row
                    y = jnp.sum(st * col(4), axis=0, keepdims=True)
                    y8[p] = jnp.where(sub_iota == j, y, y8[p])
                    st_ref[p] = st
            for p in range(n_pair):
                b, hb = divmod(p, n_half)
                y_ref[b, pl.ds(base, 8), hb * LANES:(hb + 1) * LANES] = y8[p]
            return carry
        lax.fori_loop(0, SCAN_SUB // 8, steps8, 0)

    avg = avg_ref[...]
    for b in range(nb):
        y = y_ref[b]
        mu = _dot_hi(y, avg)
        d = y - mu
        var = _dot_hi(d * d, avg)
        yn = d * lax.rsqrt(var + RWKV_GN_EPS) * lng_ref[...] + lnb_ref[...]
        o_ref[b] = (yn + bonus_ref[b]) * g_ref[b]


def rwkv_scan(w, kk, bq, k, r, v, bonus, g, ln_g, ln_b):
    B, T, dr = v.shape
    tc = min(SCAN_CHUNK, T)
    n_pair = B * dr // LANES
    blk = pl.BlockSpec((B, tc, dr), lambda i: (0, i, 0))
    row = pl.BlockSpec((1, dr), lambda i: (0, 0))
    e = _scan_select_matrices()
    return pl.pallas_call(
        _rwkv_scan_kernel,
        grid=(T // tc,),
        in_specs=[blk] * 8 + [row, row, pl.BlockSpec(e.shape, lambda i: (0, 0, 0)),
                              pl.BlockSpec((dr, dr), lambda i: (0, 0))],
        out_specs=blk,
        out_shape=jax.ShapeDtypeStruct((B, T, dr), F32),
        scratch_shapes=[pltpu.VMEM((n_pair, HEAD_DIM, LANES), F32),
                        pltpu.VMEM((tc // SCAN_SUB, 5 * n_pair * HEAD_DIM, 4 * SCAN_SUB), BF16),
                        pltpu.VMEM((B, tc, dr), F32)],
        compiler_params=_cparams("arbitrary"),
        name="rwkv_scan",
    )(w, kk, bq, k, r, v, bonus, g, ln_g.reshape(1, dr), ln_b.reshape(1, dr), e, _group_avg_matrix(dr, HEAD_DIM))


def rwkv_mixer(p, mu, w0, w_up, a0, a_up, g_up, k_k, k_a, r_k, ln_g, ln_b):
    w, kk, bq, k, v, r, bonus, g = rwkv_pre(p, mu, w0, w_up, a0, a_up, g_up, k_k, k_a, r_k.reshape(-1))
    return rwkv_scan(w, kk, bq, k, r, v, bonus, g, ln_g, ln_b)


def _dot_nt(a, b, **kw):
    return lax.dot_general(a, b, (((1,), (1,)), ((), ())), preferred_element_type=F32, **kw)


def _compress_kernel(kz_ref, vz_ref, kpos_ref, kw1_ref, kw2_ref, vpos_ref, vw1_ref, vw2_ref,
                     kc_ref, vc_ref, shift_ref):
    n = kz_ref.shape[2]
    half = kz_ref.shape[3]

    def one(z_ref, pos_ref, w1_ref, w2_ref, o_ref):
        z = z_ref[0, 0]
        top = _dot_bf(z + pos_ref[0:1, :], w1_ref[0:half, :])
        bot = _dot_bf(z + pos_ref[1:2, :], w1_ref[half:2 * half, :])
        shift_ref[0:n, :] = bot
        shift_ref[n:n + 8, :] = jnp.zeros((8, bot.shape[1]), F32)
        pre = top + shift_ref[1:n + 1, :]
        hid = pre * jax.nn.sigmoid(pre)
        o_ref[0, 0] = _dot_bf(hid, w2_ref[...])

    one(kz_ref, kpos_ref, kw1_ref, kw2_ref, kc_ref)
    one(vz_ref, vpos_ref, vw1_ref, vw2_ref, vc_ref)


def nsa_compress(k_cmp, v_cmp, ck_pos, ck_w1, ck_w2, cv_pos, cv_w1, cv_w2):
    B, G, T, dk = k_cmp.shape
    n = T // CMP_STRIDE
    half = CMP_STRIDE * dk
    zspec = pl.BlockSpec((1, 1, n, half), lambda b, g: (b, g, 0, 0))
    full = lambda a: pl.BlockSpec(a.shape, lambda b, g: (0,) * a.ndim)
    ospec = pl.BlockSpec((1, 1, n, dk), lambda b, g: (b, g, 0, 0))
    args = [ck_pos.reshape(2, half), ck_w1.astype(BF16), ck_w2.astype(BF16),
            cv_pos.reshape(2, half), cv_w1.astype(BF16), cv_w2.astype(BF16)]
    return pl.pallas_call(
        _compress_kernel,
        grid=(B, G),
        in_specs=[zspec, zspec] + [full(a) for a in args],
        out_specs=[ospec, ospec],
        out_shape=[jax.ShapeDtypeStruct((B, G, n, dk), F32)] * 2,
        scratch_shapes=[pltpu.VMEM((n + 8, CMP_HIDDEN), F32)],
        compiler_params=_cparams("parallel", "parallel"),
        name="nsa_compress",
    )(k_cmp.reshape(B, G, n, half), v_cmp.reshape(B, G, n, half), *args)


NSA_TQ = 128
NSA_TK = 1024


def _split_bf16(a):
    hi = a.astype(BF16)
    return hi, (a - hi.astype(F32)).astype(BF16)


def _dot_3pass(a, b):
    ah, al = _split_bf16(a)
    bh, bl = _split_bf16(b)
    d = lambda x, y: jnp.dot(x, y, preferred_element_type=F32)
    return d(ah, bh) + d(ah, bl) + d(al, bh)


def _nsa_kernel(q_ref, gate_ref, kc_ref, vct_ref, ks_ref, vst_ref, kw_ref, vwt_ref, ovt_ref,
                o_ref, m_ref, acc_ref, ow_ref, sel_ref, s0_ref, s1_ref, p_ref):
    qi = pl.program_id(2)
    dk = q_ref.shape[3]
    tq = gate_ref.shape[3]
    R = q_ref.shape[4] // tq
    n_c = kc_ref.shape[2]
    n_blk = ovt_ref.shape[0]
    T = ks_ref.shape[2]
    tk = s0_ref.shape[0]
    t0 = qi * tq
    qt = q_ref[0, 0, 0]
    qt_bf = qt.astype(BF16)
    t_row = t0 + lax.broadcasted_iota(jnp.int32, (1, tq), 1)
    lanes = lambda r: slice(r * tq, (r + 1) * tq)

    s = _dot_3pass(kc_ref[0, 0], qt)
    c_end = lax.broadcasted_iota(jnp.int32, (n_c, tq), 0) * CMP_STRIDE + (CMP_BLOCK - 1)
    valid_c = c_end <= t_row
    any_c = (t_row >= CMP_BLOCK - 1).astype(F32)
    p_sum = jnp.zeros((n_c, tq), F32)
    ps = []
    for r in range(R):
        s_r = jnp.where(valid_c, s[:, lanes(r)], NEG_INF)
        e = jnp.exp2(s_r - jnp.max(s_r, axis=0, keepdims=True))
        p_r = e * (any_c / jnp.sum(e, axis=0, keepdims=True))
        p_sum = p_sum + p_r
        ps.append(p_r.astype(BF16))
    o_c = jnp.dot(vct_ref[0, 0].astype(BF16), jnp.concatenate(ps, axis=1), preferred_element_type=F32)

    imp = _dot_hi(ovt_ref[...], p_sum)
    blk = lax.broadcasted_iota(jnp.int32, (n_blk, tq), 0).astype(F32)
    cur = (t_row // SEL_BLOCK).astype(F32)
    forced = (blk == 0.0) | (blk == cur) | (blk == cur - 1.0)
    x = jnp.where(forced, FORCE_SCORE, jnp.where(blk <= cur, imp, -1.0))
    x = jnp.where(blk < float(T // SEL_BLOCK), x, -3e38)
    sel = jnp.zeros((n_blk, tq), F32)
    for _ in range(N_SEL):
        mx = jnp.max(x, axis=0, keepdims=True)
        idx = jnp.min(jnp.where(x == mx, blk, float(n_blk)), axis=0, keepdims=True)
        hit = blk == idx
        sel = jnp.where(hit, 1.0, sel)
        x = jnp.where(hit, -jnp.inf, x)

    span = WINDOW + tq
    w0 = pl.multiple_of(jnp.maximum(t0 - WINDOW, 0), tq)
    s = jnp.dot(kw_ref[0, 0, pl.ds(w0, span), :], qt_bf, preferred_element_type=F32)
    dist = t_row - (w0 + lax.broadcasted_iota(jnp.int32, (span, tq), 0))
    bias_w = jnp.where((dist >= 0) & (dist < WINDOW), 0.0, NEG_INF)
    ps = []
    for r in range(R):
        s_r = s[:, lanes(r)] + bias_w
        ps.append(jnp.exp2(s_r - jnp.max(s_r, axis=0, keepdims=True)).astype(BF16))
    ow_ref[...] = jnp.dot(vwt_ref[0, 0, :, pl.ds(w0, span)], jnp.concatenate(ps, axis=1),
                          preferred_element_type=F32)

    bpt = tk // SEL_BLOCK
    n_kt = T // tk
    n_full = (t0 + tq - 1) // tk
    sel_ref[...] = jnp.where(sel > 0.5, 0.0, NEG_INF)
    m_ref[...] = jnp.full(m_ref.shape, NEG_INF, F32)
    acc_ref[...] = jnp.zeros(acc_ref.shape, F32)
    q_pad = jnp.zeros((ks_ref.shape[3] - dk - bpt, R * tq), BF16)

    def scores(kt):
        kt = jnp.minimum(kt, n_kt - 1)
        rows = sel_ref[pl.ds(pl.multiple_of(kt * bpt, bpt), bpt), :].astype(BF16)
        q_aug = jnp.concatenate([qt_bf, jnp.concatenate([rows] * R, axis=1), q_pad], axis=0)
        off = pl.multiple_of(kt * tk, tk)
        return jnp.dot(ks_ref[0, 0, pl.ds(off, tk), :], q_aug, preferred_element_type=F32)

    def softmax_tile(s_ref, kt, causal):
        if causal:
            key_pos = kt * tk + lax.broadcasted_iota(jnp.int32, (tk, tq), 0)
            bias = jnp.where(key_pos <= t_row, 0.0, NEG_INF)
        for r in range(R):
            m_old = m_ref[:, lanes(r)]
            if causal:
                m_new = jnp.maximum(m_old, jnp.max(s_ref[:, lanes(r)] + bias, axis=0, keepdims=True))
                p_r = jnp.exp2((s_ref[:, lanes(r)] - m_new) + bias)
            else:
                m_new = jnp.maximum(m_old, jnp.max(s_ref[:, lanes(r)], axis=0, keepdims=True))
                p_r = jnp.exp2(s_ref[:, lanes(r)] - m_new)
            m_ref[:, lanes(r)] = m_new
            p_ref[:, lanes(r)] = p_r.astype(BF16)
            acc_ref[:, lanes(r)] = acc_ref[:, lanes(r)] * jnp.exp2(m_old - m_new)
        off = pl.multiple_of(kt * tk, tk)
        acc_ref[...] += jnp.dot(vst_ref[0, 0, :, pl.ds(off, tk)], p_ref[...], preferred_element_type=F32)

    s0_ref[...] = scores(0)

    def tile_pair(i, carry):
        s1_ref[...] = scores(2 * i + 1)
        softmax_tile(s0_ref, 2 * i, False)
        s0_ref[...] = scores(2 * i + 2)
        softmax_tile(s1_ref, 2 * i + 1, False)
        return carry

    lax.fori_loop(0, n_full // 2, tile_pair, 0)
    odd = n_full % 2 == 1

    @pl.when(odd)
    def _():
        s1_ref[...] = scores(n_full)
        softmax_tile(s0_ref, n_full - 1, False)
        softmax_tile(s1_ref, n_full, True)

    @pl.when(jnp.logical_not(odd))
    def _():
        softmax_tile(s0_ref, n_full, True)

    acc = acc_ref[...]
    o_s = acc[0:dk] * (1.0 / acc[dk:dk + 1])
    o_w = ow_ref[...]
    o_w = o_w[0:dk] * (1.0 / o_w[dk:dk + 1])

    gate = jax.nn.sigmoid(gate_ref[0, 0])
    outs = []
    for r in range(R):
        outs.append(gate[r:r + 1] * o_c[:, lanes(r)] + gate[R + r:R + r + 1] * o_s[:, lanes(r)]
                    + gate[2 * R + r:2 * R + r + 1] * o_w[:, lanes(r)])
    o_ref[0] = jnp.concatenate(outs, axis=0).T


def _overlap_matrix_t(n_c, n_blk):
    c0 = np.arange(n_c)[None, :] * CMP_STRIDE
    s0 = np.arange(n_blk)[:, None] * SEL_BLOCK
    return jnp.asarray(((c0 < s0 + SEL_BLOCK) & (c0 + CMP_BLOCK > s0)).astype(np.float32))


def nsa_attention(qt, gate_t, kc, vct, k_sel, vt_sel, k_win, vt_win):
    B, G, nq, dk, L = qt.shape
    T = k_win.shape[2]
    tq = T // nq
    R = L // tq
    n_c = kc.shape[2]
    tk = min(NSA_TK, T)
    n_blk = max(T // SEL_BLOCK, LANES)
    ovt = _overlap_matrix_t(n_c, n_blk)
    res = lambda a: pl.BlockSpec((1, 1) + a.shape[2:], lambda b, g, i: (b, g, 0, 0))
    return pl.pallas_call(
        _nsa_kernel,
        grid=(B, G, nq),
        in_specs=[pl.BlockSpec((1, 1, 1, dk, L), lambda b, g, i: (b, g, i, 0, 0)),
                  pl.BlockSpec((1, 1, gate_t.shape[2], tq), lambda b, g, i: (b, g, 0, i)),
                  res(kc), res(vct), res(k_sel), res(vt_sel), res(k_win), res(vt_win),
                  pl.BlockSpec(ovt.shape, lambda b, g, i: (0, 0))],
        out_specs=pl.BlockSpec((1, tq, R * dk), lambda b, g, i: (b, i, g)),
        out_shape=jax.ShapeDtypeStruct((B, T, G * R * dk), F32),
        scratch_shapes=[pltpu.VMEM((1, L), F32), pltpu.VMEM((dk + 16, L), F32),
                        pltpu.VMEM((dk + 16, L), F32), pltpu.VMEM((n_blk, tq), F32),
                        pltpu.VMEM((tk, L), F32), pltpu.VMEM((tk, L), F32), pltpu.VMEM((tk, L), BF16)],
        compiler_params=_cparams("parallel", "parallel", "arbitrary"),
        name="nsa_attention",
    )(qt, gate_t, kc, vct, k_sel, vt_sel, k_win, vt_win, ovt)


def nsa_mixer(p_q, p_kv, p_gate, ck_pos, ck_w1, ck_w2, cv_pos, cv_w1, cv_w2):
    B, T, _ = p_q.shape
    G, R, dk = NSA_KV_HEADS, NSA_GROUP, HEAD_DIM
    tq = min(NSA_TQ, T)
    kv6 = p_kv.reshape(B, T, 6, G, dk)
    rows = lambda i: kv6[:, :, i].transpose(0, 2, 1, 3)
    cols = lambda i: kv6[:, :, i].transpose(0, 2, 3, 1)
    kc, vc = nsa_compress(rows(0), rows(1), ck_pos, ck_w1, ck_w2, cv_pos, cv_w1, cv_w2)
    qt = (p_q * (dk ** -0.5 * LOG2E)).reshape(B, T // tq, tq, G, R, dk).transpose(0, 3, 1, 5, 4, 2)
    qt = qt.reshape(B, G, T // tq, dk, R * tq)
    gate_t = p_gate[..., :3 * G * R].reshape(B, T, 3, G, R).transpose(0, 3, 2, 4, 1).reshape(B, G, 3 * R, T)
    gate_t = jnp.pad(gate_t, ((0, 0), (0, 0), (0, 16 - 3 * R), (0, 0)))
    tk = min(NSA_TK, T)
    blk_onehot = (jnp.arange(T)[:, None] // SEL_BLOCK % (tk // SEL_BLOCK) == jnp.arange(dk)[None, :]).astype(BF16)
    k_sel = jnp.concatenate([rows(2).astype(BF16), jnp.broadcast_to(blk_onehot, (B, G, T, dk))], axis=-1)
    ones_rows = jnp.zeros((B, G, 16, T), BF16).at[:, :, 0].set(1.0)
    with_ones = lambda vt: jnp.concatenate([vt.astype(BF16), ones_rows], axis=2)
    return nsa_attention(qt, gate_t, kc, vc.transpose(0, 1, 3, 2), k_sel, with_ones(cols(3)),
                         rows(4).astype(BF16), with_ones(cols(5)))


def _proj_out_kernel(x_ref, g1_ref, sh_ref, sc_ref, ng_ref, *refs):
    n = (len(refs) - 2) // 2
    x1_ref, h2_ref = refs[2 * n:]
    acc = None
    for m_ref, w_ref in zip(refs[:n], refs[n:2 * n]):
        d = jnp.dot(m_ref[0].astype(BF16), w_ref[...], preferred_element_type=F32)
        acc = d if acc is None else acc + d
    x1 = x_ref[0] + g1_ref[0] * acc
    x1_ref[0] = x1
    h2_ref[0] = _modulated_norm(x1, ng_ref[...], sc_ref[0], sh_ref[0]).astype(BF16)


def proj_out(x, g1, sh2, sc2, norm_g, mixes, weights, tm=512):
    B, T, D = x.shape
    tm = min(tm, T)
    vec = pl.BlockSpec((1, 1, D), lambda b, i: (b, 0, 0))
    tile = lambda n: pl.BlockSpec((1, tm, n), lambda b, i: (b, i, 0))
    return pl.pallas_call(
        _proj_out_kernel,
        grid=(B, T // tm),
        in_specs=[tile(D), vec, vec, vec, pl.BlockSpec((1, D), lambda b, i: (0, 0))]
                 + [tile(m.shape[2]) for m in mixes]
                 + [pl.BlockSpec(w.shape, lambda b, i: (0, 0)) for w in weights],
        out_specs=[tile(D), tile(D)],
        out_shape=[jax.ShapeDtypeStruct((B, T, D), F32), jax.ShapeDtypeStruct((B, T, D), BF16)],
        compiler_params=_cparams("parallel", "parallel"),
        name="proj_out",
    )(x, g1.reshape(B, 1, D), sh2.reshape(B, 1, D), sc2.reshape(B, 1, D), norm_g.reshape(1, D), *mixes, *weights)


_CAND_BLOCKS = [(0, 16), (1, 8), (None, 8), (2, 5), (3, 4), (4, 3), (5, 2), (6, 2), (7, 2)]
_CAND_PAD = 6


def _cand_constants(tm):
    flat, neg = [], []
    for i, rows in _CAND_BLOCKS:
        for r in range(rows):
            flat.append((8 + r) * PEER_TOPK if i is None else i * PEER_TOPK + r)
            neg.append(0.0)
    flat += [1e9] * _CAND_PAD
    neg += [-np.inf] * _CAND_PAD
    flat = np.tile(np.asarray(flat, np.float32)[:, None], (1, tm))
    neg = np.tile(np.asarray(neg, np.float32)[:, None], (1, tm))
    return jnp.asarray(flat), jnp.asarray(neg)


def _topk_rows(x, row_id, k, n_rows):
    vals, idxs = [], []
    for _ in range(k):
        mx = jnp.max(x, axis=0, keepdims=True)
        idx = jnp.min(jnp.where(x == mx, row_id, float(n_rows)), axis=0, keepdims=True)
        x = jnp.where(row_id == idx, -jnp.inf, x)
        vals.append(mx)
        idxs.append(idx)
    return jnp.concatenate(vals, axis=0), jnp.concatenate(idxs, axis=0)


def _peer_select_kernel(h_ref, wq_ref, k1_ref, k2_ref, flat_ref, neg_ref, a_ref, b_ref, g_ref):
    tm = h_ref.shape[0]
    nk = k1_ref.shape[0]
    half = k1_ref.shape[1]
    q = jnp.dot(h_ref[...], wq_ref[...], preferred_element_type=F32)
    key_id = lax.broadcasted_iota(jnp.int32, (nk, tm), 0).astype(F32)
    flat = flat_ref[...]
    neg = neg_ref[...]
    k1 = k1_ref[...]
    k2 = k2_ref[...]
    a_all, b_all, g_all = [], [], []
    for h in range(PEER_HEADS):
        q1 = q[:, (2 * h) * half:(2 * h + 1) * half]
        q2 = q[:, (2 * h + 1) * half:(2 * h + 2) * half]
        v1, i1 = _topk_rows(_dot_nt(k1, q1, precision=HIGHEST), key_id, PEER_TOPK, nk)
        v2, i2 = _topk_rows(_dot_nt(k2, q2, precision=HIGHEST), key_id, PEER_TOPK, nk)
        vals, ai, bj = [], [], []
        for i, rows in _CAND_BLOCKS:
            if i is None:
                vals.append(v1[8:16] + v2[0:1])
                ai.append(i1[8:16])
                bj.append(jnp.broadcast_to(i2[0:1], (8, tm)))
            else:
                vals.append(v1[i:i + 1] + v2[0:rows])
                ai.append(jnp.broadcast_to(i1[i:i + 1], (rows, tm)))
                bj.append(i2[0:rows])
        pad = [jnp.zeros((_CAND_PAD, tm), F32)]
        cand = jnp.concatenate(vals + pad, axis=0) + neg
        ai = jnp.concatenate(ai + pad, axis=0)
        bj = jnp.concatenate(bj + pad, axis=0)
        sc, sa, sb = [], [], []
        for _ in range(PEER_TOPK):
            mx = jnp.max(cand, axis=0, keepdims=True)
            fid = jnp.min(jnp.where(cand == mx, flat, 1e9), axis=0, keepdims=True)
            hit = flat == fid
            sc.append(mx)
            sa.append(jnp.sum(jnp.where(hit, ai, 0.0), axis=0, keepdims=True))
            sb.append(jnp.sum(jnp.where(hit, bj, 0.0), axis=0, keepdims=True))
            cand = jnp.where(hit, -jnp.inf, cand)
        sc = jnp.concatenate(sc, axis=0)
        e = jnp.exp(sc - sc[0:1])
        g_all.append(e / jnp.sum(e, axis=0, keepdims=True))
        a_all.append(jnp.concatenate(sa, axis=0))
        b_all.append(jnp.concatenate(sb, axis=0))
    a_ref[...] = jnp.concatenate(a_all, axis=0).T
    b_ref[...] = jnp.concatenate(b_all, axis=0).T
    g_ref[...] = jnp.concatenate(g_all, axis=0).T


def peer_select(h2, wq, k1, k2, tm=256):
    N, D = h2.shape
    tm = min(tm, N)
    flat, neg = _cand_constants(tm)
    full = lambda a: pl.BlockSpec(a.shape, lambda i: (0,) * a.ndim)
    out = jax.ShapeDtypeStruct((N, PEER_HEADS * PEER_TOPK), F32)
    ospec = pl.BlockSpec((tm, PEER_HEADS * PEER_TOPK), lambda i: (i, 0))
    return pl.pallas_call(
        _peer_select_kernel,
        grid=(N // tm,),
        in_specs=[pl.BlockSpec((tm, D), lambda i: (i, 0)), full(wq), full(k1), full(k2), full(flat), full(neg)],
        out_specs=[ospec] * 3, out_shape=[out] * 3,
        compiler_params=_cparams("parallel"),
        name="peer_select",
    )(h2, wq, k1, k2, flat, neg)


PEER_BUILD_GROUP = 16
PEER_BUILD_UNROLL = 2


def _gelu(x):
    return 0.5 * x * (1.0 + lax.erf(x * 0.7071067811865476))


def _peer_expert_kernel(*refs, final, n_e):
    if final:
        h_ref, a_ref, b_ref, g_ref, u_ref, v_ref, x_ref, g2_ref, fg_ref, o_ref, w3_ref, acc_ref, ha_ref, hb_ref = refs
    else:
        h_ref, a_ref, b_ref, g_ref, u_ref, v_ref, x_ref, g2_ref, o_ref, w3_ref, acc_ref, ha_ref, hb_ref = refs
    j = pl.program_id(1)
    tm = h_ref.shape[0]
    te = u_ref.shape[0]
    nk = PEER_KEYS
    n_a = te // nk

    def hidden():
        return _dot_nt(h_ref[...], u_ref[...])

    def finish(hid_ref, tile):
        acts = []
        for al in range(n_a):
            w_a = w3_ref[tile * n_a + al].astype(F32)
            acts.append((_gelu(hid_ref[:, al * nk:(al + 1) * nk]) * w_a).astype(BF16))
        acc_ref[...] += jnp.dot(jnp.concatenate(acts, axis=1), v_ref[...], preferred_element_type=F32)

    @pl.when(j == 0)
    def _():
        acc_ref[...] = jnp.zeros(acc_ref.shape, F32)
        ha_ref[...] = hidden()
        key_id = lax.broadcasted_iota(jnp.int32, (nk, a_ref.shape[1]), 0).astype(F32)

        def tokens(tb, carry):
            for grp in range(PEER_BUILD_UNROLL):
                base = pl.multiple_of((tb * PEER_BUILD_UNROLL + grp) * PEER_BUILD_GROUP, PEER_BUILD_GROUP)
                a8 = a_ref[pl.ds(base, PEER_BUILD_GROUP), :]
                b8 = b_ref[pl.ds(base, PEER_BUILD_GROUP), :]
                g8 = g_ref[pl.ds(base, PEER_BUILD_GROUP), :]
                ws = []
                for r in range(PEER_BUILD_GROUP):
                    g_row = g8[r:r + 1]
                    g_hi = g_row.astype(BF16).astype(F32)
                    g_lo = g_row - g_hi
                    eq_a = key_id == a8[r:r + 1]
                    x = jnp.concatenate([jnp.where(eq_a, g_hi, 0.0).astype(BF16),
                                         jnp.where(eq_a, g_lo, 0.0).astype(BF16)], axis=1)
                    y1 = jnp.where(key_id == b8[r:r + 1], 1.0, 0.0).astype(BF16)
                    y = jnp.concatenate([y1, y1], axis=1)
                    ws.append(_dot_nt(x, y))
                w3_ref[:, pl.ds(base, PEER_BUILD_GROUP), :] = jnp.swapaxes(
                    jnp.stack(ws, axis=0).astype(BF16), 0, 1)
            return carry
        lax.fori_loop(0, tm // (PEER_BUILD_GROUP * PEER_BUILD_UNROLL), tokens, 0)

    for parity, (h_write, h_read) in enumerate(((ha_ref, hb_ref), (hb_ref, ha_ref))):
        @pl.when((j > 0) & (j < n_e) & (j % 2 == parity))
        def _(h_write=h_write, h_read=h_read):
            h_write[...] = hidden()
            finish(h_read, j - 1)

    @pl.when(j == n_e)
    def _():
        finish(hb_ref if (n_e - 1) % 2 else ha_ref, n_e - 1)
        y = x_ref[...] + g2_ref[0] * acc_ref[...]
        if final:
            ms = jnp.mean(y * y, axis=-1, keepdims=True)
            y = y * lax.rsqrt(ms + RMS_EPS) * fg_ref[...]
        o_ref[...] = y


def peer_experts(h2, a_idx, b_idx, gw, u, v, x1, g2, tokens_per_batch, final_g=None, tm=512, te=1024):
    N, D = x1.shape
    E = u.shape[0]
    tm = min(tm, N)
    nb = g2.shape[0]
    n_e = E // te
    tok = lambda n: pl.BlockSpec((tm, n), lambda i, j: (i, 0))
    in_specs = [tok(D), tok(a_idx.shape[1]), tok(a_idx.shape[1]), tok(a_idx.shape[1]),
                pl.BlockSpec((te, D), lambda i, j: (jnp.minimum(j, n_e - 1), 0)),
                pl.BlockSpec((te, D), lambda i, j: (jnp.maximum(j - 1, 0), 0)),
                tok(D), pl.BlockSpec((1, 1, D), lambda i, j: (i * tm // tokens_per_batch, 0, 0))]
    args = [h2, a_idx, b_idx, gw, u, v, x1, g2.reshape(nb, 1, D)]
    if final_g is not None:
        in_specs.append(pl.BlockSpec((1, D), lambda i, j: (0, 0)))
        args.append(final_g.reshape(1, D))
    return pl.pallas_call(
        functools.partial(_peer_expert_kernel, final=final_g is not None, n_e=n_e),
        grid=(N // tm, n_e + 1),
        in_specs=in_specs,
        out_specs=tok(D),
        out_shape=jax.ShapeDtypeStruct((N, D), F32),
        scratch_shapes=[pltpu.VMEM((PEER_KEYS, tm, PEER_KEYS), BF16), pltpu.VMEM((tm, D), F32),
                        pltpu.VMEM((tm, te), F32), pltpu.VMEM((tm, te), F32)],
        compiler_params=_cparams("parallel", "arbitrary"),
        name="peer_experts",
    )(*args)


def peer_layer(x1, h2, wq, k1, k2, u, v, g2, final_g=None):
    B, T, D = x1.shape
    h2f = h2.reshape(B * T, D)
    a_idx, b_idx, gw = peer_select(h2f, wq, k1, k2)
    out = peer_experts(h2f, a_idx, b_idx, gw, u, v, x1.reshape(B * T, D), g2, T, final_g)
    return out.reshape(B, T, D)


def kernel(x, c, ada_w, ada_b, norm1_g, norm2_g, w_in, conv_dw_w, conv_dw_b, conv_gn_g, conv_gn_b,
           rwkv_mu, rwkv_w0, rwkv_w_up, rwkv_a0, rwkv_a_up, rwkv_g_up, rwkv_k_k, rwkv_k_a, rwkv_r_k,
           rwkv_ln_g, rwkv_ln_b, nsa_ck_pos, nsa_ck_w1, nsa_ck_w2, nsa_cv_pos, nsa_cv_w1, nsa_cv_w2,
           w_out, peer_wq, peer_k1, peer_k2, peer_u, peer_v, final_g):
    depth, D = norm1_g.shape
    d_conv = conv_dw_b.shape[1]
    d_rwkv = rwkv_w0.shape[1]
    n_conv = 2 * d_conv
    n_rwkv = rwkv_mu.shape[1]
    d_nsa = D - d_conv - d_rwkv
    n_kv = 6 * NSA_KV_HEADS * HEAD_DIM
    n_gate = 3 * NSA_KV_HEADS * NSA_GROUP
    cuts = np.cumsum([0, n_conv, n_rwkv, d_nsa, n_kv, n_gate])
    mod = ada_mod(c, ada_w, ada_b)
    for l in range(depth):
        sh1, sc1, g1, sh2, sc2, g2 = [mod[l, :, i * D:(i + 1) * D] for i in range(6)]
        w_l = w_in[l].astype(BF16)
        pieces = [w_l[:, cuts[i]:cuts[i + 1]] for i in range(5)]
        pieces[4] = jnp.pad(pieces[4], ((0, 0), (0, LANES - n_gate)))
        p_conv, p_rwkv, p_q, p_kv, p_gate = proj_in(x, sh1, sc1, norm1_g[l], pieces)
        o_conv = conv_mixer(p_conv, conv_dw_w[l], conv_dw_b[l], conv_gn_g[l], conv_gn_b[l])
        o_rwkv = rwkv_mixer(p_rwkv, rwkv_mu[l], rwkv_w0[l], rwkv_w_up[l], rwkv_a0[l], rwkv_a_up[l],
                            rwkv_g_up[l], rwkv_k_k[l], rwkv_k_a[l], rwkv_r_k[l], rwkv_ln_g[l], rwkv_ln_b[l])
        o_nsa = nsa_mixer(p_q, p_kv, p_gate, nsa_ck_pos[l], nsa_ck_w1[l], nsa_ck_w2[l],
                          nsa_cv_pos[l], nsa_cv_w1[l], nsa_cv_w2[l])
        wo = w_out[l].astype(BF16)
        wo_pieces = [wo[:d_conv], wo[d_conv:d_conv + d_rwkv], wo[d_conv + d_rwkv:]]
        x1, h2 = proj_out(x, g1, sh2, sc2, norm2_g[l], [o_conv, o_rwkv, o_nsa], wo_pieces)
        x = peer_layer(x1, h2, peer_wq[l].astype(BF16), peer_k1[l], peer_k2[l],
                       peer_u[l].astype(BF16), peer_v[l].astype(BF16), g2,
                       final_g if l == depth - 1 else None)
    return x
```

```python
import functools

import jax
import jax.numpy as jnp
import numpy as np
from jax import lax
from jax.experimental import pallas as pl
from jax.experimental.pallas import tpu as pltpu

F32 = jnp.float32
BF16 = jnp.bfloat16
HIGHEST = lax.Precision.HIGHEST

HEAD_DIM = 64
CONV_WIDTH = 31
CONV_EPS = 1e-5
RWKV_GN_EPS = 64e-5
RMS_EPS = 1e-6
LORA_W = 64
LORA_A = 64
LORA_G = 128
NSA_KV_HEADS = 2
NSA_GROUP = 4
CMP_BLOCK = 32
CMP_STRIDE = 16
CMP_HIDDEN = 128
SEL_BLOCK = 64
N_SEL = 16
WINDOW = 512
FORCE_SCORE = 1e4
NEG_INF = -1e30
LOG2E = 1.4426950408889634
PEER_HEADS = 8
PEER_KEYS = 128
PEER_TOPK = 16

LANES = 128
VMEM_LIMIT = 56 * 1024 * 1024


def _dot_hi(a, b):
    return jnp.dot(a, b, precision=HIGHEST, preferred_element_type=F32)


def _dot_bf(a, b):
    return jnp.dot(a.astype(BF16), b.astype(BF16), preferred_element_type=F32)


def _cparams(*sem):
    return pltpu.CompilerParams(dimension_semantics=sem, vmem_limit_bytes=VMEM_LIMIT)


def _group_avg_matrix(n, group):
    i = np.arange(n)
    return jnp.asarray((i[:, None] // group == i[None, :] // group).astype(np.float32) / group)


def _ada_kernel(c_ref, w_ref, b_ref, o_ref):
    c = c_ref[...]
    cs = c * jax.nn.sigmoid(c)
    o_ref[0] = _dot_hi(cs, w_ref[0]) + b_ref[0]


def ada_mod(c, ada_w, ada_b):
    L, D, N = ada_w.shape
    B = c.shape[0]
    bp = 8
    cp = jnp.zeros((bp, D), F32).at[:B].set(c)
    tn = N // 4
    out = pl.pallas_call(
        _ada_kernel,
        grid=(L, N // tn),
        in_specs=[pl.BlockSpec((bp, D), lambda l, j: (0, 0)),
                  pl.BlockSpec((1, D, tn), lambda l, j: (l, 0, j)),
                  pl.BlockSpec((1, 1, tn), lambda l, j: (l, 0, j))],
        out_specs=pl.BlockSpec((1, bp, tn), lambda l, j: (l, 0, j)),
        out_shape=jax.ShapeDtypeStruct((L, bp, N), F32),
        compiler_params=_cparams("parallel", "parallel"),
        name="ada_mod",
    )(cp, ada_w, ada_b.reshape(L, 1, N))
    return out[:, :B]


def _modulated_norm(x, g, sc, sh):
    ms = jnp.mean(x * x, axis=-1, keepdims=True)
    return x * lax.rsqrt(ms + RMS_EPS) * g * (1.0 + sc) + sh


def _proj_in_kernel(x_ref, sh_ref, sc_ref, g_ref, *refs):
    n = len(refs) // 2
    h = _modulated_norm(x_ref[0], g_ref[...], sc_ref[0], sh_ref[0]).astype(BF16)
    for w_ref, o_ref in zip(refs[:n], refs[n:]):
        o_ref[0] = jnp.dot(h, w_ref[...], preferred_element_type=F32)


def proj_in(x, sh, sc, g, weights, tm=512):
    B, T, D = x.shape
    tm = min(tm, T)
    vec = pl.BlockSpec((1, 1, D), lambda b, i: (b, 0, 0))
    in_specs = [pl.BlockSpec((1, tm, D), lambda b, i: (b, i, 0)), vec, vec,
                pl.BlockSpec((1, D), lambda b, i: (0, 0))]
    in_specs += [pl.BlockSpec(w.shape, lambda b, i: (0, 0)) for w in weights]
    out_specs = [pl.BlockSpec((1, tm, w.shape[1]), lambda b, i: (b, i, 0)) for w in weights]
    out_shape = [jax.ShapeDtypeStruct((B, T, w.shape[1]), F32) for w in weights]
    return pl.pallas_call(
        _proj_in_kernel,
        grid=(B, T // tm),
        in_specs=in_specs, out_specs=out_specs, out_shape=out_shape,
        compiler_params=_cparams("parallel", "parallel"),
        name="proj_in",
    )(x, sh.reshape(B, 1, D), sc.reshape(B, 1, D), g.reshape(1, D), *weights)


CONV_HALO = 32


def _conv_kernel(p_ref, w_ref, b_ref, gg_ref, gb_ref, m_ref, o_ref, ext_ref):
    i = pl.program_id(1)
    tt, dc = o_ref.shape[1], o_ref.shape[2]

    @pl.when(i == 0)
    def _():
        ext_ref[0:CONV_HALO, :] = jnp.zeros((CONV_HALO, dc), F32)

    @pl.when(i > 0)
    def _():
        ext_ref[0:CONV_HALO, :] = ext_ref[tt:tt + CONV_HALO, :]

    p = p_ref[0]
    ext_ref[CONV_HALO:CONV_HALO + tt, :] = p[:, :dc] * jax.nn.sigmoid(p[:, dc:])
    off = CONV_HALO - (CONV_WIDTH - 1)
    acc = jnp.zeros((tt, dc), F32) + b_ref[...]
    for j in range(CONV_WIDTH):
        acc = acc + ext_ref[off + j:off + j + tt, :] * w_ref[j:j + 1, :]
    m = m_ref[...]
    mu = _dot_hi(acc, m)
    d = acc - mu
    var = _dot_hi(d * d, m)
    y = d * lax.rsqrt(var + CONV_EPS) * gg_ref[...] + gb_ref[...]
    o_ref[0] = y * jax.nn.sigmoid(y)


def conv_mixer(p, dw_w, dw_b, gn_g, gn_b, tt=512):
    B, T, two_dc = p.shape
    dc = two_dc // 2
    tt = min(tt, T)
    wpad = jnp.zeros((32, dc), F32).at[:CONV_WIDTH].set(dw_w.reshape(CONV_WIDTH, dc))
    row = pl.BlockSpec((1, dc), lambda b, i: (0, 0))
    return pl.pallas_call(
        _conv_kernel,
        grid=(B, T // tt),
        in_specs=[pl.BlockSpec((1, tt, two_dc), lambda b, i: (b, i, 0)),
                  pl.BlockSpec((32, dc), lambda b, i: (0, 0)), row, row, row,
                  pl.BlockSpec((dc, dc), lambda b, i: (0, 0))],
        out_specs=pl.BlockSpec((1, tt, dc), lambda b, i: (b, i, 0)),
        out_shape=jax.ShapeDtypeStruct((B, T, dc), F32),
        scratch_shapes=[pltpu.VMEM((tt + CONV_HALO, dc), F32)],
        compiler_params=_cparams("parallel", "arbitrary"),
        name="conv_mixer",
    )(p, wpad, dw_b.reshape(1, dc), gn_g.reshape(1, dc), gn_b.reshape(1, dc),
      _group_avg_matrix(dc, HEAD_DIM))


def _group_sum_matrix(n, group):
    i = np.arange(n)
    return jnp.asarray((i[:, None] // group == i[None, :] // group).astype(np.float32))


def _rwkv_pre_kernel(p_ref, halo_ref, mu_ref, w0_ref, wup_ref, a0_ref, aup_ref, gup_ref, kk_ref, ka_ref,
                     rk_ref, ones_ref, w_o, kk_o, b_o, k_o, v_o, r_o, bonus_o, g_o, ext_ref):
    i = pl.program_id(1)
    tt = p_ref.shape[1]
    dr = w_o.shape[2]
    p = p_ref[0]
    first = (i > 0).astype(F32)
    ext_ref[0:8, :] = halo_ref[0] * first
    ext_ref[8:8 + tt, :] = p
    prev = ext_ref[7:7 + tt, :]
    xs = p + (prev - p) * mu_ref[...]
    r = xs[:, 0:dr]
    k = xs[:, dr:2 * dr]
    v = xs[:, 2 * dr:3 * dr]
    o = 3 * dr
    xw = xs[:, o:o + LORA_W]
    xa = xs[:, o + LORA_W:o + LORA_W + LORA_A]
    xg = xs[:, o + LORA_W + LORA_A:o + LORA_W + LORA_A + LORA_G]
    z = w0_ref[...] + _dot_hi(jnp.tanh(xw), wup_ref[...])
    w_log = -jax.nn.softplus(-z) - 0.5
    decay = jnp.exp(-jnp.exp(w_log))
    a = jax.nn.sigmoid(a0_ref[...] + _dot_hi(xa, aup_ref[...]))
    g = _dot_hi(jax.nn.sigmoid(xg), gup_ref[...])
    ones = ones_ref[...]
    kk = k * kk_ref[...]
    nrm = jnp.sqrt(_dot_hi(kk * kk, ones))
    kk = kk / jnp.maximum(nrm, 1e-12)
    k_eff = k * (1.0 + (a - 1.0) * ka_ref[...])
    bonus = _dot_hi(r * k_eff * rk_ref[...], ones) * v
    w_o[0] = decay
    kk_o[0] = kk
    b_o[0] = kk * a
    k_o[0] = k_eff
    v_o[0] = v
    r_o[0] = r
    bonus_o[0] = bonus
    g_o[0] = g


def rwkv_pre(p, mu, w0, w_up, a0, a_up, g_up, k_k, k_a, r_k, tt=512):
    B, T, n_in = p.shape
    dr = w0.shape[0]
    tt = min(tt, T)
    nb8 = tt // 8
    full = lambda a: pl.BlockSpec(a.shape, lambda b, i: (0,) * a.ndim)
    args = [mu.reshape(1, n_in), w0.reshape(1, dr), w_up, a0.reshape(1, dr), a_up, g_up,
            k_k.reshape(1, dr), k_a.reshape(1, dr), r_k.reshape(1, dr), _group_sum_matrix(dr, HEAD_DIM)]
    out = jax.ShapeDtypeStruct((B, T, dr), F32)
    ospec = pl.BlockSpec((1, tt, dr), lambda b, i: (b, i, 0))
    return pl.pallas_call(
        _rwkv_pre_kernel,
        grid=(B, T // tt),
        in_specs=[pl.BlockSpec((1, tt, n_in), lambda b, i: (b, i, 0)),
                  pl.BlockSpec((1, 8, n_in), lambda b, i: (b, jnp.maximum(i * nb8 - 1, 0), 0))]
                 + [full(a) for a in args],
        out_specs=[ospec] * 8, out_shape=[out] * 8,
        scratch_shapes=[pltpu.VMEM((tt + 8, n_in), F32)],
        compiler_params=_cparams("parallel", "parallel"),
        name="rwkv_pre",
    )(p, p, *args)


SCAN_SUB = 64
SCAN_CHUNK = 128


def _scan_select_matrices():
    e = np.zeros((SCAN_SUB // 2, 4 * SCAN_SUB, 2 * LANES), np.float32)
    for t in range(SCAN_SUB):
        c = (t % 2) * LANES
        for part in range(2):
            e[t // 2, 2 * part * SCAN_SUB + t, c:c + HEAD_DIM] = 1.0
            e[t // 2, (2 * part + 1) * SCAN_SUB + t, c + HEAD_DIM:c + LANES] = 1.0
    return jnp.asarray(e, BF16)


def _rwkv_scan_kernel(w_ref, kk_ref, b_ref, k_ref, r_ref, v_ref, bonus_ref, g_ref, lng_ref, lnb_ref,
                      e_ref, avg_ref, o_ref, st_ref, lhs_ref, y_ref):
    nb, tc, dr = v_ref.shape
    n_half = dr // LANES
    n_pair = nb * n_half
    n_sub = tc // SCAN_SUB
    quantities = (w_ref, kk_ref, b_ref, k_ref, r_ref)

    @pl.when(pl.program_id(0) == 0)
    def _():
        st_ref[...] = jnp.zeros(st_ref.shape, F32)

    for qi, q_ref in enumerate(quantities):
        for p in range(n_pair):
            b, hb = divmod(p, n_half)
            xt = q_ref[b, :, hb * LANES:(hb + 1) * LANES].T
            for s in range(n_sub):
                cat = jnp.concatenate([xt[0:HEAD_DIM, s * SCAN_SUB:(s + 1) * SCAN_SUB],
                                       xt[HEAD_DIM:, s * SCAN_SUB:(s + 1) * SCAN_SUB]], axis=1)
                hi = cat.astype(BF16)
                lo = (cat - hi.astype(F32)).astype(BF16)
                row = (qi * n_pair + p) * HEAD_DIM
                lhs_ref[s, row:row + HEAD_DIM, :] = jnp.concatenate([hi, lo], axis=1)

    sub_iota = lax.broadcasted_iota(jnp.int32, (8, LANES), 0)
    for s in range(n_sub):
        def steps8(t8, carry, s=s):
            base = pl.multiple_of(s * SCAN_SUB + t8 * 8, 8)
            y8 = [jnp.zeros((8, LANES), F32) for _ in range(n_pair)]
            for j in range(8):
                if j % 2 == 0:
                    z2 = jnp.dot(lhs_ref[s], e_ref[t8 * 4 + j // 2], preferred_element_type=F32)
                z = z2[:, (j % 2) * LANES:(j % 2 + 1) * LANES]
                for p in range(n_pair):
                    b, hb = divmod(p, n_half)
                    col = lambda qi: z[(qi * n_pair + p) * HEAD_DIM:(qi * n_pair + p + 1) * HEAD_DIM, :]
                    st = st_ref[p]
                    sa = jnp.sum(st * col(1), axis=0, keepdims=True)
                    v_row = v_ref[b, pl.ds(base, 8), hb * LANES:(hb + 1) * LANES][j:j + 1, :]
                    st = st * col(0) - col(2) * sa + col(3) * v_row
                    y = jnp.sum(st * col(4), axis=0, keepdims=True)
                    y8[p] = jnp.where(sub_iota == j, y, y8[p])
                    st_ref[p] = st
            for p in range(n_pair):
                b, hb = divmod(p, n_half)
                y_ref[b, pl.ds(base, 8), hb * LANES:(hb + 1) * LANES] = y8[p]
            return carry
        lax.fori_loop(0, SCAN_SUB // 8, steps8, 0)

    avg = avg_ref[...]
    for b in range(nb):
        y = y_ref[b]
        mu = _dot_hi(y, avg)
        d = y - mu
        var = _dot_hi(d * d, avg)
        yn = d * lax.rsqrt(var + RWKV_GN_EPS) * lng_ref[...] + lnb_ref[...]
        o_ref[b] = (yn + bonus_ref[b]) * g_ref[b]


def rwkv_scan(w, kk, bq, k, r, v, bonus, g, ln_g, ln_b):
    B, T, dr = v.shape
    tc = min(SCAN_CHUNK, T)
    n_pair = B * dr // LANES
    blk = pl.BlockSpec((B, tc, dr), lambda i: (0, i, 0))
    row = pl.BlockSpec((1, dr), lambda i: (0, 0))
    e = _scan_select_matrices()
    return pl.pallas_call(
        _rwkv_scan_kernel,
        grid=(T // tc,),
        in_specs=[blk] * 8 + [row, row, pl.BlockSpec(e.shape, lambda i: (0, 0, 0)),
                              pl.BlockSpec((dr, dr), lambda i: (0, 0))],
        out_specs=blk,
        out_shape=jax.ShapeDtypeStruct((B, T, dr), F32),
        scratch_shapes=[pltpu.VMEM((n_pair, HEAD_DIM, LANES), F32),
                        pltpu.VMEM((tc // SCAN_SUB, 5 * n_pair * HEAD_DIM, 4 * SCAN_SUB), BF16),
                        pltpu.VMEM((B, tc, dr), F32)],
        compiler_params=_cparams("arbitrary"),
        name="rwkv_scan",
    )(w, kk, bq, k, r, v, bonus, g, ln_g.reshape(1, dr), ln_b.reshape(1, dr), e, _group_avg_matrix(dr, HEAD_DIM))


def rwkv_mixer(p, mu, w0, w_up, a0, a_up, g_up, k_k, k_a, r_k, ln_g, ln_b):
    w, kk, bq, k, v, r, bonus, g = rwkv_pre(p, mu, w0, w_up, a0, a_up, g_up, k_k, k_a, r_k.reshape(-1))
    return rwkv_scan(w, kk, bq, k, r, v, bonus, g, ln_g, ln_b)


def _dot_nt(a, b, **kw):
    return lax.dot_general(a, b, (((1,), (1,)), ((), ())), preferred_element_type=F32, **kw)


def _compress_kernel(kz_ref, vz_ref, kpos_ref, kw1_ref, kw2_ref, vpos_ref, vw1_ref, vw2_ref,
                     kc_ref, vc_ref, shift_ref):
    n = kz_ref.shape[2]
    half = kz_ref.shape[3]

    def one(z_ref, pos_ref, w1_ref, w2_ref, o_ref):
        z = z_ref[0, 0]
        top = _dot_bf(z + pos_ref[0:1, :], w1_ref[0:half, :])
        bot = _dot_bf(z + pos_ref[1:2, :], w1_ref[half:2 * half, :])
        shift_ref[0:n, :] = bot
        shift_ref[n:n + 8, :] = jnp.zeros((8, bot.shape[1]), F32)
        pre = top + shift_ref[1:n + 1, :]
        hid = pre * jax.nn.sigmoid(pre)
        o_ref[0, 0] = _dot_bf(hid, w2_ref[...])

    one(kz_ref, kpos_ref, kw1_ref, kw2_ref, kc_ref)
    one(vz_ref, vpos_ref, vw1_ref, vw2_ref, vc_ref)


def nsa_compress(k_cmp, v_cmp, ck_pos, ck_w1, ck_w2, cv_pos, cv_w1, cv_w2):
    B, G, T, dk = k_cmp.shape
    n = T // CMP_STRIDE
    half = CMP_STRIDE * dk
    zspec = pl.BlockSpec((1, 1, n, half), lambda b, g: (b, g, 0, 0))
    full = lambda a: pl.BlockSpec(a.shape, lambda b, g: (0,) * a.ndim)
    ospec = pl.BlockSpec((1, 1, n, dk), lambda b, g: (b, g, 0, 0))
    args = [ck_pos.reshape(2, half), ck_w1.astype(BF16), ck_w2.astype(BF16),
            cv_pos.reshape(2, half), cv_w1.astype(BF16), cv_w2.astype(BF16)]
    return pl.pallas_call(
        _compress_kernel,
        grid=(B, G),
        in_specs=[zspec, zspec] + [full(a) for a in args],
        out_specs=[ospec, ospec],
        out_shape=[jax.ShapeDtypeStruct((B, G, n, dk), F32)] * 2,
        scratch_shapes=[pltpu.VMEM((n + 8, CMP_HIDDEN), F32)],
        compiler_params=_cparams("parallel", "parallel"),
        name="nsa_compress",
    )(k_cmp.reshape(B, G, n, half), v_cmp.reshape(B, G, n, half), *args)


NSA_TQ = 128
NSA_TK = 1024


def _split_bf16(a):
    hi = a.astype(BF16)
    return hi, (a - hi.astype(F32)).astype(BF16)


def _dot_3pass(a, b):
    ah, al = _split_bf16(a)
    bh, bl = _split_bf16(b)
    d = lambda x, y: jnp.dot(x, y, preferred_element_type=F32)
    return d(ah, bh) + d(ah, bl) + d(al, bh)


def _nsa_kernel(q_ref, gate_ref, kc_ref, vct_ref, ks_ref, vst_ref, kw_ref, vwt_ref, ovt_ref,
                o_ref, m_ref, acc_ref, ow_ref, oc_ref, imp_ref, sel_ref, s0_ref, s1_ref, p_ref):
    qi = pl.program_id(2)
    dk = q_ref.shape[3]
    tq = gate_ref.shape[3]
    R = q_ref.shape[4] // tq
    n_c = kc_ref.shape[2]
    n_blk = ovt_ref.shape[0]
    T = ks_ref.shape[2]
    tk = s0_ref.shape[0]
    t0 = qi * tq
    qt = q_ref[0, 0, 0]
    qt_bf = qt.astype(BF16)
    t_row = t0 + lax.broadcasted_iota(jnp.int32, (1, tq), 1)
    lanes = lambda r: slice(r * tq, (r + 1) * tq)

    any_c = (t_row >= CMP_BLOCK - 1).astype(F32)
    n_need = (t0 + tq - CMP_BLOCK) // CMP_STRIDE + 1

    def compressed(n_eff):
        s = _dot_3pass(kc_ref[0, 0, 0:n_eff, :], qt)
        c_end = lax.broadcasted_iota(jnp.int32, (n_eff, tq), 0) * CMP_STRIDE + (CMP_BLOCK - 1)
        valid_c = c_end <= t_row
        p_sum = jnp.zeros((n_eff, tq), F32)
        ps = []
        for r in range(R):
            s_r = jnp.where(valid_c, s[:, lanes(r)], NEG_INF)
            e = jnp.exp2(s_r - jnp.max(s_r, axis=0, keepdims=True))
            p_r = e * (any_c / jnp.sum(e, axis=0, keepdims=True))
            p_sum = p_sum + p_r
            ps.append(p_r.astype(BF16))
        oc_ref[...] = jnp.dot(vct_ref[0, 0, :, 0:n_eff].astype(BF16), jnp.concatenate(ps, axis=1),
                              preferred_element_type=F32)
        imp_ref[...] = _dot_hi(ovt_ref[:, 0:n_eff], p_sum)

    quarter = n_c // 4
    for v in range(4):
        lo, hi = v * quarter, (v + 1) * quarter
        cond = (n_need <= hi) if v == 0 else (n_need > lo) if v == 3 else (n_need > lo) & (n_need <= hi)
        pl.when(cond)(functools.partial(compressed, hi))
    o_c = oc_ref[...]
    imp = imp_ref[...]

    blk = lax.broadcasted_iota(jnp.int32, (n_blk, tq), 0).astype(F32)
    cur = (t_row // SEL_BLOCK).astype(F32)
    forced = (blk == 0.0) | (blk == cur) | (blk == cur - 1.0)
    x = jnp.where(forced, FORCE_SCORE, jnp.where(blk <= cur, imp, -1.0))
    x = jnp.where(blk < float(T // SEL_BLOCK), x, -3e38)
    sel = jnp.zeros((n_blk, tq), F32)
    for _ in range(N_SEL):
        mx = jnp.max(x, axis=0, keepdims=True)
        idx = jnp.min(jnp.where(x == mx, blk, float(n_blk)), axis=0, keepdims=True)
        hit = blk == idx
        sel = jnp.where(hit, 1.0, sel)
        x = jnp.where(hit, -jnp.inf, x)

    span = WINDOW + tq
    w0 = pl.multiple_of(jnp.maximum(t0 - WINDOW, 0), tq)
    s = jnp.dot(kw_ref[0, 0, pl.ds(w0, span), :], qt_bf, preferred_element_type=F32)
    dist = t_row - (w0 + lax.broadcasted_iota(jnp.int32, (span, tq), 0))
    bias_w = jnp.where((dist >= 0) & (dist < WINDOW), 0.0, NEG_INF)
    ps = []
    for r in range(R):
        s_r = s[:, lanes(r)] + bias_w
        ps.append(jnp.exp2(s_r - jnp.max(s_r, axis=0, keepdims=True)).astype(BF16))
    ow_ref[...] = jnp.dot(vwt_ref[0, 0, :, pl.ds(w0, span)], jnp.concatenate(ps, axis=1),
                          preferred_element_type=F32)

    bpt = tk // SEL_BLOCK
    n_kt = T // tk
    n_full = (t0 + tq - 1) // tk
    sel_ref[...] = jnp.where(sel > 0.5, 0.0, NEG_INF)
    m_ref[...] = jnp.full(m_ref.shape, NEG_INF, F32)
    acc_ref[...] = jnp.zeros(acc_ref.shape, F32)
    q_pad = jnp.zeros((ks_ref.shape[3] - dk - bpt, R * tq), BF16)

    def scores(kt):
        kt = jnp.minimum(kt, n_kt - 1)
        rows = sel_ref[pl.ds(pl.multiple_of(kt * bpt, bpt), bpt), :].astype(BF16)
        q_aug = jnp.concatenate([qt_bf, jnp.concatenate([rows] * R, axis=1), q_pad], axis=0)
        off = pl.multiple_of(kt * tk, tk)
        return jnp.dot(ks_ref[0, 0, pl.ds(off, tk), :], q_aug, preferred_element_type=F32)

    def softmax_tile(s_ref, kt, causal):
        if causal:
            key_pos = kt * tk + lax.broadcasted_iota(jnp.int32, (tk, tq), 0)
            bias = jnp.where(key_pos <= t_row, 0.0, NEG_INF)
        for r in range(R):
            m_old = m_ref[:, lanes(r)]
            if causal:
                m_new = jnp.maximum(m_old, jnp.max(s_ref[:, lanes(r)] + bias, axis=0, keepdims=True))
                p_r = jnp.exp2((s_ref[:, lanes(r)] - m_new) + bias)
            else:
                m_new = jnp.maximum(m_old, jnp.max(s_ref[:, lanes(r)], axis=0, keepdims=True))
                p_r = jnp.exp2(s_ref[:, lanes(r)] - m_new)
            m_ref[:, lanes(r)] = m_new
            p_ref[:, lanes(r)] = p_r.astype(BF16)
            acc_ref[:, lanes(r)] = acc_ref[:, lanes(r)] * jnp.exp2(m_old - m_new)
        off = pl.multiple_of(kt * tk, tk)
        acc_ref[...] += jnp.dot(vst_ref[0, 0, :, pl.ds(off, tk)], p_ref[...], preferred_element_type=F32)

    s0_ref[...] = scores(0)

    def tile_pair(i, carry):
        s1_ref[...] = scores(2 * i + 1)
        softmax_tile(s0_ref, 2 * i, False)
        s0_ref[...] = scores(2 * i + 2)
        softmax_tile(s1_ref, 2 * i + 1, False)
        return carry

    lax.fori_loop(0, n_full // 2, tile_pair, 0)
    odd = n_full % 2 == 1

    @pl.when(odd)
    def _():
        s1_ref[...] = scores(n_full)
        softmax_tile(s0_ref, n_full - 1, False)
        softmax_tile(s1_ref, n_full, True)

    @pl.when(jnp.logical_not(odd))
    def _():
        softmax_tile(s0_ref, n_full, True)

    acc = acc_ref[...]
    o_s = acc[0:dk] * (1.0 / acc[dk:dk + 1])
    o_w = ow_ref[...]
    o_w = o_w[0:dk] * (1.0 / o_w[dk:dk + 1])

    gate = jax.nn.sigmoid(gate_ref[0, 0])
    outs = []
    for r in range(R):
        outs.append(gate[r:r + 1] * o_c[:, lanes(r)] + gate[R + r:R + r + 1] * o_s[:, lanes(r)]
                    + gate[2 * R + r:2 * R + r + 1] * o_w[:, lanes(r)])
    o_ref[0] = jnp.concatenate(outs, axis=0).T


def _overlap_matrix_t(n_c, n_blk):
    c0 = np.arange(n_c)[None, :] * CMP_STRIDE
    s0 = np.arange(n_blk)[:, None] * SEL_BLOCK
    return jnp.asarray(((c0 < s0 + SEL_BLOCK) & (c0 + CMP_BLOCK > s0)).astype(np.float32))


def nsa_attention(qt, gate_t, kc, vct, k_sel, vt_sel, k_win, vt_win):
    B, G, nq, dk, L = qt.shape
    T = k_win.shape[2]
    tq = T // nq
    R = L // tq
    n_c = kc.shape[2]
    tk = min(NSA_TK, T)
    n_blk = max(T // SEL_BLOCK, LANES)
    ovt = _overlap_matrix_t(n_c, n_blk)
    res = lambda a: pl.BlockSpec((1, 1) + a.shape[2:], lambda b, g, i: (b, g, 0, 0))
    return pl.pallas_call(
        _nsa_kernel,
        grid=(B, G, nq),
        in_specs=[pl.BlockSpec((1, 1, 1, dk, L), lambda b, g, i: (b, g, i, 0, 0)),
                  pl.BlockSpec((1, 1, gate_t.shape[2], tq), lambda b, g, i: (b, g, 0, i)),
                  res(kc), res(vct), res(k_sel), res(vt_sel), res(k_win), res(vt_win),
                  pl.BlockSpec(ovt.shape, lambda b, g, i: (0, 0))],
        out_specs=pl.BlockSpec((1, tq, R * dk), lambda b, g, i: (b, i, g)),
        out_shape=jax.ShapeDtypeStruct((B, T, G * R * dk), F32),
        scratch_shapes=[pltpu.VMEM((1, L), F32), pltpu.VMEM((dk + 16, L), F32),
                        pltpu.VMEM((dk + 16, L), F32), pltpu.VMEM((dk, L), F32), pltpu.VMEM((n_blk, tq), F32),
                        pltpu.VMEM((n_blk, tq), F32),
                        pltpu.VMEM((tk, L), F32), pltpu.VMEM((tk, L), F32), pltpu.VMEM((tk, L), BF16)],
        compiler_params=_cparams("parallel", "parallel", "arbitrary"),
        name="nsa_attention",
    )(qt, gate_t, kc, vct, k_sel, vt_sel, k_win, vt_win, ovt)


def nsa_mixer(p_q, p_kv, p_gate, ck_pos, ck_w1, ck_w2, cv_pos, cv_w1, cv_w2):
    B, T, _ = p_q.shape
    G, R, dk = NSA_KV_HEADS, NSA_GROUP, HEAD_DIM
    tq = min(NSA_TQ, T)
    kv6 = p_kv.reshape(B, T, 6, G, dk)
    rows = lambda i: kv6[:, :, i].transpose(0, 2, 1, 3)
    cols = lambda i: kv6[:, :, i].transpose(0, 2, 3, 1)
    kc, vc = nsa_compress(rows(0), rows(1), ck_pos, ck_w1, ck_w2, cv_pos, cv_w1, cv_w2)
    qt = (p_q * (dk ** -0.5 * LOG2E)).reshape(B, T // tq, tq, G, R, dk).transpose(0, 3, 1, 5, 4, 2)
    qt = qt.reshape(B, G, T // tq, dk, R * tq)
    gate_t = p_gate[..., :3 * G * R].reshape(B, T, 3, G, R).transpose(0, 3, 2, 4, 1).reshape(B, G, 3 * R, T)
    gate_t = jnp.pad(gate_t, ((0, 0), (0, 0), (0, 16 - 3 * R), (0, 0)))
    tk = min(NSA_TK, T)
    blk_onehot = (jnp.arange(T)[:, None] // SEL_BLOCK % (tk // SEL_BLOCK) == jnp.arange(dk)[None, :]).astype(BF16)
    k_sel = jnp.concatenate([rows(2).astype(BF16), jnp.broadcast_to(blk_onehot, (B, G, T, dk))], axis=-1)
    ones_rows = jnp.zeros((B, G, 16, T), BF16).at[:, :, 0].set(1.0)
    with_ones = lambda vt: jnp.concatenate([vt.astype(BF16), ones_rows], axis=2)
    return nsa_attention(qt, gate_t, kc, vc.transpose(0, 1, 3, 2), k_sel, with_ones(cols(3)),
                         rows(4).astype(BF16), with_ones(cols(5)))


def _proj_out_kernel(x_ref, g1_ref, sh_ref, sc_ref, ng_ref, *refs):
    n = (len(refs) - 2) // 2
    x1_ref, h2_ref = refs[2 * n:]
    acc = None
    for m_ref, w_ref in zip(refs[:n], refs[n:2 * n]):
        d = jnp.dot(m_ref[0].astype(BF16), w_ref[...], preferred_element_type=F32)
        acc = d if acc is None else acc + d
    x1 = x_ref[0] + g1_ref[0] * acc
    x1_ref[0] = x1
    h2_ref[0] = _modulated_norm(x1, ng_ref[...], sc_ref[0], sh_ref[0]).astype(BF16)


def proj_out(x, g1, sh2, sc2, norm_g, mixes, weights, tm=512):
    B, T, D = x.shape
    tm = min(tm, T)
    vec = pl.BlockSpec((1, 1, D), lambda b, i: (b, 0, 0))
    tile = lambda n: pl.BlockSpec((1, tm, n), lambda b, i: (b, i, 0))
    return pl.pallas_call(
        _proj_out_kernel,
        grid=(B, T // tm),
        in_specs=[tile(D), vec, vec, vec, pl.BlockSpec((1, D), lambda b, i: (0, 0))]
                 + [tile(m.shape[2]) for m in mixes]
                 + [pl.BlockSpec(w.shape, lambda b, i: (0, 0)) for w in weights],
        out_specs=[tile(D), tile(D)],
        out_shape=[jax.ShapeDtypeStruct((B, T, D), F32), jax.ShapeDtypeStruct((B, T, D), BF16)],
        compiler_params=_cparams("parallel", "parallel"),
        name="proj_out",
    )(x, g1.reshape(B, 1, D), sh2.reshape(B, 1, D), sc2.reshape(B, 1, D), norm_g.reshape(1, D), *mixes, *weights)


_CAND_BLOCKS = [(0, 16), (1, 8), (None, 8), (2, 5), (3, 4), (4, 3), (5, 2), (6, 2), (7, 2)]
_CAND_PAD = 6


def _cand_constants(tm):
    flat, neg = [], []
    for i, rows in _CAND_BLOCKS:
        for r in range(rows):
            flat.append((8 + r) * PEER_TOPK if i is None else i * PEER_TOPK + r)
            neg.append(0.0)
    flat += [1e9] * _CAND_PAD
    neg += [-np.inf] * _CAND_PAD
    flat = np.tile(np.asarray(flat, np.float32)[:, None], (1, tm))
    neg = np.tile(np.asarray(neg, np.float32)[:, None], (1, tm))
    return jnp.asarray(flat), jnp.asarray(neg)


def _topk_rows(x, row_id, k, n_rows):
    vals, idxs = [], []
    for _ in range(k):
        mx = jnp.max(x, axis=0, keepdims=True)
        idx = jnp.min(jnp.where(x == mx, row_id, float(n_rows)), axis=0, keepdims=True)
        x = jnp.where(row_id == idx, -jnp.inf, x)
        vals.append(mx)
        idxs.append(idx)
    return jnp.concatenate(vals, axis=0), jnp.concatenate(idxs, axis=0)


def _peer_select_kernel(h_ref, wq_ref, k1_ref, k2_ref, flat_ref, neg_ref, a_ref, b_ref, g_ref):
    tm = h_ref.shape[0]
    nk = k1_ref.shape[0]
    half = k1_ref.shape[1]
    q = jnp.dot(h_ref[...], wq_ref[...], preferred_element_type=F32)
    key_id = lax.broadcasted_iota(jnp.int32, (nk, tm), 0).astype(F32)
    flat = flat_ref[...]
    neg = neg_ref[...]
    k1 = k1_ref[...]
    k2 = k2_ref[...]
    a_all, b_all, g_all = [], [], []
    for h in range(PEER_HEADS):
        q1 = q[:, (2 * h) * half:(2 * h + 1) * half]
        q2 = q[:, (2 * h + 1) * half:(2 * h + 2) * half]
        v1, i1 = _topk_rows(_dot_nt(k1, q1, precision=HIGHEST), key_id, PEER_TOPK, nk)
        v2, i2 = _topk_rows(_dot_nt(k2, q2, precision=HIGHEST), key_id, PEER_TOPK, nk)
        vals, ai, bj = [], [], []
        for i, rows in _CAND_BLOCKS:
            if i is None:
                vals.append(v1[8:16] + v2[0:1])
                ai.append(i1[8:16])
                bj.append(jnp.broadcast_to(i2[0:1], (8, tm)))
            else:
                vals.append(v1[i:i + 1] + v2[0:rows])
                ai.append(jnp.broadcast_to(i1[i:i + 1], (rows, tm)))
                bj.append(i2[0:rows])
        pad = [jnp.zeros((_CAND_PAD, tm), F32)]
        cand = jnp.concatenate(vals + pad, axis=0) + neg
        ai = jnp.concatenate(ai + pad, axis=0)
        bj = jnp.concatenate(bj + pad, axis=0)
        sc, sa, sb = [], [], []
        for _ in range(PEER_TOPK):
            mx = jnp.max(cand, axis=0, keepdims=True)
            fid = jnp.min(jnp.where(cand == mx, flat, 1e9), axis=0, keepdims=True)
            hit = flat == fid
            sc.append(mx)
            sa.append(jnp.sum(jnp.where(hit, ai, 0.0), axis=0, keepdims=True))
            sb.append(jnp.sum(jnp.where(hit, bj, 0.0), axis=0, keepdims=True))
            cand = jnp.where(hit, -jnp.inf, cand)
        sc = jnp.concatenate(sc, axis=0)
        e = jnp.exp(sc - sc[0:1])
        g_all.append(e / jnp.sum(e, axis=0, keepdims=True))
        a_all.append(jnp.concatenate(sa, axis=0))
        b_all.append(jnp.concatenate(sb, axis=0))
    a_ref[...] = jnp.concatenate(a_all, axis=0).T
    b_ref[...] = jnp.concatenate(b_all, axis=0).T
    g_ref[...] = jnp.concatenate(g_all, axis=0).T


def peer_select(h2, wq, k1, k2, tm=256):
    N, D = h2.shape
    tm = min(tm, N)
    flat, neg = _cand_constants(tm)
    full = lambda a: pl.BlockSpec(a.shape, lambda i: (0,) * a.ndim)
    out = jax.ShapeDtypeStruct((N, PEER_HEADS * PEER_TOPK), F32)
    ospec = pl.BlockSpec((tm, PEER_HEADS * PEER_TOPK), lambda i: (i, 0))
    return pl.pallas_call(
        _peer_select_kernel,
        grid=(N // tm,),
        in_specs=[pl.BlockSpec((tm, D), lambda i: (i, 0)), full(wq), full(k1), full(k2), full(flat), full(neg)],
        out_specs=[ospec] * 3, out_shape=[out] * 3,
        compiler_params=_cparams("parallel"),
        name="peer_select",
    )(h2, wq, k1, k2, flat, neg)


PEER_BUILD_GROUP = 16
PEER_BUILD_UNROLL = 2


def _gelu(x):
    return 0.5 * x * (1.0 + lax.erf(x * 0.7071067811865476))


def _peer_expert_kernel(*refs, final, n_e):
    if final:
        h_ref, a_ref, b_ref, g_ref, u_ref, v_ref, x_ref, g2_ref, fg_ref, o_ref, w3_ref, acc_ref, ha_ref, hb_ref = refs
    else:
        h_ref, a_ref, b_ref, g_ref, u_ref, v_ref, x_ref, g2_ref, o_ref, w3_ref, acc_ref, ha_ref, hb_ref = refs
    j = pl.program_id(1)
    tm = h_ref.shape[0]
    te = u_ref.shape[0]
    nk = PEER_KEYS
    n_a = te // nk

    def hidden():
        return _dot_nt(h_ref[...], u_ref[...])

    def finish(hid_ref, tile):
        acts = []
        for al in range(n_a):
            w_a = w3_ref[tile * n_a + al].astype(F32)
            acts.append((_gelu(hid_ref[:, al * nk:(al + 1) * nk]) * w_a).astype(BF16))
        acc_ref[...] += jnp.dot(jnp.concatenate(acts, axis=1), v_ref[...], preferred_element_type=F32)

    @pl.when(j == 0)
    def _():
        acc_ref[...] = jnp.zeros(acc_ref.shape, F32)
        ha_ref[...] = hidden()
        key_id = lax.broadcasted_iota(jnp.int32, (nk, a_ref.shape[1]), 0).astype(F32)

        def tokens(tb, carry):
            for grp in range(PEER_BUILD_UNROLL):
                base = pl.multiple_of((tb * PEER_BUILD_UNROLL + grp) * PEER_BUILD_GROUP, PEER_BUILD_GROUP)
                a8 = a_ref[pl.ds(base, PEER_BUILD_GROUP), :]
                b8 = b_ref[pl.ds(base, PEER_BUILD_GROUP), :]
                g8 = g_ref[pl.ds(base, PEER_BUILD_GROUP), :]
                ws = []
                for r in range(PEER_BUILD_GROUP):
                    g_row = g8[r:r + 1]
                    g_hi = g_row.astype(BF16).astype(F32)
                    g_lo = g_row - g_hi
                    eq_a = key_id == a8[r:r + 1]
                    x = jnp.concatenate([jnp.where(eq_a, g_hi, 0.0).astype(BF16),
                                         jnp.where(eq_a, g_lo, 0.0).astype(BF16)], axis=1)
                    y1 = jnp.where(key_id == b8[r:r + 1], 1.0, 0.0).astype(BF16)
                    y = jnp.concatenate([y1, y1], axis=1)
                    ws.append(_dot_nt(x, y))
                w3_ref[:, pl.ds(base, PEER_BUILD_GROUP), :] = jnp.swapaxes(
                    jnp.stack(ws, axis=0).astype(BF16), 0, 1)
            return carry
        lax.fori_loop(0, tm // (PEER_BUILD_GROUP * PEER_BUILD_UNROLL), tokens, 0)

    for parity, (h_write, h_read) in enumerate(((ha_ref, hb_ref), (hb_ref, ha_ref))):
        @pl.when((j > 0) & (j < n_e) & (j % 2 == parity))
        def _(h_write=h_write, h_read=h_read):
            h_write[...] = hidden()
            finish(h_read, j - 1)

    @pl.when(j == n_e)
    def _():
        finish(hb_ref if (n_e - 1) % 2 else ha_ref, n_e - 1)
        y = x_ref[...] + g2_ref[0] * acc_ref[...]
        if final:
            ms = jnp.mean(y * y, axis=-1, keepdims=True)
            y = y * lax.rsqrt(ms + RMS_EPS) * fg_ref[...]
        o_ref[...] = y


def peer_experts(h2, a_idx, b_idx, gw, u, v, x1, g2, tokens_per_batch, final_g=None, tm=512, te=1024):
    N, D = x1.shape
    E = u.shape[0]
    tm = min(tm, N)
    nb = g2.shape[0]
    n_e = E // te
    tok = lambda n: pl.BlockSpec((tm, n), lambda i, j: (i, 0))
    in_specs = [tok(D), tok(a_idx.shape[1]), tok(a_idx.shape[1]), tok(a_idx.shape[1]),
                pl.BlockSpec((te, D), lambda i, j: (jnp.minimum(j, n_e - 1), 0)),
                pl.BlockSpec((te, D), lambda i, j: (jnp.maximum(j - 1, 0), 0)),
                tok(D), pl.BlockSpec((1, 1, D), lambda i, j: (i * tm // tokens_per_batch, 0, 0))]
    args = [h2, a_idx, b_idx, gw, u, v, x1, g2.reshape(nb, 1, D)]
    if final_g is not None:
        in_specs.append(pl.BlockSpec((1, D), lambda i, j: (0, 0)))
        args.append(final_g.reshape(1, D))
    return pl.pallas_call(
        functools.partial(_peer_expert_kernel, final=final_g is not None, n_e=n_e),
        grid=(N // tm, n_e + 1),
        in_specs=in_specs,
        out_specs=tok(D),
        out_shape=jax.ShapeDtypeStruct((N, D), F32),
        scratch_shapes=[pltpu.VMEM((PEER_KEYS, tm, PEER_KEYS), BF16), pltpu.VMEM((tm, D), F32),
                        pltpu.VMEM((tm, te), F32), pltpu.VMEM((tm, te), F32)],
        compiler_params=_cparams("parallel", "arbitrary"),
        name="peer_experts",
    )(*args)


def peer_layer(x1, h2, wq, k1, k2, u, v, g2, final_g=None):
    B, T, D = x1.shape
    h2f = h2.reshape(B * T, D)
    a_idx, b_idx, gw = peer_select(h2f, wq, k1, k2)
    out = peer_experts(h2f, a_idx, b_idx, gw, u, v, x1.reshape(B * T, D), g2, T, final_g)
    return out.reshape(B, T, D)


def kernel(x, c, ada_w, ada_b, norm1_g, norm2_g, w_in, conv_dw_w, conv_dw_b, conv_gn_g, conv_gn_b,
           rwkv_mu, rwkv_w0, rwkv_w_up, rwkv_a0, rwkv_a_up, rwkv_g_up, rwkv_k_k, rwkv_k_a, rwkv_r_k,
           rwkv_ln_g, rwkv_ln_b, nsa_ck_pos, nsa_ck_w1, nsa_ck_w2, nsa_cv_pos, nsa_cv_w1, nsa_cv_w2,
           w_out, peer_wq, peer_k1, peer_k2, peer_u, peer_v, final_g):
    depth, D = norm1_g.shape
    d_conv = conv_dw_b.shape[1]
    d_rwkv = rwkv_w0.shape[1]
    n_conv = 2 * d_conv
    n_rwkv = rwkv_mu.shape[1]
    d_nsa = D - d_conv - d_rwkv
    n_kv = 6 * NSA_KV_HEADS * HEAD_DIM
    n_gate = 3 * NSA_KV_HEADS * NSA_GROUP
    cuts = np.cumsum([0, n_conv, n_rwkv, d_nsa, n_kv, n_gate])
    mod = ada_mod(c, ada_w, ada_b)
    for l in range(depth):
        sh1, sc1, g1, sh2, sc2, g2 = [mod[l, :, i * D:(i + 1) * D] for i in range(6)]
        w_l = w_in[l].astype(BF16)
        pieces = [w_l[:, cuts[i]:cuts[i + 1]] for i in range(5)]
        pieces[4] = jnp.pad(pieces[4], ((0, 0), (0, LANES - n_gate)))
        p_conv, p_rwkv, p_q, p_kv, p_gate = proj_in(x, sh1, sc1, norm1_g[l], pieces)
        o_conv = conv_mixer(p_conv, conv_dw_w[l], conv_dw_b[l], conv_gn_g[l], conv_gn_b[l])
        o_rwkv = rwkv_mixer(p_rwkv, rwkv_mu[l], rwkv_w0[l], rwkv_w_up[l], rwkv_a0[l], rwkv_a_up[l],
                            rwkv_g_up[l], rwkv_k_k[l], rwkv_k_a[l], rwkv_r_k[l], rwkv_ln_g[l], rwkv_ln_b[l])
        o_nsa = nsa_mixer(p_q, p_kv, p_gate, nsa_ck_pos[l], nsa_ck_w1[l], nsa_ck_w2[l],
                          nsa_cv_pos[l], nsa_cv_w1[l], nsa_cv_w2[l])
        wo = w_out[l].astype(BF16)
        wo_pieces = [wo[:d_conv], wo[d_conv:d_conv + d_rwkv], wo[d_conv + d_rwkv:]]
        x1, h2 = proj_out(x, g1, sh2, sc2, norm2_g[l], [o_conv, o_rwkv, o_nsa], wo_pieces)
        x = peer_layer(x1, h2, peer_wq[l].astype(BF16), peer_k1[l], peer_k2[l],
                       peer_u[l].astype(BF16), peer_v[l].astype(BF16), g2,
                       final_g if l == depth - 1 else None)
    return x
```

```python
import functools

import jax
import jax.numpy as jnp
import numpy as np
from jax import lax
from jax.experimental import pallas as pl
from jax.experimental.pallas import tpu as pltpu

F32 = jnp.float32
BF16 = jnp.bfloat16
HIGHEST = lax.Precision.HIGHEST

HEAD_DIM = 64
CONV_WIDTH = 31
CONV_EPS = 1e-5
RWKV_GN_EPS = 64e-5
RMS_EPS = 1e-6
LORA_W = 64
LORA_A = 64
LORA_G = 128
NSA_KV_HEADS = 2
NSA_GROUP = 4
CMP_BLOCK = 32
CMP_STRIDE = 16
CMP_HIDDEN = 128
SEL_BLOCK = 64
N_SEL = 16
WINDOW = 512
FORCE_SCORE = 1e4
NEG_INF = -1e30
LOG2E = 1.4426950408889634
PEER_HEADS = 8
PEER_KEYS = 128
PEER_TOPK = 16

LANES = 128
VMEM_LIMIT = 56 * 1024 * 1024


def _dot_hi(a, b):
    return jnp.dot(a, b, precision=HIGHEST, preferred_element_type=F32)


def _dot_bf(a, b):
    return jnp.dot(a.astype(BF16), b.astype(BF16), preferred_element_type=F32)


def _cparams(*sem):
    return pltpu.CompilerParams(dimension_semantics=sem, vmem_limit_bytes=VMEM_LIMIT)


def _group_avg_matrix(n, group):
    i = np.arange(n)
    return jnp.asarray((i[:, None] // group == i[None, :] // group).astype(np.float32) / group)


def _ada_kernel(c_ref, w_ref, b_ref, o_ref):
    c = c_ref[...]
    cs = c * jax.nn.sigmoid(c)
    o_ref[0] = _dot_hi(cs, w_ref[0]) + b_ref[0]


def ada_mod(c, ada_w, ada_b):
    L, D, N = ada_w.shape
    B = c.shape[0]
    bp = 8
    cp = jnp.zeros((bp, D), F32).at[:B].set(c)
    tn = N // 4
    out = pl.pallas_call(
        _ada_kernel,
        grid=(L, N // tn),
        in_specs=[pl.BlockSpec((bp, D), lambda l, j: (0, 0)),
                  pl.BlockSpec((1, D, tn), lambda l, j: (l, 0, j)),
                  pl.BlockSpec((1, 1, tn), lambda l, j: (l, 0, j))],
        out_specs=pl.BlockSpec((1, bp, tn), lambda l, j: (l, 0, j)),
        out_shape=jax.ShapeDtypeStruct((L, bp, N), F32),
        compiler_params=_cparams("parallel", "parallel"),
        name="ada_mod",
    )(cp, ada_w, ada_b.reshape(L, 1, N))
    return out[:, :B]


def _modulated_norm(x, g, sc, sh):
    ms = jnp.mean(x * x, axis=-1, keepdims=True)
    return x * lax.rsqrt(ms + RMS_EPS) * g * (1.0 + sc) + sh


def _proj_in_kernel(x_ref, sh_ref, sc_ref, g_ref, *refs):
    n = len(refs) // 2
    h = _modulated_norm(x_ref[0], g_ref[...], sc_ref[0], sh_ref[0]).astype(BF16)
    for w_ref, o_ref in zip(refs[:n], refs[n:]):
        o_ref[0] = jnp.dot(h, w_ref[...], preferred_element_type=F32)


def proj_in(x, sh, sc, g, weights, tm=512):
    B, T, D = x.shape
    tm = min(tm, T)
    vec = pl.BlockSpec((1, 1, D), lambda b, i: (b, 0, 0))
    in_specs = [pl.BlockSpec((1, tm, D), lambda b, i: (b, i, 0)), vec, vec,
                pl.BlockSpec((1, D), lambda b, i: (0, 0))]
    in_specs += [pl.BlockSpec(w.shape, lambda b, i: (0, 0)) for w in weights]
    out_specs = [pl.BlockSpec((1, tm, w.shape[1]), lambda b, i: (b, i, 0)) for w in weights]
    out_shape = [jax.ShapeDtypeStruct((B, T, w.shape[1]), F32) for w in weights]
    return pl.pallas_call(
        _proj_in_kernel,
        grid=(B, T // tm),
        in_specs=in_specs, out_specs=out_specs, out_shape=out_shape,
        compiler_params=_cparams("parallel", "parallel"),
        name="proj_in",
    )(x, sh.reshape(B, 1, D), sc.reshape(B, 1, D), g.reshape(1, D), *weights)


CONV_HALO = 32


def _conv_kernel(p_ref, w_ref, b_ref, gg_ref, gb_ref, m_ref, o_ref, ext_ref):
    i = pl.program_id(1)
    tt, dc = o_ref.shape[1], o_ref.shape[2]

    @pl.when(i == 0)
    def _():
        ext_ref[0:CONV_HALO, :] = jnp.zeros((CONV_HALO, dc), F32)

    @pl.when(i > 0)
    def _():
        ext_ref[0:CONV_HALO, :] = ext_ref[tt:tt + CONV_HALO, :]

    p = p_ref[0]
    ext_ref[CONV_HALO:CONV_HALO + tt, :] = p[:, :dc] * jax.nn.sigmoid(p[:, dc:])
    off = CONV_HALO - (CONV_WIDTH - 1)
    acc = jnp.zeros((tt, dc), F32) + b_ref[...]
    for j in range(CONV_WIDTH):
        acc = acc + ext_ref[off + j:off + j + tt, :] * w_ref[j:j + 1, :]
    m = m_ref[...]
    mu = _dot_hi(acc, m)
    d = acc - mu
    var = _dot_hi(d * d, m)
    y = d * lax.rsqrt(var + CONV_EPS) * gg_ref[...] + gb_ref[...]
    o_ref[0] = y * jax.nn.sigmoid(y)


def conv_mixer(p, dw_w, dw_b, gn_g, gn_b, tt=512):
    B, T, two_dc = p.shape
    dc = two_dc // 2
    tt = min(tt, T)
    wpad = jnp.zeros((32, dc), F32).at[:CONV_WIDTH].set(dw_w.reshape(CONV_WIDTH, dc))
    row = pl.BlockSpec((1, dc), lambda b, i: (0, 0))
    return pl.pallas_call(
        _conv_kernel,
        grid=(B, T // tt),
        in_specs=[pl.BlockSpec((1, tt, two_dc), lambda b, i: (b, i, 0)),
                  pl.BlockSpec((32, dc), lambda b, i: (0, 0)), row, row, row,
                  pl.BlockSpec((dc, dc), lambda b, i: (0, 0))],
        out_specs=pl.BlockSpec((1, tt, dc), lambda b, i: (b, i, 0)),
        out_shape=jax.ShapeDtypeStruct((B, T, dc), F32),
        scratch_shapes=[pltpu.VMEM((tt + CONV_HALO, dc), F32)],
        compiler_params=_cparams("parallel", "arbitrary"),
        name="conv_mixer",
    )(p, wpad, dw_b.reshape(1, dc), gn_g.reshape(1, dc), gn_b.reshape(1, dc),
      _group_avg_matrix(dc, HEAD_DIM))


SCAN_SUB = 64
SCAN_CHUNK = 128


def _group_sum_matrix(n, group):
    i = np.arange(n)
    return jnp.asarray((i[:, None] // group == i[None, :] // group).astype(np.float32))


def _rwkv_pre_kernel(p_ref, halo_ref, mu_ref, w0_ref, wup_ref, a0_ref, aup_ref, gup_ref, kk_ref, ka_ref,
                     rk_ref, ones_ref, tri_ref, w_o, kk_o, b_o, k_o, v_o, r_o, bonus_o, g_o, ext_ref):
    i = pl.program_id(1)
    tt = p_ref.shape[1]
    dr = w_o.shape[2]
    p = p_ref[0]
    first = (i > 0).astype(F32)
    ext_ref[0:8, :] = halo_ref[0] * first
    ext_ref[8:8 + tt, :] = p
    prev = ext_ref[7:7 + tt, :]
    xs = p + (prev - p) * mu_ref[...]
    r = xs[:, 0:dr]
    k = xs[:, dr:2 * dr]
    v = xs[:, 2 * dr:3 * dr]
    o = 3 * dr
    xw = xs[:, o:o + LORA_W]
    xa = xs[:, o + LORA_W:o + LORA_W + LORA_A]
    xg = xs[:, o + LORA_W + LORA_A:o + LORA_W + LORA_A + LORA_G]
    z = w0_ref[...] + _dot_hi(jnp.tanh(xw), wup_ref[...])
    w_log = -jax.nn.softplus(-z) - 0.5
    log_decay = -jnp.exp(w_log)
    cum = _dot_hi(tri_ref[...], log_decay)
    p_incl = jnp.exp(cum)
    p_excl = jnp.exp(cum - log_decay)
    p_inv = jnp.exp(-cum)
    a = jax.nn.sigmoid(a0_ref[...] + _dot_hi(xa, aup_ref[...]))
    g = _dot_hi(jax.nn.sigmoid(xg), gup_ref[...])
    ones = ones_ref[...]
    kk = k * kk_ref[...]
    nrm = jnp.sqrt(_dot_hi(kk * kk, ones))
    kk = kk / jnp.maximum(nrm, 1e-12)
    k_eff = k * (1.0 + (a - 1.0) * ka_ref[...])
    bonus = _dot_hi(r * k_eff * rk_ref[...], ones) * v
    w_o[0] = p_incl
    kk_o[0] = kk * p_excl
    b_o[0] = kk * a * p_inv
    k_o[0] = k_eff * p_inv
    v_o[0] = v
    r_o[0] = r * p_incl
    bonus_o[0] = bonus
    g_o[0] = g


def rwkv_pre(p, mu, w0, w_up, a0, a_up, g_up, k_k, k_a, r_k, tt=512):
    B, T, n_in = p.shape
    dr = w0.shape[0]
    tt = min(tt, T)
    nb8 = tt // 8
    full = lambda a: pl.BlockSpec(a.shape, lambda b, i: (0,) * a.ndim)
    blk_id = np.arange(tt) // SCAN_SUB
    tri = jnp.asarray(((blk_id[:, None] == blk_id[None, :])
                       & (np.arange(tt)[:, None] >= np.arange(tt)[None, :])).astype(np.float32))
    args = [mu.reshape(1, n_in), w0.reshape(1, dr), w_up, a0.reshape(1, dr), a_up, g_up,
            k_k.reshape(1, dr), k_a.reshape(1, dr), r_k.reshape(1, dr), _group_sum_matrix(dr, HEAD_DIM), tri]
    out = jax.ShapeDtypeStruct((B, T, dr), F32)
    ospec = pl.BlockSpec((1, tt, dr), lambda b, i: (b, i, 0))
    return pl.pallas_call(
        _rwkv_pre_kernel,
        grid=(B, T // tt),
        in_specs=[pl.BlockSpec((1, tt, n_in), lambda b, i: (b, i, 0)),
                  pl.BlockSpec((1, 8, n_in), lambda b, i: (b, jnp.maximum(i * nb8 - 1, 0), 0))]
                 + [full(a) for a in args],
        out_specs=[ospec] * 8, out_shape=[out] * 8,
        scratch_shapes=[pltpu.VMEM((tt + 8, n_in), F32)],
        compiler_params=_cparams("parallel", "parallel"),
        name="rwkv_pre",
    )(p, p, *args)


def _scan_select_matrices():
    e = np.zeros((SCAN_SUB // 2, 4 * SCAN_SUB, 2 * LANES), np.float32)
    for t in range(SCAN_SUB):
        c = (t % 2) * LANES
        for part in range(2):
            e[t // 2, 2 * part * SCAN_SUB + t, c:c + HEAD_DIM] = 1.0
            e[t // 2, (2 * part + 1) * SCAN_SUB + t, c + HEAD_DIM:c + LANES] = 1.0
    return jnp.asarray(e, BF16)


def _rwkv_scan_kernel(pc_ref, kk_ref, b_ref, k_ref, r_ref, v_ref, bonus_ref, g_ref, lng_ref, lnb_ref,
                      e_ref, avg_ref, o_ref, st_ref, lhs_ref, y_ref):
    nb, tc, dr = v_ref.shape
    n_half = dr // LANES
    n_pair = nb * n_half
    n_sub = tc // SCAN_SUB
    quantities = (kk_ref, b_ref, k_ref, r_ref)

    @pl.when(pl.program_id(0) == 0)
    def _():
        st_ref[...] = jnp.zeros(st_ref.shape, F32)

    for qi, q_ref in enumerate(quantities):
        for p in range(n_pair):
            b, hb = divmod(p, n_half)
            xt = q_ref[b, :, hb * LANES:(hb + 1) * LANES].T
            for s in range(n_sub):
                cat = jnp.concatenate([xt[0:HEAD_DIM, s * SCAN_SUB:(s + 1) * SCAN_SUB],
                                       xt[HEAD_DIM:, s * SCAN_SUB:(s + 1) * SCAN_SUB]], axis=1)
                hi = cat.astype(BF16)
                lo = (cat - hi.astype(F32)).astype(BF16)
                row = (qi * n_pair + p) * HEAD_DIM
                lhs_ref[s, row:row + HEAD_DIM, :] = jnp.concatenate([hi, lo], axis=1)

    sub_iota = lax.broadcasted_iota(jnp.int32, (8, LANES), 0)
    ri = lax.broadcasted_iota(jnp.int32, (HEAD_DIM, LANES), 0)
    ci = lax.broadcasted_iota(jnp.int32, (HEAD_DIM, LANES), 1)
    diag2 = (ci % HEAD_DIM == ri).astype(F32)
    rj = lax.broadcasted_iota(jnp.int32, (LANES, LANES), 0)
    cj = lax.broadcasted_iota(jnp.int32, (LANES, LANES), 1)
    half_ones = (rj // HEAD_DIM == cj // HEAD_DIM).astype(F32)
    for s in range(n_sub):
        def steps8(t8, carry, s=s):
            base = pl.multiple_of(s * SCAN_SUB + t8 * 8, 8)
            y8 = [jnp.zeros((8, LANES), F32) for _ in range(n_pair)]
            for j in range(8):
                if j % 2 == 0:
                    z2 = jnp.dot(lhs_ref[s], e_ref[t8 * 4 + j // 2], preferred_element_type=F32)
                z = z2[:, (j % 2) * LANES:(j % 2 + 1) * LANES]
                for p in range(n_pair):
                    b, hb = divmod(p, n_half)
                    col = lambda qi: z[(qi * n_pair + p) * HEAD_DIM:(qi * n_pair + p + 1) * HEAD_DIM, :]
                    st = st_ref[p]
                    sa = jnp.sum(st * col(0), axis=0, keepdims=True)
                    v_row = v_ref[b, pl.ds(base, 8), hb * LANES:(hb + 1) * LANES][j:j + 1, :]
                    st = st - col(1) * sa + col(2) * v_row
                    y = jnp.sum(st * col(3), axis=0, keepdims=True)
                    y8[p] = jnp.where(sub_iota == j, y, y8[p])
                    st_ref[p] = st
            for p in range(n_pair):
                b, hb = divmod(p, n_half)
                y_ref[b, pl.ds(base, 8), hb * LANES:(hb + 1) * LANES] = y8[p]
            return carry
        lax.fori_loop(0, SCAN_SUB // 8, steps8, 0)
        last = s * SCAN_SUB + SCAN_SUB - 1
        for p in range(n_pair):
            b, hb = divmod(p, n_half)
            p_row = pc_ref[b, last - 7:last + 1, hb * LANES:(hb + 1) * LANES][7:8, :]
            st_ref[p] = st_ref[p] * _dot_hi(diag2 * p_row, half_ones)

    avg = avg_ref[...]
    for b in range(nb):
        y = y_ref[b]
        mu = _dot_hi(y, avg)
        d = y - mu
        var = _dot_hi(d * d, avg)
        yn = d * lax.rsqrt(var + RWKV_GN_EPS) * lng_ref[...] + lnb_ref[...]
        o_ref[b] = (yn + bonus_ref[b]) * g_ref[b]


def rwkv_scan(w, kk, bq, k, r, v, bonus, g, ln_g, ln_b):
    B, T, dr = v.shape
    tc = min(SCAN_CHUNK, T)
    n_pair = B * dr // LANES
    blk = pl.BlockSpec((B, tc, dr), lambda i: (0, i, 0))
    row = pl.BlockSpec((1, dr), lambda i: (0, 0))
    e = _scan_select_matrices()
    return pl.pallas_call(
        _rwkv_scan_kernel,
        grid=(T // tc,),
        in_specs=[blk] * 8 + [row, row, pl.BlockSpec(e.shape, lambda i: (0, 0, 0)),
                              pl.BlockSpec((dr, dr), lambda i: (0, 0))],
        out_specs=blk,
        out_shape=jax.ShapeDtypeStruct((B, T, dr), F32),
        scratch_shapes=[pltpu.VMEM((n_pair, HEAD_DIM, LANES), F32),
                        pltpu.VMEM((tc // SCAN_SUB, 4 * n_pair * HEAD_DIM, 4 * SCAN_SUB), BF16),
                        pltpu.VMEM((B, tc, dr), F32)],
        compiler_params=_cparams("arbitrary"),
        name="rwkv_scan",
    )(w, kk, bq, k, r, v, bonus, g, ln_g.reshape(1, dr), ln_b.reshape(1, dr), e, _group_avg_matrix(dr, HEAD_DIM))


def rwkv_mixer(p, mu, w0, w_up, a0, a_up, g_up, k_k, k_a, r_k, ln_g, ln_b):
    w, kk, bq, k, v, r, bonus, g = rwkv_pre(p, mu, w0, w_up, a0, a_up, g_up, k_k, k_a, r_k.reshape(-1))
    return rwkv_scan(w, kk, bq, k, r, v, bonus, g, ln_g, ln_b)


def _dot_nt(a, b, **kw):
    return lax.dot_general(a, b, (((1,), (1,)), ((), ())), preferred_element_type=F32, **kw)


def _compress_kernel(kz_ref, vz_ref, kpos_ref, kw1_ref, kw2_ref, vpos_ref, vw1_ref, vw2_ref,
                     kc_ref, vc_ref, shift_ref):
    n = kz_ref.shape[2]
    half = kz_ref.shape[3]

    def one(z_ref, pos_ref, w1_ref, w2_ref, o_ref):
        z = z_ref[0, 0]
        top = _dot_bf(z + pos_ref[0:1, :], w1_ref[0:half, :])
        bot = _dot_bf(z + pos_ref[1:2, :], w1_ref[half:2 * half, :])
        shift_ref[0:n, :] = bot
        shift_ref[n:n + 8, :] = jnp.zeros((8, bot.shape[1]), F32)
        pre = top + shift_ref[1:n + 1, :]
        hid = pre * jax.nn.sigmoid(pre)
        o_ref[0, 0] = _dot_bf(hid, w2_ref[...])

    one(kz_ref, kpos_ref, kw1_ref, kw2_ref, kc_ref)
    one(vz_ref, vpos_ref, vw1_ref, vw2_ref, vc_ref)


def nsa_compress(k_cmp, v_cmp, ck_pos, ck_w1, ck_w2, cv_pos, cv_w1, cv_w2):
    B, G, T, dk = k_cmp.shape
    n = T // CMP_STRIDE
    half = CMP_STRIDE * dk
    zspec = pl.BlockSpec((1, 1, n, half), lambda b, g: (b, g, 0, 0))
    full = lambda a: pl.BlockSpec(a.shape, lambda b, g: (0,) * a.ndim)
    ospec = pl.BlockSpec((1, 1, n, dk), lambda b, g: (b, g, 0, 0))
    args = [ck_pos.reshape(2, half), ck_w1.astype(BF16), ck_w2.astype(BF16),
            cv_pos.reshape(2, half), cv_w1.astype(BF16), cv_w2.astype(BF16)]
    return pl.pallas_call(
        _compress_kernel,
        grid=(B, G),
        in_specs=[zspec, zspec] + [full(a) for a in args],
        out_specs=[ospec, ospec],
        out_shape=[jax.ShapeDtypeStruct((B, G, n, dk), F32)] * 2,
        scratch_shapes=[pltpu.VMEM((n + 8, CMP_HIDDEN), F32)],
        compiler_params=_cparams("parallel", "parallel"),
        name="nsa_compress",
    )(k_cmp.reshape(B, G, n, half), v_cmp.reshape(B, G, n, half), *args)


NSA_TQ = 128
NSA_TK = 1024


def _split_bf16(a):
    hi = a.astype(BF16)
    return hi, (a - hi.astype(F32)).astype(BF16)


def _dot_3pass(a, b):
    ah, al = _split_bf16(a)
    bh, bl = _split_bf16(b)
    d = lambda x, y: jnp.dot(x, y, preferred_element_type=F32)
    return d(ah, bh) + d(ah, bl) + d(al, bh)


def _nsa_kernel(q_ref, gate_ref, kc_ref, vct_ref, ks_ref, vst_ref, kw_ref, vwt_ref, ovt_ref,
                o_ref, m_ref, acc_ref, ow_ref, sel_ref, s0_ref, s1_ref, p_ref):
    qi = pl.program_id(2)
    dk = q_ref.shape[3]
    tq = gate_ref.shape[3]
    R = q_ref.shape[4] // tq
    n_c = kc_ref.shape[2]
    n_blk = ovt_ref.shape[0]
    T = ks_ref.shape[2]
    tk = s0_ref.shape[0]
    t0 = qi * tq
    qt = q_ref[0, 0, 0]
    qt_bf = qt.astype(BF16)
    t_row = t0 + lax.broadcasted_iota(jnp.int32, (1, tq), 1)
    lanes = lambda r: slice(r * tq, (r + 1) * tq)

    s = _dot_3pass(kc_ref[0, 0], qt)
    c_end = lax.broadcasted_iota(jnp.int32, (n_c, tq), 0) * CMP_STRIDE + (CMP_BLOCK - 1)
    valid_c = c_end <= t_row
    any_c = (t_row >= CMP_BLOCK - 1).astype(F32)
    p_sum = jnp.zeros((n_c, tq), F32)
    ps = []
    for r in range(R):
        s_r = jnp.where(valid_c, s[:, lanes(r)], NEG_INF)
        e = jnp.exp2(s_r - jnp.max(s_r, axis=0, keepdims=True))
        p_r = e * (any_c / jnp.sum(e, axis=0, keepdims=True))
        p_sum = p_sum + p_r
        ps.append(p_r.astype(BF16))
    o_c = jnp.dot(vct_ref[0, 0].astype(BF16), jnp.concatenate(ps, axis=1), preferred_element_type=F32)

    imp = _dot_hi(ovt_ref[...], p_sum)
    blk = lax.broadcasted_iota(jnp.int32, (n_blk, tq), 0).astype(F32)
    cur = (t_row // SEL_BLOCK).astype(F32)
    forced = (blk == 0.0) | (blk == cur) | (blk == cur - 1.0)
    x = jnp.where(forced, FORCE_SCORE, jnp.where(blk <= cur, imp, -1.0))
    x = jnp.where(blk < float(T // SEL_BLOCK), x, -3e38)
    sel = jnp.zeros((n_blk, tq), F32)
    for _ in range(N_SEL):
        mx = jnp.max(x, axis=0, keepdims=True)
        idx = jnp.min(jnp.where(x == mx, blk, float(n_blk)), axis=0, keepdims=True)
        hit = blk == idx
        sel = jnp.where(hit, 1.0, sel)
        x = jnp.where(hit, -jnp.inf, x)

    span = WINDOW + tq
    w0 = pl.multiple_of(jnp.maximum(t0 - WINDOW, 0), tq)
    s = jnp.dot(kw_ref[0, 0, pl.ds(w0, span), :], qt_bf, preferred_element_type=F32)
    dist = t_row - (w0 + lax.broadcasted_iota(jnp.int32, (span, tq), 0))
    bias_w = jnp.where((dist >= 0) & (dist < WINDOW), 0.0, NEG_INF)
    ps = []
    for r in range(R):
        s_r = s[:, lanes(r)] + bias_w
        ps.append(jnp.exp2(s_r - jnp.max(s_r, axis=0, keepdims=True)).astype(BF16))
    ow_ref[...] = jnp.dot(vwt_ref[0, 0, :, pl.ds(w0, span)], jnp.concatenate(ps, axis=1),
                          preferred_element_type=F32)

    bpt = tk // SEL_BLOCK
    n_kt = T // tk
    n_full = (t0 + tq - 1) // tk
    sel_ref[...] = jnp.where(sel > 0.5, 0.0, NEG_INF)
    m_ref[...] = jnp.full(m_ref.shape, NEG_INF, F32)
    acc_ref[...] = jnp.zeros(acc_ref.shape, F32)
    q_pad = jnp.zeros((ks_ref.shape[3] - dk - bpt, R * tq), BF16)

    def scores(kt):
        kt = jnp.minimum(kt, n_kt - 1)
        rows = sel_ref[pl.ds(pl.multiple_of(kt * bpt, bpt), bpt), :].astype(BF16)
        q_aug = jnp.concatenate([qt_bf, jnp.concatenate([rows] * R, axis=1), q_pad], axis=0)
        off = pl.multiple_of(kt * tk, tk)
        return jnp.dot(ks_ref[0, 0, pl.ds(off, tk), :], q_aug, preferred_element_type=F32)

    def softmax_tile(s_ref, kt, causal):
        if causal:
            key_pos = kt * tk + lax.broadcasted_iota(jnp.int32, (tk, tq), 0)
            bias = jnp.where(key_pos <= t_row, 0.0, NEG_INF)
        for r in range(R):
            m_old = m_ref[:, lanes(r)]
            if causal:
                m_new = jnp.maximum(m_old, jnp.max(s_ref[:, lanes(r)] + bias, axis=0, keepdims=True))
                p_r = jnp.exp2((s_ref[:, lanes(r)] - m_new) + bias)
            else:
                m_new = jnp.maximum(m_old, jnp.max(s_ref[:, lanes(r)], axis=0, keepdims=True))
                p_r = jnp.exp2(s_ref[:, lanes(r)] - m_new)
            m_ref[:, lanes(r)] = m_new
            p_ref[:, lanes(r)] = p_r.astype(BF16)
            acc_ref[:, lanes(r)] = acc_ref[:, lanes(r)] * jnp.exp2(m_old - m_new)
        off = pl.multiple_of(kt * tk, tk)
        acc_ref[...] += jnp.dot(vst_ref[0, 0, :, pl.ds(off, tk)], p_ref[...], preferred_element_type=F32)

    s0_ref[...] = scores(0)

    def tile_pair(i, carry):
        s1_ref[...] = scores(2 * i + 1)
        softmax_tile(s0_ref, 2 * i, False)
        s0_ref[...] = scores(2 * i + 2)
        softmax_tile(s1_ref, 2 * i + 1, False)
        return carry

    lax.fori_loop(0, n_full // 2, tile_pair, 0)
    odd = n_full % 2 == 1

    @pl.when(odd)
    def _():
        s1_ref[...] = scores(n_full)
        softmax_tile(s0_ref, n_full - 1, False)
        softmax_tile(s1_ref, n_full, True)

    @pl.when(jnp.logical_not(odd))
    def _():
        softmax_tile(s0_ref, n_full, True)

    acc = acc_ref[...]
    o_s = acc[0:dk] * (1.0 / acc[dk:dk + 1])
    o_w = ow_ref[...]
    o_w = o_w[0:dk] * (1.0 / o_w[dk:dk + 1])

    gate = jax.nn.sigmoid(gate_ref[0, 0])
    outs = []
    for r in range(R):
        outs.append(gate[r:r + 1] * o_c[:, lanes(r)] + gate[R + r:R + r + 1] * o_s[:, lanes(r)]
                    + gate[2 * R + r:2 * R + r + 1] * o_w[:, lanes(r)])
    o_ref[0] = jnp.concatenate(outs, axis=0).T


def _overlap_matrix_t(n_c, n_blk):
    c0 = np.arange(n_c)[None, :] * CMP_STRIDE
    s0 = np.arange(n_blk)[:, None] * SEL_BLOCK
    return jnp.asarray(((c0 < s0 + SEL_BLOCK) & (c0 + CMP_BLOCK > s0)).astype(np.float32))


def nsa_attention(qt, gate_t, kc, vct, k_sel, vt_sel, k_win, vt_win):
    B, G, nq, dk, L = qt.shape
    T = k_win.shape[2]
    tq = T // nq
    R = L // tq
    n_c = kc.shape[2]
    tk = min(NSA_TK, T)
    n_blk = max(T // SEL_BLOCK, LANES)
    ovt = _overlap_matrix_t(n_c, n_blk)
    res = lambda a: pl.BlockSpec((1, 1) + a.shape[2:], lambda b, g, i: (b, g, 0, 0))
    return pl.pallas_call(
        _nsa_kernel,
        grid=(B, G, nq),
        in_specs=[pl.BlockSpec((1, 1, 1, dk, L), lambda b, g, i: (b, g, i, 0, 0)),
                  pl.BlockSpec((1, 1, gate_t.shape[2], tq), lambda b, g, i: (b, g, 0, i)),
                  res(kc), res(vct), res(k_sel), res(vt_sel), res(k_win), res(vt_win),
                  pl.BlockSpec(ovt.shape, lambda b, g, i: (0, 0))],
        out_specs=pl.BlockSpec((1, tq, R * dk), lambda b, g, i: (b, i, g)),
        out_shape=jax.ShapeDtypeStruct((B, T, G * R * dk), F32),
        scratch_shapes=[pltpu.VMEM((1, L), F32), pltpu.VMEM((dk + 16, L), F32),
                        pltpu.VMEM((dk + 16, L), F32), pltpu.VMEM((n_blk, tq), F32),
                        pltpu.VMEM((tk, L), F32), pltpu.VMEM((tk, L), F32), pltpu.VMEM((tk, L), BF16)],
        compiler_params=_cparams("parallel", "parallel", "arbitrary"),
        name="nsa_attention",
    )(qt, gate_t, kc, vct, k_sel, vt_sel, k_win, vt_win, ovt)


def nsa_mixer(p_q, p_kv, p_gate, ck_pos, ck_w1, ck_w2, cv_pos, cv_w1, cv_w2):
    B, T, _ = p_q.shape
    G, R, dk = NSA_KV_HEADS, NSA_GROUP, HEAD_DIM
    tq = min(NSA_TQ, T)
    kv6 = p_kv.reshape(B, T, 6, G, dk)
    rows = lambda i: kv6[:, :, i].transpose(0, 2, 1, 3)
    cols = lambda i: kv6[:, :, i].transpose(0, 2, 3, 1)
    kc, vc = nsa_compress(rows(0), rows(1), ck_pos, ck_w1, ck_w2, cv_pos, cv_w1, cv_w2)
    qt = (p_q * (dk ** -0.5 * LOG2E)).reshape(B, T // tq, tq, G, R, dk).transpose(0, 3, 1, 5, 4, 2)
    qt = qt.reshape(B, G, T // tq, dk, R * tq)
    gate_t = p_gate[..., :3 * G * R].reshape(B, T, 3, G, R).transpose(0, 3, 2, 4, 1).reshape(B, G, 3 * R, T)
    gate_t = jnp.pad(gate_t, ((0, 0), (0, 0), (0, 16 - 3 * R), (0, 0)))
    tk = min(NSA_TK, T)
    blk_onehot = (jnp.arange(T)[:, None] // SEL_BLOCK % (tk // SEL_BLOCK) == jnp.arange(dk)[None, :]).astype(BF16)
    k_sel = jnp.concatenate([rows(2).astype(BF16), jnp.broadcast_to(blk_onehot, (B, G, T, dk))], axis=-1)
    ones_rows = jnp.zeros((B, G, 16, T), BF16).at[:, :, 0].set(1.0)
    with_ones = lambda vt: jnp.concatenate([vt.astype(BF16), ones_rows], axis=2)
    return nsa_attention(qt, gate_t, kc, vc.transpose(0, 1, 3, 2), k_sel, with_ones(cols(3)),
                         rows(4).astype(BF16), with_ones(cols(5)))


def _proj_out_kernel(x_ref, g1_ref, sh_ref, sc_ref, ng_ref, *refs):
    n = (len(refs) - 2) // 2
    x1_ref, h2_ref = refs[2 * n:]
    acc = None
    for m_ref, w_ref in zip(refs[:n], refs[n:2 * n]):
        d = jnp.dot(m_ref[0].astype(BF16), w_ref[...], preferred_element_type=F32)
        acc = d if acc is None else acc + d
    x1 = x_ref[0] + g1_ref[0] * acc
    x1_ref[0] = x1
    h2_ref[0] = _modulated_norm(x1, ng_ref[...], sc_ref[0], sh_ref[0]).astype(BF16)


def proj_out(x, g1, sh2, sc2, norm_g, mixes, weights, tm=512):
    B, T, D = x.shape
    tm = min(tm, T)
    vec = pl.BlockSpec((1, 1, D), lambda b, i: (b, 0, 0))
    tile = lambda n: pl.BlockSpec((1, tm, n), lambda b, i: (b, i, 0))
    return pl.pallas_call(
        _proj_out_kernel,
        grid=(B, T // tm),
        in_specs=[tile(D), vec, vec, vec, pl.BlockSpec((1, D), lambda b, i: (0, 0))]
                 + [tile(m.shape[2]) for m in mixes]
                 + [pl.BlockSpec(w.shape, lambda b, i: (0, 0)) for w in weights],
        out_specs=[tile(D), tile(D)],
        out_shape=[jax.ShapeDtypeStruct((B, T, D), F32), jax.ShapeDtypeStruct((B, T, D), BF16)],
        compiler_params=_cparams("parallel", "parallel"),
        name="proj_out",
    )(x, g1.reshape(B, 1, D), sh2.reshape(B, 1, D), sc2.reshape(B, 1, D), norm_g.reshape(1, D), *mixes, *weights)


_CAND_BLOCKS = [(0, 16), (1, 8), (None, 8), (2, 5), (3, 4), (4, 3), (5, 2), (6, 2), (7, 2)]
_CAND_PAD = 6


def _cand_constants(tm):
    flat, neg = [], []
    for i, rows in _CAND_BLOCKS:
        for r in range(rows):
            flat.append((8 + r) * PEER_TOPK if i is None else i * PEER_TOPK + r)
            neg.append(0.0)
    flat += [1e9] * _CAND_PAD
    neg += [-np.inf] * _CAND_PAD
    flat = np.tile(np.asarray(flat, np.float32)[:, None], (1, tm))
    neg = np.tile(np.asarray(neg, np.float32)[:, None], (1, tm))
    return jnp.asarray(flat), jnp.asarray(neg)


def _topk_rows(x, row_id, k, n_rows):
    vals, idxs = [], []
    for _ in range(k):
        mx = jnp.max(x, axis=0, keepdims=True)
        idx = jnp.min(jnp.where(x == mx, row_id, float(n_rows)), axis=0, keepdims=True)
        x = jnp.where(row_id == idx, -jnp.inf, x)
        vals.append(mx)
        idxs.append(idx)
    return jnp.concatenate(vals, axis=0), jnp.concatenate(idxs, axis=0)


def _peer_select_kernel(h_ref, wq_ref, k1_ref, k2_ref, flat_ref, neg_ref, a_ref, b_ref, g_ref):
    tm = h_ref.shape[0]
    nk = k1_ref.shape[0]
    half = k1_ref.shape[1]
    q = jnp.dot(h_ref[...], wq_ref[...], preferred_element_type=F32)
    key_id = lax.broadcasted_iota(jnp.int32, (nk, tm), 0).astype(F32)
    flat = flat_ref[...]
    neg = neg_ref[...]
    k1 = k1_ref[...]
    k2 = k2_ref[...]
    a_all, b_all, g_all = [], [], []
    for h in range(PEER_HEADS):
        q1 = q[:, (2 * h) * half:(2 * h + 1) * half]
        q2 = q[:, (2 * h + 1) * half:(2 * h + 2) * half]
        v1, i1 = _topk_rows(_dot_nt(k1, q1, precision=HIGHEST), key_id, PEER_TOPK, nk)
        v2, i2 = _topk_rows(_dot_nt(k2, q2, precision=HIGHEST), key_id, PEER_TOPK, nk)
        vals, ai, bj = [], [], []
        for i, rows in _CAND_BLOCKS:
            if i is None:
                vals.append(v1[8:16] + v2[0:1])
                ai.append(i1[8:16])
                bj.append(jnp.broadcast_to(i2[0:1], (8, tm)))
            else:
                vals.append(v1[i:i + 1] + v2[0:rows])
                ai.append(jnp.broadcast_to(i1[i:i + 1], (rows, tm)))
                bj.append(i2[0:rows])
        pad = [jnp.zeros((_CAND_PAD, tm), F32)]
        cand = jnp.concatenate(vals + pad, axis=0) + neg
        ai = jnp.concatenate(ai + pad, axis=0)
        bj = jnp.concatenate(bj + pad, axis=0)
        sc, sa, sb = [], [], []
        for _ in range(PEER_TOPK):
            mx = jnp.max(cand, axis=0, keepdims=True)
            fid = jnp.min(jnp.where(cand == mx, flat, 1e9), axis=0, keepdims=True)
            hit = flat == fid
            sc.append(mx)
            sa.append(jnp.sum(jnp.where(hit, ai, 0.0), axis=0, keepdims=True))
            sb.append(jnp.sum(jnp.where(hit, bj, 0.0), axis=0, keepdims=True))
            cand = jnp.where(hit, -jnp.inf, cand)
        sc = jnp.concatenate(sc, axis=0)
        e = jnp.exp(sc - sc[0:1])
        g_all.append(e / jnp.sum(e, axis=0, keepdims=True))
        a_all.append(jnp.concatenate(sa, axis=0))
        b_all.append(jnp.concatenate(sb, axis=0))
    a_ref[...] = jnp.concatenate(a_all, axis=0).T
    b_ref[...] = jnp.concatenate(b_all, axis=0).T
    g_ref[...] = jnp.concatenate(g_all, axis=0).T


def peer_select(h2, wq, k1, k2, tm=256):
    N, D = h2.shape
    tm = min(tm, N)
    flat, neg = _cand_constants(tm)
    full = lambda a: pl.BlockSpec(a.shape, lambda i: (0,) * a.ndim)
    out = jax.ShapeDtypeStruct((N, PEER_HEADS * PEER_TOPK), F32)
    ospec = pl.BlockSpec((tm, PEER_HEADS * PEER_TOPK), lambda i: (i, 0))
    return pl.pallas_call(
        _peer_select_kernel,
        grid=(N // tm,),
        in_specs=[pl.BlockSpec((tm, D), lambda i: (i, 0)), full(wq), full(k1), full(k2), full(flat), full(neg)],
        out_specs=[ospec] * 3, out_shape=[out] * 3,
        compiler_params=_cparams("parallel"),
        name="peer_select",
    )(h2, wq, k1, k2, flat, neg)


PEER_BUILD_GROUP = 16
PEER_BUILD_UNROLL = 2


def _gelu(x):
    return 0.5 * x * (1.0 + lax.erf(x * 0.7071067811865476))


def _peer_expert_kernel(*refs, final, n_e):
    if final:
        h_ref, a_ref, b_ref, g_ref, u_ref, v_ref, x_ref, g2_ref, fg_ref, o_ref, w3_ref, acc_ref, ha_ref, hb_ref = refs
    else:
        h_ref, a_ref, b_ref, g_ref, u_ref, v_ref, x_ref, g2_ref, o_ref, w3_ref, acc_ref, ha_ref, hb_ref = refs
    j = pl.program_id(1)
    tm = h_ref.shape[0]
    te = u_ref.shape[0]
    nk = PEER_KEYS
    n_a = te // nk

    def hidden():
        return _dot_nt(h_ref[...], u_ref[...])

    def finish(hid_ref, tile):
        acts = []
        for al in range(n_a):
            w_a = w3_ref[tile * n_a + al].astype(F32)
            acts.append((_gelu(hid_ref[:, al * nk:(al + 1) * nk]) * w_a).astype(BF16))
        acc_ref[...] += jnp.dot(jnp.concatenate(acts, axis=1), v_ref[...], preferred_element_type=F32)

    @pl.when(j == 0)
    def _():
        acc_ref[...] = jnp.zeros(acc_ref.shape, F32)
        ha_ref[...] = hidden()
        key_id = lax.broadcasted_iota(jnp.int32, (nk, a_ref.shape[1]), 0).astype(F32)

        def tokens(tb, carry):
            for grp in range(PEER_BUILD_UNROLL):
                base = pl.multiple_of((tb * PEER_BUILD_UNROLL + grp) * PEER_BUILD_GROUP, PEER_BUILD_GROUP)
                a8 = a_ref[pl.ds(base, PEER_BUILD_GROUP), :]
                b8 = b_ref[pl.ds(base, PEER_BUILD_GROUP), :]
                g8 = g_ref[pl.ds(base, PEER_BUILD_GROUP), :]
                ws = []
                for r in range(PEER_BUILD_GROUP):
                    g_row = g8[r:r + 1]
                    g_hi = g_row.astype(BF16).astype(F32)
                    g_lo = g_row - g_hi
                    eq_a = key_id == a8[r:r + 1]
                    x = jnp.concatenate([jnp.where(eq_a, g_hi, 0.0).astype(BF16),
                                         jnp.where(eq_a, g_lo, 0.0).astype(BF16)], axis=1)
                    y1 = jnp.where(key_id == b8[r:r + 1], 1.0, 0.0).astype(BF16)
                    y = jnp.concatenate([y1, y1], axis=1)
                    ws.append(_dot_nt(x, y))
                w3_ref[:, pl.ds(base, PEER_BUILD_GROUP), :] = jnp.swapaxes(
                    jnp.stack(ws, axis=0).astype(BF16), 0, 1)
            return carry
        lax.fori_loop(0, tm // (PEER_BUILD_GROUP * PEER_BUILD_UNROLL), tokens, 0)

    for parity, (h_write, h_read) in enumerate(((ha_ref, hb_ref), (hb_ref, ha_ref))):
        @pl.when((j > 0) & (j < n_e) & (j % 2 == parity))
        def _(h_write=h_write, h_read=h_read):
            h_write[...] = hidden()
            finish(h_read, j - 1)

    @pl.when(j == n_e)
    def _():
        finish(hb_ref if (n_e - 1) % 2 else ha_ref, n_e - 1)
        y = x_ref[...] + g2_ref[0] * acc_ref[...]
        if final:
            ms = jnp.mean(y * y, axis=-1, keepdims=True)
            y = y * lax.rsqrt(ms + RMS_EPS) * fg_ref[...]
        o_ref[...] = y


def peer_experts(h2, a_idx, b_idx, gw, u, v, x1, g2, tokens_per_batch, final_g=None, tm=512, te=1024):
    N, D = x1.shape
    E = u.shape[0]
    tm = min(tm, N)
    nb = g2.shape[0]
    n_e = E // te
    tok = lambda n: pl.BlockSpec((tm, n), lambda i, j: (i, 0))
    in_specs = [tok(D), tok(a_idx.shape[1]), tok(a_idx.shape[1]), tok(a_idx.shape[1]),
                pl.BlockSpec((te, D), lambda i, j: (jnp.minimum(j, n_e - 1), 0)),
                pl.BlockSpec((te, D), lambda i, j: (jnp.maximum(j - 1, 0), 0)),
                tok(D), pl.BlockSpec((1, 1, D), lambda i, j: (i * tm // tokens_per_batch, 0, 0))]
    args = [h2, a_idx, b_idx, gw, u, v, x1, g2.reshape(nb, 1, D)]
    if final_g is not None:
        in_specs.append(pl.BlockSpec((1, D), lambda i, j: (0, 0)))
        args.append(final_g.reshape(1, D))
    return pl.pallas_call(
        functools.partial(_peer_expert_kernel, final=final_g is not None, n_e=n_e),
        grid=(N // tm, n_e + 1),
        in_specs=in_specs,
        out_specs=tok(D),
        out_shape=jax.ShapeDtypeStruct((N, D), F32),
        scratch_shapes=[pltpu.VMEM((PEER_KEYS, tm, PEER_KEYS), BF16), pltpu.VMEM((tm, D), F32),
                        pltpu.VMEM((tm, te), F32), pltpu.VMEM((tm, te), F32)],
        compiler_params=_cparams("parallel", "arbitrary"),
        name="peer_experts",
    )(*args)


def peer_layer(x1, h2, wq, k1, k2, u, v, g2, final_g=None):
    B, T, D = x1.shape
    h2f = h2.reshape(B * T, D)
    a_idx, b_idx, gw = peer_select(h2f, wq, k1, k2)
    out = peer_experts(h2f, a_idx, b_idx, gw, u, v, x1.reshape(B * T, D), g2, T, final_g)
    return out.reshape(B, T, D)


def kernel(x, c, ada_w, ada_b, norm1_g, norm2_g, w_in, conv_dw_w, conv_dw_b, conv_gn_g, conv_gn_b,
           rwkv_mu, rwkv_w0, rwkv_w_up, rwkv_a0, rwkv_a_up, rwkv_g_up, rwkv_k_k, rwkv_k_a, rwkv_r_k,
           rwkv_ln_g, rwkv_ln_b, nsa_ck_pos, nsa_ck_w1, nsa_ck_w2, nsa_cv_pos, nsa_cv_w1, nsa_cv_w2,
           w_out, peer_wq, peer_k1, peer_k2, peer_u, peer_v, final_g):
    depth, D = norm1_g.shape
    d_conv = conv_dw_b.shape[1]
    d_rwkv = rwkv_w0.shape[1]
    n_conv = 2 * d_conv
    n_rwkv = rwkv_mu.shape[1]
    d_nsa = D - d_conv - d_rwkv
    n_kv = 6 * NSA_KV_HEADS * HEAD_DIM
    n_gate = 3 * NSA_KV_HEADS * NSA_GROUP
    cuts = np.cumsum([0, n_conv, n_rwkv, d_nsa, n_kv, n_gate])
    mod = ada_mod(c, ada_w, ada_b)
    for l in range(depth):
        sh1, sc1, g1, sh2, sc2, g2 = [mod[l, :, i * D:(i + 1) * D] for i in range(6)]
        w_l = w_in[l].astype(BF16)
        pieces = [w_l[:, cuts[i]:cuts[i + 1]] for i in range(5)]
        pieces[4] = jnp.pad(pieces[4], ((0, 0), (0, LANES - n_gate)))
        p_conv, p_rwkv, p_q, p_kv, p_gate = proj_in(x, sh1, sc1, norm1_g[l], pieces)
        o_conv = conv_mixer(p_conv, conv_dw_w[l], conv_dw_b[l], conv_gn_g[l], conv_gn_b[l])
        o_rwkv = rwkv_mixer(p_rwkv, rwkv_mu[l], rwkv_w0[l], rwkv_w_up[l], rwkv_a0[l], rwkv_a_up[l],
                            rwkv_g_up[l], rwkv_k_k[l], rwkv_k_a[l], rwkv_r_k[l], rwkv_ln_g[l], rwkv_ln_b[l])
        o_nsa = nsa_mixer(p_q, p_kv, p_gate, nsa_ck_pos[l], nsa_ck_w1[l], nsa_ck_w2[l],
                          nsa_cv_pos[l], nsa_cv_w1[l], nsa_cv_w2[l])
        wo = w_out[l].astype(BF16)
        wo_pieces = [wo[:d_conv], wo[d_conv:d_conv + d_rwkv], wo[d_conv + d_rwkv:]]
        x1, h2 = proj_out(x, g1, sh2, sc2, norm2_g[l], [o_conv, o_rwkv, o_nsa], wo_pieces)
        x = peer_layer(x1, h2, peer_wq[l].astype(BF16), peer_k1[l], peer_k2[l],
                       peer_u[l].astype(BF16), peer_v[l].astype(BF16), g2,
                       final_g if l == depth - 1 else None)
    return x
```

```python
import functools

import jax
import jax.numpy as jnp
import numpy as np
from jax import lax
from jax.experimental import pallas as pl
from jax.experimental.pallas import tpu as pltpu

F32 = jnp.float32
BF16 = jnp.bfloat16
HIGHEST = lax.Precision.HIGHEST

HEAD_DIM = 64
CONV_WIDTH = 31
CONV_EPS = 1e-5
RWKV_GN_EPS = 64e-5
RMS_EPS = 1e-6
LORA_W = 64
LORA_A = 64
LORA_G = 128
NSA_KV_HEADS = 2
NSA_GROUP = 4
CMP_BLOCK = 32
CMP_STRIDE = 16
CMP_HIDDEN = 128
SEL_BLOCK = 64
N_SEL = 16
WINDOW = 512
FORCE_SCORE = 1e4
NEG_INF = -1e30
LOG2E = 1.4426950408889634
PEER_HEADS = 8
PEER_KEYS = 128
PEER_TOPK = 16

LANES = 128
VMEM_LIMIT = 56 * 1024 * 1024


def _dot_hi(a, b):
    return jnp.dot(a, b, precision=HIGHEST, preferred_element_type=F32)


def _dot_bf(a, b):
    return jnp.dot(a.astype(BF16), b.astype(BF16), preferred_element_type=F32)


def _cparams(*sem):
    return pltpu.CompilerParams(dimension_semantics=sem, vmem_limit_bytes=VMEM_LIMIT)


def _group_avg_matrix(n, group):
    i = np.arange(n)
    return jnp.asarray((i[:, None] // group == i[None, :] // group).astype(np.float32) / group)


def _ada_kernel(c_ref, w_ref, b_ref, o_ref):
    c = c_ref[...]
    cs = c * jax.nn.sigmoid(c)
    o_ref[0] = _dot_hi(cs, w_ref[0]) + b_ref[0]


def ada_mod(c, ada_w, ada_b):
    L, D, N = ada_w.shape
    B = c.shape[0]
    bp = 8
    cp = jnp.zeros((bp, D), F32).at[:B].set(c)
    tn = N // 4
    out = pl.pallas_call(
        _ada_kernel,
        grid=(L, N // tn),
        in_specs=[pl.BlockSpec((bp, D), lambda l, j: (0, 0)),
                  pl.BlockSpec((1, D, tn), lambda l, j: (l, 0, j)),
                  pl.BlockSpec((1, 1, tn), lambda l, j: (l, 0, j))],
        out_specs=pl.BlockSpec((1, bp, tn), lambda l, j: (l, 0, j)),
        out_shape=jax.ShapeDtypeStruct((L, bp, N), F32),
        compiler_params=_cparams("parallel", "parallel"),
        name="ada_mod",
    )(cp, ada_w, ada_b.reshape(L, 1, N))
    return out[:, :B]


def _modulated_norm(x, g, sc, sh):
    ms = jnp.mean(x * x, axis=-1, keepdims=True)
    return x * lax.rsqrt(ms + RMS_EPS) * g * (1.0 + sc) + sh


def _proj_in_kernel(x_ref, sh_ref, sc_ref, g_ref, *refs):
    n = len(refs) // 2
    h = _modulated_norm(x_ref[0], g_ref[...], sc_ref[0], sh_ref[0]).astype(BF16)
    for w_ref, o_ref in zip(refs[:n], refs[n:]):
        o_ref[0] = jnp.dot(h, w_ref[...], preferred_element_type=F32)


def proj_in(x, sh, sc, g, weights, tm=512):
    B, T, D = x.shape
    tm = min(tm, T)
    vec = pl.BlockSpec((1, 1, D), lambda b, i: (b, 0, 0))
    in_specs = [pl.BlockSpec((1, tm, D), lambda b, i: (b, i, 0)), vec, vec,
                pl.BlockSpec((1, D), lambda b, i: (0, 0))]
    in_specs += [pl.BlockSpec(w.shape, lambda b, i: (0, 0)) for w in weights]
    out_specs = [pl.BlockSpec((1, tm, w.shape[1]), lambda b, i: (b, i, 0)) for w in weights]
    out_shape = [jax.ShapeDtypeStruct((B, T, w.shape[1]), F32) for w in weights]
    return pl.pallas_call(
        _proj_in_kernel,
        grid=(B, T // tm),
        in_specs=in_specs, out_specs=out_specs, out_shape=out_shape,
        compiler_params=_cparams("parallel", "parallel"),
        name="proj_in",
    )(x, sh.reshape(B, 1, D), sc.reshape(B, 1, D), g.reshape(1, D), *weights)


CONV_HALO = 32


def _conv_kernel(p_ref, w_ref, b_ref, gg_ref, gb_ref, m_ref, o_ref, ext_ref):
    i = pl.program_id(1)
    tt, dc = o_ref.shape[1], o_ref.shape[2]

    @pl.when(i == 0)
    def _():
        ext_ref[0:CONV_HALO, :] = jnp.zeros((CONV_HALO, dc), F32)

    @pl.when(i > 0)
    def _():
        ext_ref[0:CONV_HALO, :] = ext_ref[tt:tt + CONV_HALO, :]

    p = p_ref[0]
    ext_ref[CONV_HALO:CONV_HALO + tt, :] = p[:, :dc] * jax.nn.sigmoid(p[:, dc:])
    off = CONV_HALO - (CONV_WIDTH - 1)
    acc = jnp.zeros((tt, dc), F32) + b_ref[...]
    for j in range(CONV_WIDTH):
        acc = acc + ext_ref[off + j:off + j + tt, :] * w_ref[j:j + 1, :]
    m = m_ref[...]
    mu = _dot_hi(acc, m)
    d = acc - mu
    var = _dot_hi(d * d, m)
    y = d * lax.rsqrt(var + CONV_EPS) * gg_ref[...] + gb_ref[...]
    o_ref[0] = y * jax.nn.sigmoid(y)


def conv_mixer(p, dw_w, dw_b, gn_g, gn_b, tt=512):
    B, T, two_dc = p.shape
    dc = two_dc // 2
    tt = min(tt, T)
    wpad = jnp.zeros((32, dc), F32).at[:CONV_WIDTH].set(dw_w.reshape(CONV_WIDTH, dc))
    row = pl.BlockSpec((1, dc), lambda b, i: (0, 0))
    return pl.pallas_call(
        _conv_kernel,
        grid=(B, T // tt),
        in_specs=[pl.BlockSpec((1, tt, two_dc), lambda b, i: (b, i, 0)),
                  pl.BlockSpec((32, dc), lambda b, i: (0, 0)), row, row, row,
                  pl.BlockSpec((dc, dc), lambda b, i: (0, 0))],
        out_specs=pl.BlockSpec((1, tt, dc), lambda b, i: (b, i, 0)),
        out_shape=jax.ShapeDtypeStruct((B, T, dc), F32),
        scratch_shapes=[pltpu.VMEM((tt + CONV_HALO, dc), F32)],
        compiler_params=_cparams("parallel", "arbitrary"),
        name="conv_mixer",
    )(p, wpad, dw_b.reshape(1, dc), gn_g.reshape(1, dc), gn_b.reshape(1, dc),
      _group_avg_matrix(dc, HEAD_DIM))


SCAN_SUB = 64
SCAN_CHUNK = 128


def _group_sum_matrix(n, group):
    i = np.arange(n)
    return jnp.asarray((i[:, None] // group == i[None, :] // group).astype(np.float32))


def _rwkv_pre_kernel(p_ref, halo_ref, mu_ref, w0_ref, wup_ref, a0_ref, aup_ref, gup_ref, kk_ref, ka_ref,
                     rk_ref, ones_ref, tri_ref, w_o, kk_o, b_o, k_o, v_o, r_o, bonus_o, g_o, ext_ref):
    i = pl.program_id(1)
    tt = p_ref.shape[1]
    dr = w_o.shape[2]
    p = p_ref[0]
    first = (i > 0).astype(F32)
    ext_ref[0:8, :] = halo_ref[0] * first
    ext_ref[8:8 + tt, :] = p
    prev = ext_ref[7:7 + tt, :]
    xs = p + (prev - p) * mu_ref[...]
    r = xs[:, 0:dr]
    k = xs[:, dr:2 * dr]
    v = xs[:, 2 * dr:3 * dr]
    o = 3 * dr
    xw = xs[:, o:o + LORA_W]
    xa = xs[:, o + LORA_W:o + LORA_W + LORA_A]
    xg = xs[:, o + LORA_W + LORA_A:o + LORA_W + LORA_A + LORA_G]
    z = w0_ref[...] + _dot_hi(jnp.tanh(xw), wup_ref[...])
    w_log = -jax.nn.softplus(-z) - 0.5
    log_decay = -jnp.exp(w_log)
    cum = _dot_hi(tri_ref[...], log_decay)
    p_incl = jnp.exp(cum)
    p_excl = jnp.exp(cum - log_decay)
    p_inv = jnp.exp(-cum)
    a = jax.nn.sigmoid(a0_ref[...] + _dot_hi(xa, aup_ref[...]))
    g = _dot_hi(jax.nn.sigmoid(xg), gup_ref[...])
    ones = ones_ref[...]
    kk = k * kk_ref[...]
    nrm = jnp.sqrt(_dot_hi(kk * kk, ones))
    kk = kk / jnp.maximum(nrm, 1e-12)
    k_eff = k * (1.0 + (a - 1.0) * ka_ref[...])
    bonus = _dot_hi(r * k_eff * rk_ref[...], ones) * v
    w_o[0] = p_incl
    kk_o[0] = kk * p_excl
    b_o[0] = kk * a * p_inv
    k_o[0] = k_eff * p_inv
    v_o[0] = v
    r_o[0] = r * p_incl
    bonus_o[0] = bonus
    g_o[0] = g


def rwkv_pre(p, mu, w0, w_up, a0, a_up, g_up, k_k, k_a, r_k, tt=512):
    B, T, n_in = p.shape
    dr = w0.shape[0]
    tt = min(tt, T)
    nb8 = tt // 8
    full = lambda a: pl.BlockSpec(a.shape, lambda b, i: (0,) * a.ndim)
    blk_id = np.arange(tt) // SCAN_SUB
    tri = jnp.asarray(((blk_id[:, None] == blk_id[None, :])
                       & (np.arange(tt)[:, None] >= np.arange(tt)[None, :])).astype(np.float32))
    args = [mu.reshape(1, n_in), w0.reshape(1, dr), w_up, a0.reshape(1, dr), a_up, g_up,
            k_k.reshape(1, dr), k_a.reshape(1, dr), r_k.reshape(1, dr), _group_sum_matrix(dr, HEAD_DIM), tri]
    out = jax.ShapeDtypeStruct((B, T, dr), F32)
    ospec = pl.BlockSpec((1, tt, dr), lambda b, i: (b, i, 0))
    return pl.pallas_call(
        _rwkv_pre_kernel,
        grid=(B, T // tt),
        in_specs=[pl.BlockSpec((1, tt, n_in), lambda b, i: (b, i, 0)),
                  pl.BlockSpec((1, 8, n_in), lambda b, i: (b, jnp.maximum(i * nb8 - 1, 0), 0))]
                 + [full(a) for a in args],
        out_specs=[ospec] * 8, out_shape=[out] * 8,
        scratch_shapes=[pltpu.VMEM((tt + 8, n_in), F32)],
        compiler_params=_cparams("parallel", "parallel"),
        name="rwkv_pre",
    )(p, p, *args)


def _scan_select_matrices():
    e = np.zeros((SCAN_SUB // 2, 4 * SCAN_SUB, 2 * LANES), np.float32)
    for t in range(SCAN_SUB):
        c = (t % 2) * LANES
        for part in range(2):
            e[t // 2, 2 * part * SCAN_SUB + t, c:c + HEAD_DIM] = 1.0
            e[t // 2, (2 * part + 1) * SCAN_SUB + t, c + HEAD_DIM:c + LANES] = 1.0
    return jnp.asarray(e, BF16)


def _rwkv_scan_kernel(pc_ref, kk_ref, b_ref, k_ref, r_ref, v_ref, bonus_ref, g_ref, lng_ref, lnb_ref,
                      e_ref, avg_ref, o_ref, st_ref, lhs_ref, y_ref):
    nb, tc, dr = v_ref.shape
    n_half = dr // LANES
    n_pair = nb * n_half
    n_sub = tc // SCAN_SUB
    quantities = (kk_ref, b_ref, k_ref, r_ref)

    @pl.when(pl.program_id(0) == 0)
    def _():
        st_ref[...] = jnp.zeros(st_ref.shape, F32)

    for qi, q_ref in enumerate(quantities):
        for p in range(n_pair):
            b, hb = divmod(p, n_half)
            xt = q_ref[b, :, hb * LANES:(hb + 1) * LANES].T
            for s in range(n_sub):
                cat = jnp.concatenate([xt[0:HEAD_DIM, s * SCAN_SUB:(s + 1) * SCAN_SUB],
                                       xt[HEAD_DIM:, s * SCAN_SUB:(s + 1) * SCAN_SUB]], axis=1)
                hi = cat.astype(BF16)
                lo = (cat - hi.astype(F32)).astype(BF16)
                row = (qi * n_pair + p) * HEAD_DIM
                lhs_ref[s, row:row + HEAD_DIM, :] = jnp.concatenate([hi, lo], axis=1)

    sub_iota = lax.broadcasted_iota(jnp.int32, (8, LANES), 0)
    ri = lax.broadcasted_iota(jnp.int32, (HEAD_DIM, LANES), 0)
    ci = lax.broadcasted_iota(jnp.int32, (HEAD_DIM, LANES), 1)
    diag2 = (ci % HEAD_DIM == ri).astype(F32)
    rj = lax.broadcasted_iota(jnp.int32, (LANES, LANES), 0)
    cj = lax.broadcasted_iota(jnp.int32, (LANES, LANES), 1)
    half_ones = (rj // HEAD_DIM == cj // HEAD_DIM).astype(F32)
    for s in range(n_sub):
        def steps8(t8, carry, s=s):
            base = pl.multiple_of(s * SCAN_SUB + t8 * 8, 8)
            y8 = [jnp.zeros((8, LANES), F32) for _ in range(n_pair)]
            for j in range(8):
                if j % 2 == 0:
                    z2 = jnp.dot(lhs_ref[s], e_ref[t8 * 4 + j // 2], preferred_element_type=F32)
                z = z2[:, (j % 2) * LANES:(j % 2 + 1) * LANES]
                for p in range(n_pair):
                    b, hb = divmod(p, n_half)
                    col = lambda qi: z[(qi * n_pair + p) * HEAD_DIM:(qi * n_pair + p + 1) * HEAD_DIM, :]
                    st = st_ref[p]
                    sa = jnp.sum(st * col(0), axis=0, keepdims=True)
                    v_row = v_ref[b, pl.ds(base, 8), hb * LANES:(hb + 1) * LANES][j:j + 1, :]
                    st = st - col(1) * sa + col(2) * v_row
                    y = jnp.sum(st * col(3), axis=0, keepdims=True)
                    y8[p] = jnp.where(sub_iota == j, y, y8[p])
                    st_ref[p] = st
            for p in range(n_pair):
                b, hb = divmod(p, n_half)
                y_ref[b, pl.ds(base, 8), hb * LANES:(hb + 1) * LANES] = y8[p]
            return carry
        lax.fori_loop(0, SCAN_SUB // 8, steps8, 0)
        last = s * SCAN_SUB + SCAN_SUB - 1
        for p in range(n_pair):
            b, hb = divmod(p, n_half)
            p_row = pc_ref[b, last - 7:last + 1, hb * LANES:(hb + 1) * LANES][7:8, :]
            st_ref[p] = st_ref[p] * _dot_hi(diag2 * p_row, half_ones)

    avg = avg_ref[...]
    for b in range(nb):
        y = y_ref[b]
        mu = _dot_hi(y, avg)
        d = y - mu
        var = _dot_hi(d * d, avg)
        yn = d * lax.rsqrt(var + RWKV_GN_EPS) * lng_ref[...] + lnb_ref[...]
        o_ref[b] = (yn + bonus_ref[b]) * g_ref[b]


def rwkv_scan(w, kk, bq, k, r, v, bonus, g, ln_g, ln_b):
    B, T, dr = v.shape
    tc = min(SCAN_CHUNK, T)
    n_pair = B * dr // LANES
    blk = pl.BlockSpec((B, tc, dr), lambda i: (0, i, 0))
    row = pl.BlockSpec((1, dr), lambda i: (0, 0))
    e = _scan_select_matrices()
    return pl.pallas_call(
        _rwkv_scan_kernel,
        grid=(T // tc,),
        in_specs=[blk] * 8 + [row, row, pl.BlockSpec(e.shape, lambda i: (0, 0, 0)),
                              pl.BlockSpec((dr, dr), lambda i: (0, 0))],
        out_specs=blk,
        out_shape=jax.ShapeDtypeStruct((B, T, dr), F32),
        scratch_shapes=[pltpu.VMEM((n_pair, HEAD_DIM, LANES), F32),
                        pltpu.VMEM((tc // SCAN_SUB, 4 * n_pair * HEAD_DIM, 4 * SCAN_SUB), BF16),
                        pltpu.VMEM((B, tc, dr), F32)],
        compiler_params=_cparams("arbitrary"),
        name="rwkv_scan",
    )(w, kk, bq, k, r, v, bonus, g, ln_g.reshape(1, dr), ln_b.reshape(1, dr), e, _group_avg_matrix(dr, HEAD_DIM))


def rwkv_mixer(p, mu, w0, w_up, a0, a_up, g_up, k_k, k_a, r_k, ln_g, ln_b):
    w, kk, bq, k, v, r, bonus, g = rwkv_pre(p, mu, w0, w_up, a0, a_up, g_up, k_k, k_a, r_k.reshape(-1))
    return rwkv_scan(w, kk, bq, k, r, v, bonus, g, ln_g, ln_b)


def _dot_nt(a, b, **kw):
    return lax.dot_general(a, b, (((1,), (1,)), ((), ())), preferred_element_type=F32, **kw)


def _compress_kernel(kz_ref, vz_ref, kpos_ref, kw1_ref, kw2_ref, vpos_ref, vw1_ref, vw2_ref,
                     kc_ref, vc_ref, shift_ref):
    n = kz_ref.shape[2]
    half = kz_ref.shape[3]

    def one(z_ref, pos_ref, w1_ref, w2_ref, o_ref):
        z = z_ref[0, 0]
        top = _dot_bf(z + pos_ref[0:1, :], w1_ref[0:half, :])
        bot = _dot_bf(z + pos_ref[1:2, :], w1_ref[half:2 * half, :])
        shift_ref[0:n, :] = bot
        shift_ref[n:n + 8, :] = jnp.zeros((8, bot.shape[1]), F32)
        pre = top + shift_ref[1:n + 1, :]
        hid = pre * jax.nn.sigmoid(pre)
        o_ref[0, 0] = _dot_bf(hid, w2_ref[...])

    one(kz_ref, kpos_ref, kw1_ref, kw2_ref, kc_ref)
    one(vz_ref, vpos_ref, vw1_ref, vw2_ref, vc_ref)


def nsa_compress(k_cmp, v_cmp, ck_pos, ck_w1, ck_w2, cv_pos, cv_w1, cv_w2):
    B, G, T, dk = k_cmp.shape
    n = T // CMP_STRIDE
    half = CMP_STRIDE * dk
    zspec = pl.BlockSpec((1, 1, n, half), lambda b, g: (b, g, 0, 0))
    full = lambda a: pl.BlockSpec(a.shape, lambda b, g: (0,) * a.ndim)
    ospec = pl.BlockSpec((1, 1, n, dk), lambda b, g: (b, g, 0, 0))
    args = [ck_pos.reshape(2, half), ck_w1.astype(BF16), ck_w2.astype(BF16),
            cv_pos.reshape(2, half), cv_w1.astype(BF16), cv_w2.astype(BF16)]
    return pl.pallas_call(
        _compress_kernel,
        grid=(B, G),
        in_specs=[zspec, zspec] + [full(a) for a in args],
        out_specs=[ospec, ospec],
        out_shape=[jax.ShapeDtypeStruct((B, G, n, dk), F32)] * 2,
        scratch_shapes=[pltpu.VMEM((n + 8, CMP_HIDDEN), F32)],
        compiler_params=_cparams("parallel", "parallel"),
        name="nsa_compress",
    )(k_cmp.reshape(B, G, n, half), v_cmp.reshape(B, G, n, half), *args)


NSA_TQ = 128
NSA_TK = 1024


def _split_bf16(a):
    hi = a.astype(BF16)
    return hi, (a - hi.astype(F32)).astype(BF16)


def _dot_3pass(a, b):
    ah, al = _split_bf16(a)
    bh, bl = _split_bf16(b)
    d = lambda x, y: jnp.dot(x, y, preferred_element_type=F32)
    return d(ah, bh) + d(ah, bl) + d(al, bh)


def _nsa_kernel(q_ref, gate_ref, kc_ref, vct_ref, ks_ref, vst_ref, kw_ref, vwt_ref, ovt_ref,
                o_ref, m_ref, acc_ref, ow_ref, sel_ref, s0_ref, s1_ref, p_ref):
    qi = pl.program_id(2)
    dk = q_ref.shape[3]
    tq = gate_ref.shape[3]
    R = q_ref.shape[4] // tq
    n_c = kc_ref.shape[2]
    n_blk = ovt_ref.shape[0]
    T = ks_ref.shape[2]
    tk = s0_ref.shape[0]
    t0 = qi * tq
    qt = q_ref[0, 0, 0]
    qt_bf = qt.astype(BF16)
    t_row = t0 + lax.broadcasted_iota(jnp.int32, (1, tq), 1)
    lanes = lambda r: slice(r * tq, (r + 1) * tq)

    s = _dot_3pass(kc_ref[0, 0], qt)
    c_end = lax.broadcasted_iota(jnp.int32, (n_c, tq), 0) * CMP_STRIDE + (CMP_BLOCK - 1)
    valid_c = c_end <= t_row
    any_c = (t_row >= CMP_BLOCK - 1).astype(F32)
    p_sum = jnp.zeros((n_c, tq), F32)
    ps = []
    for r in range(R):
        s_r = jnp.where(valid_c, s[:, lanes(r)], NEG_INF)
        e = jnp.exp2(s_r - jnp.max(s_r, axis=0, keepdims=True))
        p_r = e * (any_c / jnp.sum(e, axis=0, keepdims=True))
        p_sum = p_sum + p_r
        ps.append(p_r.astype(BF16))
    o_c = jnp.dot(vct_ref[0, 0].astype(BF16), jnp.concatenate(ps, axis=1), preferred_element_type=F32)

    imp = _dot_hi(ovt_ref[...], p_sum)
    blk = lax.broadcasted_iota(jnp.int32, (n_blk, tq), 0).astype(F32)
    cur = (t_row // SEL_BLOCK).astype(F32)
    forced = (blk == 0.0) | (blk == cur) | (blk == cur - 1.0)
    x = jnp.where(forced, FORCE_SCORE, jnp.where(blk <= cur, imp, -1.0))
    x = jnp.where(blk < float(T // SEL_BLOCK), x, -3e38)
    sel = jnp.zeros((n_blk, tq), F32)
    for _ in range(N_SEL):
        mx = jnp.max(x, axis=0, keepdims=True)
        idx = jnp.min(jnp.where(x == mx, blk, float(n_blk)), axis=0, keepdims=True)
        hit = blk == idx
        sel = jnp.where(hit, 1.0, sel)
        x = jnp.where(hit, -jnp.inf, x)

    span = WINDOW + tq
    w0 = pl.multiple_of(jnp.maximum(t0 - WINDOW, 0), tq)
    s = jnp.dot(kw_ref[0, 0, pl.ds(w0, span), :], qt_bf, preferred_element_type=F32)
    dist = t_row - (w0 + lax.broadcasted_iota(jnp.int32, (span, tq), 0))
    bias_w = jnp.where((dist >= 0) & (dist < WINDOW), 0.0, NEG_INF)
    ps = []
    for r in range(R):
        s_r = s[:, lanes(r)] + bias_w
        ps.append(jnp.exp2(s_r - jnp.max(s_r, axis=0, keepdims=True)).astype(BF16))
    ow_ref[...] = jnp.dot(vwt_ref[0, 0, :, pl.ds(w0, span)], jnp.concatenate(ps, axis=1),
                          preferred_element_type=F32)

    bpt = tk // SEL_BLOCK
    n_kt = T // tk
    n_full = (t0 + tq - 1) // tk
    sel_ref[...] = jnp.where(sel > 0.5, 0.0, NEG_INF)
    m_ref[...] = jnp.full(m_ref.shape, NEG_INF, F32)
    acc_ref[...] = jnp.zeros(acc_ref.shape, F32)
    q_pad = jnp.zeros((ks_ref.shape[3] - dk - bpt, R * tq), BF16)

    def scores(kt):
        kt = jnp.minimum(kt, n_kt - 1)
        rows = sel_ref[pl.ds(pl.multiple_of(kt * bpt, bpt), bpt), :].astype(BF16)
        q_aug = jnp.concatenate([qt_bf, jnp.concatenate([rows] * R, axis=1), q_pad], axis=0)
        off = pl.multiple_of(kt * tk, tk)
        return jnp.dot(ks_ref[0, 0, pl.ds(off, tk), :], q_aug, preferred_element_type=F32)

    def softmax_tile(s_ref, kt, causal):
        if causal:
            key_pos = kt * tk + lax.broadcasted_iota(jnp.int32, (tk, tq), 0)
            bias = jnp.where(key_pos <= t_row, 0.0, NEG_INF)
        for r in range(R):
            m_old = m_ref[:, lanes(r)]
            if causal:
                m_new = jnp.maximum(m_old, jnp.max(s_ref[:, lanes(r)] + bias, axis=0, keepdims=True))
                p_r = jnp.exp2((s_ref[:, lanes(r)] - m_new) + bias)
            else:
                m_new = jnp.maximum(m_old, jnp.max(s_ref[:, lanes(r)], axis=0, keepdims=True))
                p_r = jnp.exp2(s_ref[:, lanes(r)] - m_new)
            m_ref[:, lanes(r)] = m_new
            p_ref[:, lanes(r)] = p_r.astype(BF16)
            acc_ref[:, lanes(r)] = acc_ref[:, lanes(r)] * jnp.exp2(m_old - m_new)
        off = pl.multiple_of(kt * tk, tk)
        acc_ref[...] += jnp.dot(vst_ref[0, 0, :, pl.ds(off, tk)], p_ref[...], preferred_element_type=F32)

    s0_ref[...] = scores(0)

    def tile_pair(i, carry):
        s1_ref[...] = scores(2 * i + 1)
        softmax_tile(s0_ref, 2 * i, False)
        s0_ref[...] = scores(2 * i + 2)
        softmax_tile(s1_ref, 2 * i + 1, False)
        return carry

    lax.fori_loop(0, n_full // 2, tile_pair, 0)
    odd = n_full % 2 == 1

    @pl.when(odd)
    def _():
        s1_ref[...] = scores(n_full)
        softmax_tile(s0_ref, n_full - 1, False)
        softmax_tile(s1_ref, n_full, True)

    @pl.when(jnp.logical_not(odd))
    def _():
        softmax_tile(s0_ref, n_full, True)

    acc = acc_ref[...]
    o_s = acc[0:dk] * (1.0 / acc[dk:dk + 1])
    o_w = ow_ref[...]
    o_w = o_w[0:dk] * (1.0 / o_w[dk:dk + 1])

    gate = jax.nn.sigmoid(gate_ref[0, 0])
    outs = []
    for r in range(R):
        outs.append(gate[r:r + 1] * o_c[:, lanes(r)] + gate[R + r:R + r + 1] * o_s[:, lanes(r)]
                    + gate[2 * R + r:2 * R + r + 1] * o_w[:, lanes(r)])
    o_ref[0] = jnp.concatenate(outs, axis=0).T


def _overlap_matrix_t(n_c, n_blk):
    c0 = np.arange(n_c)[None, :] * CMP_STRIDE
    s0 = np.arange(n_blk)[:, None] * SEL_BLOCK
    return jnp.asarray(((c0 < s0 + SEL_BLOCK) & (c0 + CMP_BLOCK > s0)).astype(np.float32))


def nsa_attention(qt, gate_t, kc, vct, k_sel, vt_sel, k_win, vt_win):
    B, G, nq, dk, L = qt.shape
    T = k_win.shape[2]
    tq = T // nq
    R = L // tq
    n_c = kc.shape[2]
    tk = min(NSA_TK, T)
    n_blk = max(T // SEL_BLOCK, LANES)
    ovt = _overlap_matrix_t(n_c, n_blk)
    res = lambda a: pl.BlockSpec((1, 1) + a.shape[2:], lambda b, g, i: (b, g, 0, 0))
    return pl.pallas_call(
        _nsa_kernel,
        grid=(B, G, nq),
        in_specs=[pl.BlockSpec((1, 1, 1, dk, L), lambda b, g, i: (b, g, i, 0, 0)),
                  pl.BlockSpec((1, 1, gate_t.shape[2], tq), lambda b, g, i: (b, g, 0, i)),
                  res(kc), res(vct), res(k_sel), res(vt_sel), res(k_win), res(vt_win),
                  pl.BlockSpec(ovt.shape, lambda b, g, i: (0, 0))],
        out_specs=pl.BlockSpec((1, tq, R * dk), lambda b, g, i: (b, i, g)),
        out_shape=jax.ShapeDtypeStruct((B, T, G * R * dk), F32),
        scratch_shapes=[pltpu.VMEM((1, L), F32), pltpu.VMEM((dk + 16, L), F32),
                        pltpu.VMEM((dk + 16, L), F32), pltpu.VMEM((n_blk, tq), F32),
                        pltpu.VMEM((tk, L), F32), pltpu.VMEM((tk, L), F32), pltpu.VMEM((tk, L), BF16)],
        compiler_params=_cparams("parallel", "parallel", "arbitrary"),
        name="nsa_attention",
    )(qt, gate_t, kc, vct, k_sel, vt_sel, k_win, vt_win, ovt)


def nsa_mixer(p_q, p_kv, p_gate, ck_pos, ck_w1, ck_w2, cv_pos, cv_w1, cv_w2):
    B, T, _ = p_q.shape
    G, R, dk = NSA_KV_HEADS, NSA_GROUP, HEAD_DIM
    tq = min(NSA_TQ, T)
    kv6 = p_kv.reshape(B, T, 6, G, dk)
    rows = lambda i: kv6[:, :, i].transpose(0, 2, 1, 3)
    cols = lambda i: kv6[:, :, i].transpose(0, 2, 3, 1)
    kc, vc = nsa_compress(rows(0), rows(1), ck_pos, ck_w1, ck_w2, cv_pos, cv_w1, cv_w2)
    qt = (p_q * (dk ** -0.5 * LOG2E)).reshape(B, T // tq, tq, G, R, dk).transpose(0, 3, 1, 5, 4, 2)
    qt = qt.reshape(B, G, T // tq, dk, R * tq)
    gate_t = p_gate[..., :3 * G * R].reshape(B, T, 3, G, R).transpose(0, 3, 2, 4, 1).reshape(B, G, 3 * R, T)
    gate_t = jnp.pad(gate_t, ((0, 0), (0, 0), (0, 16 - 3 * R), (0, 0)))
    tk = min(NSA_TK, T)
    blk_onehot = (jnp.arange(T)[:, None] // SEL_BLOCK % (tk // SEL_BLOCK) == jnp.arange(dk)[None, :]).astype(BF16)
    k_sel = jnp.concatenate([rows(2).astype(BF16), jnp.broadcast_to(blk_onehot, (B, G, T, dk))], axis=-1)
    ones_rows = jnp.zeros((B, G, 16, T), BF16).at[:, :, 0].set(1.0)
    with_ones = lambda vt: jnp.concatenate([vt.astype(BF16), ones_rows], axis=2)
    return nsa_attention(qt, gate_t, kc, vc.transpose(0, 1, 3, 2), k_sel, with_ones(cols(3)),
                         rows(4).astype(BF16), with_ones(cols(5)))


def _proj_out_kernel(x_ref, g1_ref, sh_ref, sc_ref, ng_ref, *refs):
    n = (len(refs) - 2) // 2
    x1_ref, h2_ref = refs[2 * n:]
    acc = None
    for m_ref, w_ref in zip(refs[:n], refs[n:2 * n]):
        d = jnp.dot(m_ref[0].astype(BF16), w_ref[...], preferred_element_type=F32)
        acc = d if acc is None else acc + d
    x1 = x_ref[0] + g1_ref[0] * acc
    x1_ref[0] = x1
    h2_ref[0] = _modulated_norm(x1, ng_ref[...], sc_ref[0], sh_ref[0]).astype(BF16)


def proj_out(x, g1, sh2, sc2, norm_g, mixes, weights, tm=512):
    B, T, D = x.shape
    tm = min(tm, T)
    vec = pl.BlockSpec((1, 1, D), lambda b, i: (b, 0, 0))
    tile = lambda n: pl.BlockSpec((1, tm, n), lambda b, i: (b, i, 0))
    return pl.pallas_call(
        _proj_out_kernel,
        grid=(B, T // tm),
        in_specs=[tile(D), vec, vec, vec, pl.BlockSpec((1, D), lambda b, i: (0, 0))]
                 + [tile(m.shape[2]) for m in mixes]
                 + [pl.BlockSpec(w.shape, lambda b, i: (0, 0)) for w in weights],
        out_specs=[tile(D), tile(D)],
        out_shape=[jax.ShapeDtypeStruct((B, T, D), F32), jax.ShapeDtypeStruct((B, T, D), BF16)],
        compiler_params=_cparams("parallel", "parallel"),
        name="proj_out",
    )(x, g1.reshape(B, 1, D), sh2.reshape(B, 1, D), sc2.reshape(B, 1, D), norm_g.reshape(1, D), *mixes, *weights)


_CAND_BLOCKS = [(0, 16), (1, 8), (None, 8), (2, 5), (3, 4), (4, 3), (5, 2), (6, 2), (7, 2)]
_CAND_PAD = 6


def _cand_constants(tm):
    flat, neg = [], []
    for i, rows in _CAND_BLOCKS:
        for r in range(rows):
            flat.append((8 + r) * PEER_TOPK if i is None else i * PEER_TOPK + r)
            neg.append(0.0)
    flat += [1e9] * _CAND_PAD
    neg += [-np.inf] * _CAND_PAD
    flat = np.tile(np.asarray(flat, np.float32)[:, None], (1, tm))
    neg = np.tile(np.asarray(neg, np.float32)[:, None], (1, tm))
    return jnp.asarray(flat), jnp.asarray(neg)


def _topk_rows(x, row_id, k, n_rows):
    vals, idxs = [], []
    for _ in range(k):
        mx = jnp.max(x, axis=0, keepdims=True)
        idx = jnp.min(jnp.where(x == mx, row_id, float(n_rows)), axis=0, keepdims=True)
        x = jnp.where(row_id == idx, -jnp.inf, x)
        vals.append(mx)
        idxs.append(idx)
    return jnp.concatenate(vals, axis=0), jnp.concatenate(idxs, axis=0)


def _sorting_network(n):
    pairs = []
    p = 1
    while p < n:
        k = p
        while k >= 1:
            for j in range(k % p, n - k, 2 * k):
                for i in range(min(k, n - j - k)):
                    if (i + j) // (2 * p) == (i + j + k) // (2 * p):
                        pairs.append((i + j, i + j + k))
            k //= 2
        p *= 2
    return pairs


def _topk_sorted_columns(x, k):
    n_rows, m = x.shape
    n_lvl = n_rows // 8
    slot = lax.broadcasted_iota(jnp.int32, (8, m), 0).astype(F32)
    s = [x[8 * v:8 * v + 8] for v in range(n_lvl)]
    p = [slot + 8.0 * v for v in range(n_lvl)]
    for i, j in _sorting_network(n_lvl):
        swap = s[j] > s[i]
        s[i], s[j] = jnp.where(swap, s[j], s[i]), jnp.where(swap, s[i], s[j])
        p[i], p[j] = jnp.where(swap, p[j], p[i]), jnp.where(swap, p[i], p[j])
    tie = jnp.zeros((8, m), F32)
    for d in range(n_lvl - 1):
        tie = jnp.where(s[d] == s[d + 1], 1.0, tie)
    tie = jnp.max(tie, axis=0, keepdims=True)
    s.append(jnp.full((8, m), -jnp.inf, F32))
    p.append(jnp.zeros((8, m), F32))
    vals, idxs = [], []
    for r in range(k):
        mx = jnp.max(s[0], axis=0, keepdims=True)
        hit = s[0] == mx
        cnt = jnp.sum(jnp.where(hit, 1.0, 0.0), axis=0, keepdims=True)
        tie = jnp.where(cnt > 1.5, 1.0, tie)
        vals.append(mx)
        idxs.append(jnp.sum(jnp.where(hit, p[0], 0.0), axis=0, keepdims=True))
        for d in range(n_lvl - r):
            s[d] = jnp.where(hit, s[d + 1], s[d])
            p[d] = jnp.where(hit, p[d + 1], p[d])
    tie = jnp.where(jnp.max(s[0], axis=0, keepdims=True) == vals[-1], 1.0, tie)
    return jnp.concatenate(vals, axis=0), jnp.concatenate(idxs, axis=0), tie


def _peer_select_kernel(h_ref, wq_ref, k1_ref, k2_ref, flat_ref, neg_ref, a_ref, b_ref, g_ref):
    tm = h_ref.shape[0]
    nk = k1_ref.shape[0]
    half = k1_ref.shape[1]
    q = jnp.dot(h_ref[...], wq_ref[...], preferred_element_type=F32)
    flat = flat_ref[...]
    neg = neg_ref[...]
    k1 = k1_ref[...]
    k2 = k2_ref[...]

    def route(topk):
        a_all, b_all, g_all = [], [], []
        tie = jnp.zeros((1, tm), F32)
        for h in range(PEER_HEADS):
            q1 = q[:, (2 * h) * half:(2 * h + 1) * half]
            q2 = q[:, (2 * h + 1) * half:(2 * h + 2) * half]
            v1, i1, t1 = topk(_dot_nt(k1, q1, precision=HIGHEST))
            v2, i2, t2 = topk(_dot_nt(k2, q2, precision=HIGHEST))
            tie = jnp.maximum(tie, jnp.maximum(t1, t2))
            vals, ai, bj = [], [], []
            for i, rows in _CAND_BLOCKS:
                if i is None:
                    vals.append(v1[8:16] + v2[0:1])
                    ai.append(i1[8:16])
                    bj.append(jnp.broadcast_to(i2[0:1], (8, tm)))
                else:
                    vals.append(v1[i:i + 1] + v2[0:rows])
                    ai.append(jnp.broadcast_to(i1[i:i + 1], (rows, tm)))
                    bj.append(i2[0:rows])
            pad = [jnp.zeros((_CAND_PAD, tm), F32)]
            cand = jnp.concatenate(vals + pad, axis=0) + neg
            ai = jnp.concatenate(ai + pad, axis=0)
            bj = jnp.concatenate(bj + pad, axis=0)
            sc, sa, sb = [], [], []
            for _ in range(PEER_TOPK):
                mx = jnp.max(cand, axis=0, keepdims=True)
                fid = jnp.min(jnp.where(cand == mx, flat, 1e9), axis=0, keepdims=True)
                hit = flat == fid
                sc.append(mx)
                sa.append(jnp.sum(jnp.where(hit, ai, 0.0), axis=0, keepdims=True))
                sb.append(jnp.sum(jnp.where(hit, bj, 0.0), axis=0, keepdims=True))
                cand = jnp.where(hit, -jnp.inf, cand)
            sc = jnp.concatenate(sc, axis=0)
            e = jnp.exp(sc - sc[0:1])
            g_all.append(e / jnp.sum(e, axis=0, keepdims=True))
            a_all.append(jnp.concatenate(sa, axis=0))
            b_all.append(jnp.concatenate(sb, axis=0))
        a_ref[...] = jnp.concatenate(a_all, axis=0).T
        b_ref[...] = jnp.concatenate(b_all, axis=0).T
        g_ref[...] = jnp.concatenate(g_all, axis=0).T
        return tie

    tie = route(functools.partial(_topk_sorted_columns, k=PEER_TOPK))

    @pl.when(jnp.max(tie) > 0.0)
    def _():
        key_id = lax.broadcasted_iota(jnp.int32, (nk, tm), 0).astype(F32)
        route(lambda s: _topk_rows(s, key_id, PEER_TOPK, nk) + (jnp.zeros((1, tm), F32),))


def peer_select(h2, wq, k1, k2, tm=256):
    N, D = h2.shape
    tm = min(tm, N)
    flat, neg = _cand_constants(tm)
    full = lambda a: pl.BlockSpec(a.shape, lambda i: (0,) * a.ndim)
    out = jax.ShapeDtypeStruct((N, PEER_HEADS * PEER_TOPK), F32)
    ospec = pl.BlockSpec((tm, PEER_HEADS * PEER_TOPK), lambda i: (i, 0))
    return pl.pallas_call(
        _peer_select_kernel,
        grid=(N // tm,),
        in_specs=[pl.BlockSpec((tm, D), lambda i: (i, 0)), full(wq), full(k1), full(k2), full(flat), full(neg)],
        out_specs=[ospec] * 3, out_shape=[out] * 3,
        compiler_params=_cparams("parallel"),
        name="peer_select",
    )(h2, wq, k1, k2, flat, neg)


PEER_BUILD_GROUP = 16
PEER_BUILD_UNROLL = 2


def _gelu(x):
    return 0.5 * x * (1.0 + lax.erf(x * 0.7071067811865476))


def _peer_expert_kernel(*refs, final, n_e):
    if final:
        h_ref, a_ref, b_ref, g_ref, u_ref, v_ref, x_ref, g2_ref, fg_ref, o_ref, w3_ref, acc_ref, ha_ref, hb_ref = refs
    else:
        h_ref, a_ref, b_ref, g_ref, u_ref, v_ref, x_ref, g2_ref, o_ref, w3_ref, acc_ref, ha_ref, hb_ref = refs
    j = pl.program_id(1)
    tm = h_ref.shape[0]
    te = u_ref.shape[0]
    nk = PEER_KEYS
    n_a = te // nk

    def hidden():
        return _dot_nt(h_ref[...], u_ref[...])

    def finish(hid_ref, tile):
        acts = []
        for al in range(n_a):
            w_a = w3_ref[tile * n_a + al].astype(F32)
            acts.append((_gelu(hid_ref[:, al * nk:(al + 1) * nk]) * w_a).astype(BF16))
        acc_ref[...] += jnp.dot(jnp.concatenate(acts, axis=1), v_ref[...], preferred_element_type=F32)

    @pl.when(j == 0)
    def _():
        acc_ref[...] = jnp.zeros(acc_ref.shape, F32)
        ha_ref[...] = hidden()
        key_id = lax.broadcasted_iota(jnp.int32, (nk, a_ref.shape[1]), 0).astype(F32)

        def tokens(tb, carry):
            for grp in range(PEER_BUILD_UNROLL):
                base = pl.multiple_of((tb * PEER_BUILD_UNROLL + grp) * PEER_BUILD_GROUP, PEER_BUILD_GROUP)
                a8 = a_ref[pl.ds(base, PEER_BUILD_GROUP), :]
                b8 = b_ref[pl.ds(base, PEER_BUILD_GROUP), :]
                g8 = g_ref[pl.ds(base, PEER_BUILD_GROUP), :]
                ws = []
                for r in range(PEER_BUILD_GROUP):
                    g_row = g8[r:r + 1]
                    g_hi = g_row.astype(BF16).astype(F32)
                    g_lo = g_row - g_hi
                    eq_a = key_id == a8[r:r + 1]
                    x = jnp.concatenate([jnp.where(eq_a, g_hi, 0.0).astype(BF16),
                                         jnp.where(eq_a, g_lo, 0.0).astype(BF16)], axis=1)
                    y1 = jnp.where(key_id == b8[r:r + 1], 1.0, 0.0).astype(BF16)
                    y = jnp.concatenate([y1, y1], axis=1)
                    ws.append(_dot_nt(x, y))
                w3_ref[:, pl.ds(base, PEER_BUILD_GROUP), :] = jnp.swapaxes(
                    jnp.stack(ws, axis=0).astype(BF16), 0, 1)
            return carry
        lax.fori_loop(0, tm // (PEER_BUILD_GROUP * PEER_BUILD_UNROLL), tokens, 0)

    for parity, (h_write, h_read) in enumerate(((ha_ref, hb_ref), (hb_ref, ha_ref))):
        @pl.when((j > 0) & (j < n_e) & (j % 2 == parity))
        def _(h_write=h_write, h_read=h_read):
            h_write[...] = hidden()
            finish(h_read, j - 1)

    @pl.when(j == n_e)
    def _():
        finish(hb_ref if (n_e - 1) % 2 else ha_ref, n_e - 1)
        y = x_ref[...] + g2_ref[0] * acc_ref[...]
        if final:
            ms = jnp.mean(y * y, axis=-1, keepdims=True)
            y = y * lax.rsqrt(ms + RMS_EPS) * fg_ref[...]
        o_ref[...] = y


def peer_experts(h2, a_idx, b_idx, gw, u, v, x1, g2, tokens_per_batch, final_g=None, tm=512, te=1024):
    N, D = x1.shape
    E = u.shape[0]
    tm = min(tm, N)
    nb = g2.shape[0]
    n_e = E // te
    tok = lambda n: pl.BlockSpec((tm, n), lambda i, j: (i, 0))
    in_specs = [tok(D), tok(a_idx.shape[1]), tok(a_idx.shape[1]), tok(a_idx.shape[1]),
                pl.BlockSpec((te, D), lambda i, j: (jnp.minimum(j, n_e - 1), 0)),
                pl.BlockSpec((te, D), lambda i, j: (jnp.maximum(j - 1, 0), 0)),
                tok(D), pl.BlockSpec((1, 1, D), lambda i, j: (i * tm // tokens_per_batch, 0, 0))]
    args = [h2, a_idx, b_idx, gw, u, v, x1, g2.reshape(nb, 1, D)]
    if final_g is not None:
        in_specs.append(pl.BlockSpec((1, D), lambda i, j: (0, 0)))
        args.append(final_g.reshape(1, D))
    return pl.pallas_call(
        functools.partial(_peer_expert_kernel, final=final_g is not None, n_e=n_e),
        grid=(N // tm, n_e + 1),
        in_specs=in_specs,
        out_specs=tok(D),
        out_shape=jax.ShapeDtypeStruct((N, D), F32),
        scratch_shapes=[pltpu.VMEM((PEER_KEYS, tm, PEER_KEYS), BF16), pltpu.VMEM((tm, D), F32),
                        pltpu.VMEM((tm, te), F32), pltpu.VMEM((tm, te), F32)],
        compiler_params=_cparams("parallel", "arbitrary"),
        name="peer_experts",
    )(*args)


def peer_layer(x1, h2, wq, k1, k2, u, v, g2, final_g=None):
    B, T, D = x1.shape
    h2f = h2.reshape(B * T, D)
    a_idx, b_idx, gw = peer_select(h2f, wq, k1, k2)
    out = peer_experts(h2f, a_idx, b_idx, gw, u, v, x1.reshape(B * T, D), g2, T, final_g)
    return out.reshape(B, T, D)


def kernel(x, c, ada_w, ada_b, norm1_g, norm2_g, w_in, conv_dw_w, conv_dw_b, conv_gn_g, conv_gn_b,
           rwkv_mu, rwkv_w0, rwkv_w_up, rwkv_a0, rwkv_a_up, rwkv_g_up, rwkv_k_k, rwkv_k_a, rwkv_r_k,
           rwkv_ln_g, rwkv_ln_b, nsa_ck_pos, nsa_ck_w1, nsa_ck_w2, nsa_cv_pos, nsa_cv_w1, nsa_cv_w2,
           w_out, peer_wq, peer_k1, peer_k2, peer_u, peer_v, final_g):
    depth, D = norm1_g.shape
    d_conv = conv_dw_b.shape[1]
    d_rwkv = rwkv_w0.shape[1]
    n_conv = 2 * d_conv
    n_rwkv = rwkv_mu.shape[1]
    d_nsa = D - d_conv - d_rwkv
    n_kv = 6 * NSA_KV_HEADS * HEAD_DIM
    n_gate = 3 * NSA_KV_HEADS * NSA_GROUP
    cuts = np.cumsum([0, n_conv, n_rwkv, d_nsa, n_kv, n_gate])
    mod = ada_mod(c, ada_w, ada_b)
    for l in range(depth):
        sh1, sc1, g1, sh2, sc2, g2 = [mod[l, :, i * D:(i + 1) * D] for i in range(6)]
        w_l = w_in[l].astype(BF16)
        pieces = [w_l[:, cuts[i]:cuts[i + 1]] for i in range(5)]
        pieces[4] = jnp.pad(pieces[4], ((0, 0), (0, LANES - n_gate)))
        p_conv, p_rwkv, p_q, p_kv, p_gate = proj_in(x, sh1, sc1, norm1_g[l], pieces)
        o_conv = conv_mixer(p_conv, conv_dw_w[l], conv_dw_b[l], conv_gn_g[l], conv_gn_b[l])
        o_rwkv = rwkv_mixer(p_rwkv, rwkv_mu[l], rwkv_w0[l], rwkv_w_up[l], rwkv_a0[l], rwkv_a_up[l],
                            rwkv_g_up[l], rwkv_k_k[l], rwkv_k_a[l], rwkv_r_k[l], rwkv_ln_g[l], rwkv_ln_b[l])
        o_nsa = nsa_mixer(p_q, p_kv, p_gate, nsa_ck_pos[l], nsa_ck_w1[l], nsa_ck_w2[l],
                          nsa_cv_pos[l], nsa_cv_w1[l], nsa_cv_w2[l])
        wo = w_out[l].astype(BF16)
        wo_pieces = [wo[:d_conv], wo[d_conv:d_conv + d_rwkv], wo[d_conv + d_rwkv:]]
        x1, h2 = proj_out(x, g1, sh2, sc2, norm2_g[l], [o_conv, o_rwkv, o_nsa], wo_pieces)
        x = peer_layer(x1, h2, peer_wq[l].astype(BF16), peer_k1[l], peer_k2[l],
                       peer_u[l].astype(BF16), peer_v[l].astype(BF16), g2,
                       final_g if l == depth - 1 else None)
    return x
```

```python
import functools

import jax
import jax.numpy as jnp
import numpy as np
from jax import lax
from jax.experimental import pallas as pl
from jax.experimental.pallas import tpu as pltpu

F32 = jnp.float32
BF16 = jnp.bfloat16
HIGHEST = lax.Precision.HIGHEST

HEAD_DIM = 64
CONV_WIDTH = 31
CONV_EPS = 1e-5
RWKV_GN_EPS = 64e-5
RMS_EPS = 1e-6
LORA_W = 64
LORA_A = 64
LORA_G = 128
NSA_KV_HEADS = 2
NSA_GROUP = 4
CMP_BLOCK = 32
CMP_STRIDE = 16
CMP_HIDDEN = 128
SEL_BLOCK = 64
N_SEL = 16
WINDOW = 512
FORCE_SCORE = 1e4
NEG_INF = -1e30
LOG2E = 1.4426950408889634
PEER_HEADS = 8
PEER_KEYS = 128
PEER_TOPK = 16

LANES = 128
VMEM_LIMIT = 56 * 1024 * 1024


def _dot_hi(a, b):
    return jnp.dot(a, b, precision=HIGHEST, preferred_element_type=F32)


def _dot_bf(a, b):
    return jnp.dot(a.astype(BF16), b.astype(BF16), preferred_element_type=F32)


def _cparams(*sem):
    return pltpu.CompilerParams(dimension_semantics=sem, vmem_limit_bytes=VMEM_LIMIT)


def _group_avg_matrix(n, group):
    i = np.arange(n)
    return jnp.asarray((i[:, None] // group == i[None, :] // group).astype(np.float32) / group)


def _ada_kernel(c_ref, w_ref, b_ref, o_ref):
    c = c_ref[...]
    cs = c * jax.nn.sigmoid(c)
    o_ref[0] = _dot_hi(cs, w_ref[0]) + b_ref[0]


def ada_mod(c, ada_w, ada_b):
    L, D, N = ada_w.shape
    B = c.shape[0]
    bp = 8
    cp = jnp.zeros((bp, D), F32).at[:B].set(c)
    tn = N // 4
    out = pl.pallas_call(
        _ada_kernel,
        grid=(L, N // tn),
        in_specs=[pl.BlockSpec((bp, D), lambda l, j: (0, 0)),
                  pl.BlockSpec((1, D, tn), lambda l, j: (l, 0, j)),
                  pl.BlockSpec((1, 1, tn), lambda l, j: (l, 0, j))],
        out_specs=pl.BlockSpec((1, bp, tn), lambda l, j: (l, 0, j)),
        out_shape=jax.ShapeDtypeStruct((L, bp, N), F32),
        compiler_params=_cparams("parallel", "parallel"),
        name="ada_mod",
    )(cp, ada_w, ada_b.reshape(L, 1, N))
    return out[:, :B]


def _modulated_norm(x, g, sc, sh):
    ms = jnp.mean(x * x, axis=-1, keepdims=True)
    return x * lax.rsqrt(ms + RMS_EPS) * g * (1.0 + sc) + sh


def _proj_in_kernel(x_ref, sh_ref, sc_ref, g_ref, *refs):
    n = len(refs) // 2
    h = _modulated_norm(x_ref[0], g_ref[...], sc_ref[0], sh_ref[0]).astype(BF16)
    for w_ref, o_ref in zip(refs[:n], refs[n:]):
        o_ref[0] = jnp.dot(h, w_ref[...], preferred_element_type=F32)


def proj_in(x, sh, sc, g, weights, tm=512):
    B, T, D = x.shape
    tm = min(tm, T)
    vec = pl.BlockSpec((1, 1, D), lambda b, i: (b, 0, 0))
    in_specs = [pl.BlockSpec((1, tm, D), lambda b, i: (b, i, 0)), vec, vec,
                pl.BlockSpec((1, D), lambda b, i: (0, 0))]
    in_specs += [pl.BlockSpec(w.shape, lambda b, i: (0, 0)) for w in weights]
    out_specs = [pl.BlockSpec((1, tm, w.shape[1]), lambda b, i: (b, i, 0)) for w in weights]
    out_shape = [jax.ShapeDtypeStruct((B, T, w.shape[1]), F32) for w in weights]
    return pl.pallas_call(
        _proj_in_kernel,
        grid=(B, T // tm),
        in_specs=in_specs, out_specs=out_specs, out_shape=out_shape,
        compiler_params=_cparams("parallel", "parallel"),
        name="proj_in",
    )(x, sh.reshape(B, 1, D), sc.reshape(B, 1, D), g.reshape(1, D), *weights)


CONV_HALO = 32


def _conv_kernel(p_ref, w_ref, b_ref, gg_ref, gb_ref, m_ref, o_ref, ext_ref):
    i = pl.program_id(1)
    tt, dc = o_ref.shape[1], o_ref.shape[2]

    @pl.when(i == 0)
    def _():
        ext_ref[0:CONV_HALO, :] = jnp.zeros((CONV_HALO, dc), F32)

    @pl.when(i > 0)
    def _():
        ext_ref[0:CONV_HALO, :] = ext_ref[tt:tt + CONV_HALO, :]

    p = p_ref[0]
    ext_ref[CONV_HALO:CONV_HALO + tt, :] = p[:, :dc] * jax.nn.sigmoid(p[:, dc:])
    off = CONV_HALO - (CONV_WIDTH - 1)
    acc = jnp.zeros((tt, dc), F32) + b_ref[...]
    for j in range(CONV_WIDTH):
        acc = acc + ext_ref[off + j:off + j + tt, :] * w_ref[j:j + 1, :]
    m = m_ref[...]
    mu = _dot_hi(acc, m)
    d = acc - mu
    var = _dot_hi(d * d, m)
    y = d * lax.rsqrt(var + CONV_EPS) * gg_ref[...] + gb_ref[...]
    o_ref[0] = y * jax.nn.sigmoid(y)


def conv_mixer(p, dw_w, dw_b, gn_g, gn_b, tt=512):
    B, T, two_dc = p.shape
    dc = two_dc // 2
    tt = min(tt, T)
    wpad = jnp.zeros((32, dc), F32).at[:CONV_WIDTH].set(dw_w.reshape(CONV_WIDTH, dc))
    row = pl.BlockSpec((1, dc), lambda b, i: (0, 0))
    return pl.pallas_call(
        _conv_kernel,
        grid=(B, T // tt),
        in_specs=[pl.BlockSpec((1, tt, two_dc), lambda b, i: (b, i, 0)),
                  pl.BlockSpec((32, dc), lambda b, i: (0, 0)), row, row, row,
                  pl.BlockSpec((dc, dc), lambda b, i: (0, 0))],
        out_specs=pl.BlockSpec((1, tt, dc), lambda b, i: (b, i, 0)),
        out_shape=jax.ShapeDtypeStruct((B, T, dc), F32),
        scratch_shapes=[pltpu.VMEM((tt + CONV_HALO, dc), F32)],
        compiler_params=_cparams("parallel", "arbitrary"),
        name="conv_mixer",
    )(p, wpad, dw_b.reshape(1, dc), gn_g.reshape(1, dc), gn_b.reshape(1, dc),
      _group_avg_matrix(dc, HEAD_DIM))


SCAN_SUB = 64
SCAN_CHUNK = 128


def _group_sum_matrix(n, group):
    i = np.arange(n)
    return jnp.asarray((i[:, None] // group == i[None, :] // group).astype(np.float32))


def _rwkv_pre_kernel(p_ref, halo_ref, mu_ref, w0_ref, wup_ref, a0_ref, aup_ref, gup_ref, kk_ref, ka_ref,
                     rk_ref, ones_ref, tri_ref, w_o, kk_o, b_o, k_o, v_o, r_o, bonus_o, g_o, ext_ref):
    i = pl.program_id(1)
    tt = p_ref.shape[1]
    dr = w_o.shape[2]
    p = p_ref[0]
    first = (i > 0).astype(F32)
    ext_ref[0:8, :] = halo_ref[0] * first
    ext_ref[8:8 + tt, :] = p
    prev = ext_ref[7:7 + tt, :]
    xs = p + (prev - p) * mu_ref[...]
    r = xs[:, 0:dr]
    k = xs[:, dr:2 * dr]
    v = xs[:, 2 * dr:3 * dr]
    o = 3 * dr
    xw = xs[:, o:o + LORA_W]
    xa = xs[:, o + LORA_W:o + LORA_W + LORA_A]
    xg = xs[:, o + LORA_W + LORA_A:o + LORA_W + LORA_A + LORA_G]
    z = w0_ref[...] + _dot_hi(jnp.tanh(xw), wup_ref[...])
    w_log = -jax.nn.softplus(-z) - 0.5
    log_decay = -jnp.exp(w_log)
    cum = _dot_hi(tri_ref[...], log_decay)
    p_incl = jnp.exp(cum)
    p_excl = jnp.exp(cum - log_decay)
    p_inv = jnp.exp(-cum)
    a = jax.nn.sigmoid(a0_ref[...] + _dot_hi(xa, aup_ref[...]))
    g = _dot_hi(jax.nn.sigmoid(xg), gup_ref[...])
    ones = ones_ref[...]
    kk = k * kk_ref[...]
    nrm = jnp.sqrt(_dot_hi(kk * kk, ones))
    kk = kk / jnp.maximum(nrm, 1e-12)
    k_eff = k * (1.0 + (a - 1.0) * ka_ref[...])
    bonus = _dot_hi(r * k_eff * rk_ref[...], ones) * v
    w_o[0] = p_incl
    kk_o[0] = kk * p_excl
    b_o[0] = kk * a * p_inv
    k_o[0] = k_eff * p_inv
    v_o[0] = v
    r_o[0] = r * p_incl
    bonus_o[0] = bonus
    g_o[0] = g


def rwkv_pre(p, mu, w0, w_up, a0, a_up, g_up, k_k, k_a, r_k, tt=512):
    B, T, n_in = p.shape
    dr = w0.shape[0]
    tt = min(tt, T)
    nb8 = tt // 8
    full = lambda a: pl.BlockSpec(a.shape, lambda b, i: (0,) * a.ndim)
    blk_id = np.arange(tt) // SCAN_SUB
    tri = jnp.asarray(((blk_id[:, None] == blk_id[None, :])
                       & (np.arange(tt)[:, None] >= np.arange(tt)[None, :])).astype(np.float32))
    args = [mu.reshape(1, n_in), w0.reshape(1, dr), w_up, a0.reshape(1, dr), a_up, g_up,
            k_k.reshape(1, dr), k_a.reshape(1, dr), r_k.reshape(1, dr), _group_sum_matrix(dr, HEAD_DIM), tri]
    out = jax.ShapeDtypeStruct((B, T, dr), F32)
    ospec = pl.BlockSpec((1, tt, dr), lambda b, i: (b, i, 0))
    return pl.pallas_call(
        _rwkv_pre_kernel,
        grid=(B, T // tt),
        in_specs=[pl.BlockSpec((1, tt, n_in), lambda b, i: (b, i, 0)),
                  pl.BlockSpec((1, 8, n_in), lambda b, i: (b, jnp.maximum(i * nb8 - 1, 0), 0))]
                 + [full(a) for a in args],
        out_specs=[ospec] * 8, out_shape=[out] * 8,
        scratch_shapes=[pltpu.VMEM((tt + 8, n_in), F32)],
        compiler_params=_cparams("parallel", "parallel"),
        name="rwkv_pre",
    )(p, p, *args)


def _scan_select_matrices():
    e = np.zeros((SCAN_SUB // 2, 4 * SCAN_SUB, 2 * LANES), np.float32)
    for t in range(SCAN_SUB):
        c = (t % 2) * LANES
        for part in range(2):
            e[t // 2, 2 * part * SCAN_SUB + t, c:c + HEAD_DIM] = 1.0
            e[t // 2, (2 * part + 1) * SCAN_SUB + t, c + HEAD_DIM:c + LANES] = 1.0
    return jnp.asarray(e, BF16)


def _rwkv_scan_kernel(pc_ref, kk_ref, b_ref, k_ref, r_ref, v_ref, bonus_ref, g_ref, lng_ref, lnb_ref,
                      e_ref, avg_ref, o_ref, st_ref, lhs_ref, y_ref):
    nb, tc, dr = v_ref.shape
    n_half = dr // LANES
    n_pair = nb * n_half
    n_sub = tc // SCAN_SUB
    quantities = (kk_ref, b_ref, k_ref, r_ref)

    @pl.when(pl.program_id(0) == 0)
    def _():
        st_ref[...] = jnp.zeros(st_ref.shape, F32)

    for qi, q_ref in enumerate(quantities):
        for p in range(n_pair):
            b, hb = divmod(p, n_half)
            xt = q_ref[b, :, hb * LANES:(hb + 1) * LANES].T
            for s in range(n_sub):
                cat = jnp.concatenate([xt[0:HEAD_DIM, s * SCAN_SUB:(s + 1) * SCAN_SUB],
                                       xt[HEAD_DIM:, s * SCAN_SUB:(s + 1) * SCAN_SUB]], axis=1)
                hi = cat.astype(BF16)
                lo = (cat - hi.astype(F32)).astype(BF16)
                row = (qi * n_pair + p) * HEAD_DIM
                lhs_ref[s, row:row + HEAD_DIM, :] = jnp.concatenate([hi, lo], axis=1)

    sub_iota = lax.broadcasted_iota(jnp.int32, (8, LANES), 0)
    ri = lax.broadcasted_iota(jnp.int32, (HEAD_DIM, LANES), 0)
    ci = lax.broadcasted_iota(jnp.int32, (HEAD_DIM, LANES), 1)
    diag2 = (ci % HEAD_DIM == ri).astype(F32)
    rj = lax.broadcasted_iota(jnp.int32, (LANES, LANES), 0)
    cj = lax.broadcasted_iota(jnp.int32, (LANES, LANES), 1)
    half_ones = (rj // HEAD_DIM == cj // HEAD_DIM).astype(F32)
    for s in range(n_sub):
        def steps8(t8, carry, s=s):
            base = pl.multiple_of(s * SCAN_SUB + t8 * 8, 8)
            y8 = [jnp.zeros((8, LANES), F32) for _ in range(n_pair)]
            for j in range(8):
                if j % 2 == 0:
                    z2 = jnp.dot(lhs_ref[s], e_ref[t8 * 4 + j // 2], preferred_element_type=F32)
                z = z2[:, (j % 2) * LANES:(j % 2 + 1) * LANES]
                for p in range(n_pair):
                    b, hb = divmod(p, n_half)
                    col = lambda qi: z[(qi * n_pair + p) * HEAD_DIM:(qi * n_pair + p + 1) * HEAD_DIM, :]
                    st = st_ref[p]
                    sa = jnp.sum(st * col(0), axis=0, keepdims=True)
                    v_row = v_ref[b, pl.ds(base, 8), hb * LANES:(hb + 1) * LANES][j:j + 1, :]
                    st = st - col(1) * sa + col(2) * v_row
                    y = jnp.sum(st * col(3), axis=0, keepdims=True)
                    y8[p] = jnp.where(sub_iota == j, y, y8[p])
                    st_ref[p] = st
            for p in range(n_pair):
                b, hb = divmod(p, n_half)
                y_ref[b, pl.ds(base, 8), hb * LANES:(hb + 1) * LANES] = y8[p]
            return carry
        lax.fori_loop(0, SCAN_SUB // 8, steps8, 0)
        last = s * SCAN_SUB + SCAN_SUB - 1
        for p in range(n_pair):
            b, hb = divmod(p, n_half)
            p_row = pc_ref[b, last - 7:last + 1, hb * LANES:(hb + 1) * LANES][7:8, :]
            st_ref[p] = st_ref[p] * _dot_hi(diag2 * p_row, half_ones)

    avg = avg_ref[...]
    for b in range(nb):
        y = y_ref[b]
        mu = _dot_hi(y, avg)
        d = y - mu
        var = _dot_hi(d * d, avg)
        yn = d * lax.rsqrt(var + RWKV_GN_EPS) * lng_ref[...] + lnb_ref[...]
        o_ref[b] = (yn + bonus_ref[b]) * g_ref[b]


def rwkv_scan(w, kk, bq, k, r, v, bonus, g, ln_g, ln_b):
    B, T, dr = v.shape
    tc = min(SCAN_CHUNK, T)
    n_pair = B * dr // LANES
    blk = pl.BlockSpec((B, tc, dr), lambda i: (0, i, 0))
    row = pl.BlockSpec((1, dr), lambda i: (0, 0))
    e = _scan_select_matrices()
    return pl.pallas_call(
        _rwkv_scan_kernel,
        grid=(T // tc,),
        in_specs=[blk] * 8 + [row, row, pl.BlockSpec(e.shape, lambda i: (0, 0, 0)),
                              pl.BlockSpec((dr, dr), lambda i: (0, 0))],
        out_specs=blk,
        out_shape=jax.ShapeDtypeStruct((B, T, dr), F32),
        scratch_shapes=[pltpu.VMEM((n_pair, HEAD_DIM, LANES), F32),
                        pltpu.VMEM((tc // SCAN_SUB, 4 * n_pair * HEAD_DIM, 4 * SCAN_SUB), BF16),
                        pltpu.VMEM((B, tc, dr), F32)],
        compiler_params=_cparams("arbitrary"),
        name="rwkv_scan",
    )(w, kk, bq, k, r, v, bonus, g, ln_g.reshape(1, dr), ln_b.reshape(1, dr), e, _group_avg_matrix(dr, HEAD_DIM))


def rwkv_mixer(p, mu, w0, w_up, a0, a_up, g_up, k_k, k_a, r_k, ln_g, ln_b):
    w, kk, bq, k, v, r, bonus, g = rwkv_pre(p, mu, w0, w_up, a0, a_up, g_up, k_k, k_a, r_k.reshape(-1))
    return rwkv_scan(w, kk, bq, k, r, v, bonus, g, ln_g, ln_b)


def _dot_nt(a, b, **kw):
    return lax.dot_general(a, b, (((1,), (1,)), ((), ())), preferred_element_type=F32, **kw)


def _compress_kernel(kz_ref, vz_ref, kpos_ref, kw1_ref, kw2_ref, vpos_ref, vw1_ref, vw2_ref,
                     kc_ref, vc_ref, shift_ref):
    n = kz_ref.shape[2]
    half = kz_ref.shape[3]

    def one(z_ref, pos_ref, w1_ref, w2_ref, o_ref):
        z = z_ref[0, 0]
        top = _dot_bf(z + pos_ref[0:1, :], w1_ref[0:half, :])
        bot = _dot_bf(z + pos_ref[1:2, :], w1_ref[half:2 * half, :])
        shift_ref[0:n, :] = bot
        shift_ref[n:n + 8, :] = jnp.zeros((8, bot.shape[1]), F32)
        pre = top + shift_ref[1:n + 1, :]
        hid = pre * jax.nn.sigmoid(pre)
        o_ref[0, 0] = _dot_bf(hid, w2_ref[...])

    one(kz_ref, kpos_ref, kw1_ref, kw2_ref, kc_ref)
    one(vz_ref, vpos_ref, vw1_ref, vw2_ref, vc_ref)


def nsa_compress(k_cmp, v_cmp, ck_pos, ck_w1, ck_w2, cv_pos, cv_w1, cv_w2):
    B, G, T, dk = k_cmp.shape
    n = T // CMP_STRIDE
    half = CMP_STRIDE * dk
    zspec = pl.BlockSpec((1, 1, n, half), lambda b, g: (b, g, 0, 0))
    full = lambda a: pl.BlockSpec(a.shape, lambda b, g: (0,) * a.ndim)
    ospec = pl.BlockSpec((1, 1, n, dk), lambda b, g: (b, g, 0, 0))
    args = [ck_pos.reshape(2, half), ck_w1.astype(BF16), ck_w2.astype(BF16),
            cv_pos.reshape(2, half), cv_w1.astype(BF16), cv_w2.astype(BF16)]
    return pl.pallas_call(
        _compress_kernel,
        grid=(B, G),
        in_specs=[zspec, zspec] + [full(a) for a in args],
        out_specs=[ospec, ospec],
        out_shape=[jax.ShapeDtypeStruct((B, G, n, dk), F32)] * 2,
        scratch_shapes=[pltpu.VMEM((n + 8, CMP_HIDDEN), F32)],
        compiler_params=_cparams("parallel", "parallel"),
        name="nsa_compress",
    )(k_cmp.reshape(B, G, n, half), v_cmp.reshape(B, G, n, half), *args)


NSA_TQ = 256
NSA_TK = 1024


def _split_bf16(a):
    hi = a.astype(BF16)
    return hi, (a - hi.astype(F32)).astype(BF16)


def _dot_3pass(a, b):
    ah, al = _split_bf16(a)
    bh, bl = _split_bf16(b)
    d = lambda x, y: jnp.dot(x, y, preferred_element_type=F32)
    return d(ah, bh) + d(ah, bl) + d(al, bh)


def _nsa_kernel(q_ref, gate_ref, kc_ref, vct_ref, ks_ref, vst_ref, kw_ref, vwt_ref, ovt_ref,
                o_ref, m_ref, acc_ref, ow_ref, sel_ref, s0_ref, s1_ref, p_ref):
    qi = pl.program_id(2)
    dk = q_ref.shape[3]
    tq = gate_ref.shape[3]
    R = q_ref.shape[4] // tq
    n_c = kc_ref.shape[2]
    n_blk = ovt_ref.shape[0]
    T = ks_ref.shape[2]
    tk = s0_ref.shape[0]
    t0 = qi * tq
    qt = q_ref[0, 0, 0]
    qt_bf = qt.astype(BF16)
    t_row = t0 + lax.broadcasted_iota(jnp.int32, (1, tq), 1)
    lanes = lambda r: slice(r * tq, (r + 1) * tq)

    s = _dot_3pass(kc_ref[0, 0], qt)
    c_end = lax.broadcasted_iota(jnp.int32, (n_c, tq), 0) * CMP_STRIDE + (CMP_BLOCK - 1)
    valid_c = c_end <= t_row
    any_c = (t_row >= CMP_BLOCK - 1).astype(F32)
    p_sum = jnp.zeros((n_c, tq), F32)
    ps = []
    for r in range(R):
        s_r = jnp.where(valid_c, s[:, lanes(r)], NEG_INF)
        e = jnp.exp2(s_r - jnp.max(s_r, axis=0, keepdims=True))
        p_r = e * (any_c / jnp.sum(e, axis=0, keepdims=True))
        p_sum = p_sum + p_r
        ps.append(p_r.astype(BF16))
    o_c = jnp.dot(vct_ref[0, 0].astype(BF16), jnp.concatenate(ps, axis=1), preferred_element_type=F32)

    imp = _dot_hi(ovt_ref[...], p_sum)
    blk = lax.broadcasted_iota(jnp.int32, (n_blk, tq), 0).astype(F32)
    cur = (t_row // SEL_BLOCK).astype(F32)
    forced = (blk == 0.0) | (blk == cur) | (blk == cur - 1.0)
    x = jnp.where(forced, FORCE_SCORE, jnp.where(blk <= cur, imp, -1.0))
    x = jnp.where(blk < float(T // SEL_BLOCK), x, -3e38)
    sel = jnp.zeros((n_blk, tq), F32)
    for _ in range(N_SEL):
        mx = jnp.max(x, axis=0, keepdims=True)
        idx = jnp.min(jnp.where(x == mx, blk, float(n_blk)), axis=0, keepdims=True)
        hit = blk == idx
        sel = jnp.where(hit, 1.0, sel)
        x = jnp.where(hit, -jnp.inf, x)

    span = WINDOW + tq
    w0 = pl.multiple_of(jnp.maximum(t0 - WINDOW, 0), tq)
    s = jnp.dot(kw_ref[0, 0, pl.ds(w0, span), :], qt_bf, preferred_element_type=F32)
    dist = t_row - (w0 + lax.broadcasted_iota(jnp.int32, (span, tq), 0))
    bias_w = jnp.where((dist >= 0) & (dist < WINDOW), 0.0, NEG_INF)
    ps = []
    for r in range(R):
        s_r = s[:, lanes(r)] + bias_w
        ps.append(jnp.exp2(s_r - jnp.max(s_r, axis=0, keepdims=True)).astype(BF16))
    ow_ref[...] = jnp.dot(vwt_ref[0, 0, :, pl.ds(w0, span)], jnp.concatenate(ps, axis=1),
                          preferred_element_type=F32)

    bpt = tk // SEL_BLOCK
    n_kt = T // tk
    n_full = (t0 + tq - 1) // tk
    sel_ref[...] = jnp.where(sel > 0.5, 0.0, NEG_INF)
    m_ref[...] = jnp.full(m_ref.shape, NEG_INF, F32)
    acc_ref[...] = jnp.zeros(acc_ref.shape, F32)
    q_pad = jnp.zeros((ks_ref.shape[3] - dk - bpt, R * tq), BF16)

    def scores(kt):
        kt = jnp.minimum(kt, n_kt - 1)
        rows = sel_ref[pl.ds(pl.multiple_of(kt * bpt, bpt), bpt), :].astype(BF16)
        q_aug = jnp.concatenate([qt_bf, jnp.concatenate([rows] * R, axis=1), q_pad], axis=0)
        off = pl.multiple_of(kt * tk, tk)
        return jnp.dot(ks_ref[0, 0, pl.ds(off, tk), :], q_aug, preferred_element_type=F32)

    def softmax_tile(s_ref, kt, causal):
        if causal:
            key_pos = kt * tk + lax.broadcasted_iota(jnp.int32, (tk, tq), 0)
            bias = jnp.where(key_pos <= t_row, 0.0, NEG_INF)
        for r in range(R):
            m_old = m_ref[:, lanes(r)]
            if causal:
                m_new = jnp.maximum(m_old, jnp.max(s_ref[:, lanes(r)] + bias, axis=0, keepdims=True))
                p_r = jnp.exp2((s_ref[:, lanes(r)] - m_new) + bias)
            else:
                m_new = jnp.maximum(m_old, jnp.max(s_ref[:, lanes(r)], axis=0, keepdims=True))
                p_r = jnp.exp2(s_ref[:, lanes(r)] - m_new)
            m_ref[:, lanes(r)] = m_new
            p_ref[:, lanes(r)] = p_r.astype(BF16)
            acc_ref[:, lanes(r)] = acc_ref[:, lanes(r)] * jnp.exp2(m_old - m_new)
        off = pl.multiple_of(kt * tk, tk)
        acc_ref[...] += jnp.dot(vst_ref[0, 0, :, pl.ds(off, tk)], p_ref[...], preferred_element_type=F32)

    s0_ref[...] = scores(0)

    def tile_pair(i, carry):
        s1_ref[...] = scores(2 * i + 1)
        softmax_tile(s0_ref, 2 * i, False)
        s0_ref[...] = scores(2 * i + 2)
        softmax_tile(s1_ref, 2 * i + 1, False)
        return carry

    lax.fori_loop(0, n_full // 2, tile_pair, 0)
    odd = n_full % 2 == 1

    @pl.when(odd)
    def _():
        s1_ref[...] = scores(n_full)
        softmax_tile(s0_ref, n_full - 1, False)
        softmax_tile(s1_ref, n_full, True)

    @pl.when(jnp.logical_not(odd))
    def _():
        softmax_tile(s0_ref, n_full, True)

    acc = acc_ref[...]
    o_s = acc[0:dk] * (1.0 / acc[dk:dk + 1])
    o_w = ow_ref[...]
    o_w = o_w[0:dk] * (1.0 / o_w[dk:dk + 1])

    gate = jax.nn.sigmoid(gate_ref[0, 0])
    outs = []
    for r in range(R):
        outs.append(gate[r:r + 1] * o_c[:, lanes(r)] + gate[R + r:R + r + 1] * o_s[:, lanes(r)]
                    + gate[2 * R + r:2 * R + r + 1] * o_w[:, lanes(r)])
    o_ref[0] = jnp.concatenate(outs, axis=0).T


def _overlap_matrix_t(n_c, n_blk):
    c0 = np.arange(n_c)[None, :] * CMP_STRIDE
    s0 = np.arange(n_blk)[:, None] * SEL_BLOCK
    return jnp.asarray(((c0 < s0 + SEL_BLOCK) & (c0 + CMP_BLOCK > s0)).astype(np.float32))


def nsa_attention(qt, gate_t, kc, vct, k_sel, vt_sel, k_win, vt_win):
    B, G, nq, dk, L = qt.shape
    T = k_win.shape[2]
    tq = T // nq
    R = L // tq
    n_c = kc.shape[2]
    tk = min(NSA_TK, T)
    n_blk = max(T // SEL_BLOCK, LANES)
    ovt = _overlap_matrix_t(n_c, n_blk)
    res = lambda a: pl.BlockSpec((1, 1) + a.shape[2:], lambda b, g, i: (b, g, 0, 0))
    return pl.pallas_call(
        _nsa_kernel,
        grid=(B, G, nq),
        in_specs=[pl.BlockSpec((1, 1, 1, dk, L), lambda b, g, i: (b, g, i, 0, 0)),
                  pl.BlockSpec((1, 1, gate_t.shape[2], tq), lambda b, g, i: (b, g, 0, i)),
                  res(kc), res(vct), res(k_sel), res(vt_sel), res(k_win), res(vt_win),
                  pl.BlockSpec(ovt.shape, lambda b, g, i: (0, 0))],
        out_specs=pl.BlockSpec((1, tq, R * dk), lambda b, g, i: (b, i, g)),
        out_shape=jax.ShapeDtypeStruct((B, T, G * R * dk), F32),
        scratch_shapes=[pltpu.VMEM((1, L), F32), pltpu.VMEM((dk + 16, L), F32),
                        pltpu.VMEM((dk + 16, L), F32), pltpu.VMEM((n_blk, tq), F32),
                        pltpu.VMEM((tk, L), F32), pltpu.VMEM((tk, L), F32), pltpu.VMEM((tk, L), BF16)],
        compiler_params=_cparams("parallel", "parallel", "arbitrary"),
        name="nsa_attention",
    )(qt, gate_t, kc, vct, k_sel, vt_sel, k_win, vt_win, ovt)


def nsa_mixer(p_q, p_kv, p_gate, ck_pos, ck_w1, ck_w2, cv_pos, cv_w1, cv_w2):
    B, T, _ = p_q.shape
    G, R, dk = NSA_KV_HEADS, NSA_GROUP, HEAD_DIM
    tq = min(NSA_TQ, T)
    kv6 = p_kv.reshape(B, T, 6, G, dk)
    rows = lambda i: kv6[:, :, i].transpose(0, 2, 1, 3)
    cols = lambda i: kv6[:, :, i].transpose(0, 2, 3, 1)
    kc, vc = nsa_compress(rows(0), rows(1), ck_pos, ck_w1, ck_w2, cv_pos, cv_w1, cv_w2)
    qt = (p_q * (dk ** -0.5 * LOG2E)).reshape(B, T // tq, tq, G, R, dk).transpose(0, 3, 1, 5, 4, 2)
    qt = qt.reshape(B, G, T // tq, dk, R * tq)
    gate_t = p_gate[..., :3 * G * R].reshape(B, T, 3, G, R).transpose(0, 3, 2, 4, 1).reshape(B, G, 3 * R, T)
    gate_t = jnp.pad(gate_t, ((0, 0), (0, 0), (0, 16 - 3 * R), (0, 0)))
    tk = min(NSA_TK, T)
    blk_onehot = (jnp.arange(T)[:, None] // SEL_BLOCK % (tk // SEL_BLOCK) == jnp.arange(dk)[None, :]).astype(BF16)
    k_sel = jnp.concatenate([rows(2).astype(BF16), jnp.broadcast_to(blk_onehot, (B, G, T, dk))], axis=-1)
    ones_rows = jnp.zeros((B, G, 16, T), BF16).at[:, :, 0].set(1.0)
    with_ones = lambda vt: jnp.concatenate([vt.astype(BF16), ones_rows], axis=2)
    return nsa_attention(qt, gate_t, kc, vc.transpose(0, 1, 3, 2), k_sel, with_ones(cols(3)),
                         rows(4).astype(BF16), with_ones(cols(5)))


def _proj_out_kernel(x_ref, g1_ref, sh_ref, sc_ref, ng_ref, *refs):
    n = (len(refs) - 2) // 2
    x1_ref, h2_ref = refs[2 * n:]
    acc = None
    for m_ref, w_ref in zip(refs[:n], refs[n:2 * n]):
        d = jnp.dot(m_ref[0].astype(BF16), w_ref[...], preferred_element_type=F32)
        acc = d if acc is None else acc + d
    x1 = x_ref[0] + g1_ref[0] * acc
    x1_ref[0] = x1
    h2_ref[0] = _modulated_norm(x1, ng_ref[...], sc_ref[0], sh_ref[0]).astype(BF16)


def proj_out(x, g1, sh2, sc2, norm_g, mixes, weights, tm=512):
    B, T, D = x.shape
    tm = min(tm, T)
    vec = pl.BlockSpec((1, 1, D), lambda b, i: (b, 0, 0))
    tile = lambda n: pl.BlockSpec((1, tm, n), lambda b, i: (b, i, 0))
    return pl.pallas_call(
        _proj_out_kernel,
        grid=(B, T // tm),
        in_specs=[tile(D), vec, vec, vec, pl.BlockSpec((1, D), lambda b, i: (0, 0))]
                 + [tile(m.shape[2]) for m in mixes]
                 + [pl.BlockSpec(w.shape, lambda b, i: (0, 0)) for w in weights],
        out_specs=[tile(D), tile(D)],
        out_shape=[jax.ShapeDtypeStruct((B, T, D), F32), jax.ShapeDtypeStruct((B, T, D), BF16)],
        compiler_params=_cparams("parallel", "parallel"),
        name="proj_out",
    )(x, g1.reshape(B, 1, D), sh2.reshape(B, 1, D), sc2.reshape(B, 1, D), norm_g.reshape(1, D), *mixes, *weights)


_CAND_BLOCKS = [(0, 16), (1, 8), (None, 8), (2, 5), (3, 4), (4, 3), (5, 2), (6, 2), (7, 2)]
_CAND_PAD = 6


def _cand_constants(tm):
    flat, neg = [], []
    for i, rows in _CAND_BLOCKS:
        for r in range(rows):
            flat.append((8 + r) * PEER_TOPK if i is None else i * PEER_TOPK + r)
            neg.append(0.0)
    flat += [1e9] * _CAND_PAD
    neg += [-np.inf] * _CAND_PAD
    flat = np.tile(np.asarray(flat, np.float32)[:, None], (1, tm))
    neg = np.tile(np.asarray(neg, np.float32)[:, None], (1, tm))
    return jnp.asarray(flat), jnp.asarray(neg)


def _topk_rows(x, row_id, k, n_rows):
    vals, idxs = [], []
    for _ in range(k):
        mx = jnp.max(x, axis=0, keepdims=True)
        idx = jnp.min(jnp.where(x == mx, row_id, float(n_rows)), axis=0, keepdims=True)
        x = jnp.where(row_id == idx, -jnp.inf, x)
        vals.append(mx)
        idxs.append(idx)
    return jnp.concatenate(vals, axis=0), jnp.concatenate(idxs, axis=0)


def _sorting_network(n):
    pairs = []
    p = 1
    while p < n:
        k = p
        while k >= 1:
            for j in range(k % p, n - k, 2 * k):
                for i in range(min(k, n - j - k)):
                    if (i + j) // (2 * p) == (i + j + k) // (2 * p):
                        pairs.append((i + j, i + j + k))
            k //= 2
        p *= 2
    return pairs


def _topk_sorted_columns(x, k):
    n_rows, m = x.shape
    n_lvl = n_rows // 8
    slot = lax.broadcasted_iota(jnp.int32, (8, m), 0).astype(F32)
    s = [x[8 * v:8 * v + 8] for v in range(n_lvl)]
    p = [slot + 8.0 * v for v in range(n_lvl)]
    for i, j in _sorting_network(n_lvl):
        swap = s[j] > s[i]
        s[i], s[j] = jnp.where(swap, s[j], s[i]), jnp.where(swap, s[i], s[j])
        p[i], p[j] = jnp.where(swap, p[j], p[i]), jnp.where(swap, p[i], p[j])
    tie = jnp.zeros((8, m), F32)
    for d in range(n_lvl - 1):
        tie = jnp.where(s[d] == s[d + 1], 1.0, tie)
    tie = jnp.max(tie, axis=0, keepdims=True)
    s.append(jnp.full((8, m), -jnp.inf, F32))
    p.append(jnp.zeros((8, m), F32))
    vals, idxs = [], []
    for r in range(k):
        mx = jnp.max(s[0], axis=0, keepdims=True)
        hit = s[0] == mx
        cnt = jnp.sum(jnp.where(hit, 1.0, 0.0), axis=0, keepdims=True)
        tie = jnp.where(cnt > 1.5, 1.0, tie)
        vals.append(mx)
        idxs.append(jnp.sum(jnp.where(hit, p[0], 0.0), axis=0, keepdims=True))
        for d in range(n_lvl - r):
            s[d] = jnp.where(hit, s[d + 1], s[d])
            p[d] = jnp.where(hit, p[d + 1], p[d])
    tie = jnp.where(jnp.max(s[0], axis=0, keepdims=True) == vals[-1], 1.0, tie)
    return jnp.concatenate(vals, axis=0), jnp.concatenate(idxs, axis=0), tie


def _peer_select_kernel(h_ref, wq_ref, k1_ref, k2_ref, flat_ref, neg_ref, a_ref, b_ref, g_ref):
    tm = h_ref.shape[0]
    nk = k1_ref.shape[0]
    half = k1_ref.shape[1]
    q = jnp.dot(h_ref[...], wq_ref[...], preferred_element_type=F32)
    flat = flat_ref[...]
    neg = neg_ref[...]
    k1 = k1_ref[...]
    k2 = k2_ref[...]

    def route(topk):
        a_all, b_all, g_all = [], [], []
        tie = jnp.zeros((1, tm), F32)
        for h in range(PEER_HEADS):
            q1 = q[:, (2 * h) * half:(2 * h + 1) * half]
            q2 = q[:, (2 * h + 1) * half:(2 * h + 2) * half]
            v1, i1, t1 = topk(_dot_nt(k1, q1, precision=HIGHEST))
            v2, i2, t2 = topk(_dot_nt(k2, q2, precision=HIGHEST))
            tie = jnp.maximum(tie, jnp.maximum(t1, t2))
            vals, ai, bj = [], [], []
            for i, rows in _CAND_BLOCKS:
                if i is None:
                    vals.append(v1[8:16] + v2[0:1])
                    ai.append(i1[8:16])
                    bj.append(jnp.broadcast_to(i2[0:1], (8, tm)))
                else:
                    vals.append(v1[i:i + 1] + v2[0:rows])
                    ai.append(jnp.broadcast_to(i1[i:i + 1], (rows, tm)))
                    bj.append(i2[0:rows])
            pad = [jnp.zeros((_CAND_PAD, tm), F32)]
            cand = jnp.concatenate(vals + pad, axis=0) + neg
            ai = jnp.concatenate(ai + pad, axis=0)
            bj = jnp.concatenate(bj + pad, axis=0)
            sc, sa, sb = [], [], []
            for _ in range(PEER_TOPK):
                mx = jnp.max(cand, axis=0, keepdims=True)
                fid = jnp.min(jnp.where(cand == mx, flat, 1e9), axis=0, keepdims=True)
                hit = flat == fid
                sc.append(mx)
                sa.append(jnp.sum(jnp.where(hit, ai, 0.0), axis=0, keepdims=True))
                sb.append(jnp.sum(jnp.where(hit, bj, 0.0), axis=0, keepdims=True))
                cand = jnp.where(hit, -jnp.inf, cand)
            sc = jnp.concatenate(sc, axis=0)
            e = jnp.exp(sc - sc[0:1])
            g_all.append(e / jnp.sum(e, axis=0, keepdims=True))
            a_all.append(jnp.concatenate(sa, axis=0))
            b_all.append(jnp.concatenate(sb, axis=0))
        a_ref[...] = jnp.concatenate(a_all, axis=0).T
        b_ref[...] = jnp.concatenate(b_all, axis=0).T
        g_ref[...] = jnp.concatenate(g_all, axis=0).T
        return tie

    tie = route(functools.partial(_topk_sorted_columns, k=PEER_TOPK))

    @pl.when(jnp.max(tie) > 0.0)
    def _():
        key_id = lax.broadcasted_iota(jnp.int32, (nk, tm), 0).astype(F32)
        route(lambda s: _topk_rows(s, key_id, PEER_TOPK, nk) + (jnp.zeros((1, tm), F32),))


def peer_select(h2, wq, k1, k2, tm=256):
    N, D = h2.shape
    tm = min(tm, N)
    flat, neg = _cand_constants(tm)
    full = lambda a: pl.BlockSpec(a.shape, lambda i: (0,) * a.ndim)
    out = jax.ShapeDtypeStruct((N, PEER_HEADS * PEER_TOPK), F32)
    ospec = pl.BlockSpec((tm, PEER_HEADS * PEER_TOPK), lambda i: (i, 0))
    return pl.pallas_call(
        _peer_select_kernel,
        grid=(N // tm,),
        in_specs=[pl.BlockSpec((tm, D), lambda i: (i, 0)), full(wq), full(k1), full(k2), full(flat), full(neg)],
        out_specs=[ospec] * 3, out_shape=[out] * 3,
        compiler_params=_cparams("parallel"),
        name="peer_select",
    )(h2, wq, k1, k2, flat, neg)


PEER_BUILD_GROUP = 16
PEER_BUILD_UNROLL = 2


def _gelu(x):
    return 0.5 * x * (1.0 + lax.erf(x * 0.7071067811865476))


def _peer_expert_kernel(*refs, final, n_e):
    if final:
        h_ref, a_ref, b_ref, g_ref, u_ref, v_ref, x_ref, g2_ref, fg_ref, o_ref, w3_ref, acc_ref, ha_ref, hb_ref = refs
    else:
        h_ref, a_ref, b_ref, g_ref, u_ref, v_ref, x_ref, g2_ref, o_ref, w3_ref, acc_ref, ha_ref, hb_ref = refs
    j = pl.program_id(1)
    tm = h_ref.shape[0]
    te = u_ref.shape[0]
    nk = PEER_KEYS
    n_a = te // nk

    def hidden():
        return _dot_nt(h_ref[...], u_ref[...])

    def finish(hid_ref, tile):
        acts = []
        for al in range(n_a):
            w_a = w3_ref[tile * n_a + al].astype(F32)
            acts.append((_gelu(hid_ref[:, al * nk:(al + 1) * nk]) * w_a).astype(BF16))
        acc_ref[...] += jnp.dot(jnp.concatenate(acts, axis=1), v_ref[...], preferred_element_type=F32)

    @pl.when(j == 0)
    def _():
        acc_ref[...] = jnp.zeros(acc_ref.shape, F32)
        ha_ref[...] = hidden()
        key_id = lax.broadcasted_iota(jnp.int32, (nk, a_ref.shape[1]), 0).astype(F32)

        def tokens(tb, carry):
            for grp in range(PEER_BUILD_UNROLL):
                base = pl.multiple_of((tb * PEER_BUILD_UNROLL + grp) * PEER_BUILD_GROUP, PEER_BUILD_GROUP)
                a8 = a_ref[pl.ds(base, PEER_BUILD_GROUP), :]
                b8 = b_ref[pl.ds(base, PEER_BUILD_GROUP), :]
                g8 = g_ref[pl.ds(base, PEER_BUILD_GROUP), :]
                ws = []
                for r in range(PEER_BUILD_GROUP):
                    g_row = g8[r:r + 1]
                    g_hi = g_row.astype(BF16).astype(F32)
                    g_lo = g_row - g_hi
                    eq_a = key_id == a8[r:r + 1]
                    x = jnp.concatenate([jnp.where(eq_a, g_hi, 0.0).astype(BF16),
                                         jnp.where(eq_a, g_lo, 0.0).astype(BF16)], axis=1)
                    y1 = jnp.where(key_id == b8[r:r + 1], 1.0, 0.0).astype(BF16)
                    y = jnp.concatenate([y1, y1], axis=1)
                    ws.append(_dot_nt(x, y))
                w3_ref[:, pl.ds(base, PEER_BUILD_GROUP), :] = jnp.swapaxes(
                    jnp.stack(ws, axis=0).astype(BF16), 0, 1)
            return carry
        lax.fori_loop(0, tm // (PEER_BUILD_GROUP * PEER_BUILD_UNROLL), tokens, 0)

    for parity, (h_write, h_read) in enumerate(((ha_ref, hb_ref), (hb_ref, ha_ref))):
        @pl.when((j > 0) & (j < n_e) & (j % 2 == parity))
        def _(h_write=h_write, h_read=h_read):
            h_write[...] = hidden()
            finish(h_read, j - 1)

    @pl.when(j == n_e)
    def _():
        finish(hb_ref if (n_e - 1) % 2 else ha_ref, n_e - 1)
        y = x_ref[...] + g2_ref[0] * acc_ref[...]
        if final:
            ms = jnp.mean(y * y, axis=-1, keepdims=True)
            y = y * lax.rsqrt(ms + RMS_EPS) * fg_ref[...]
        o_ref[...] = y


def peer_experts(h2, a_idx, b_idx, gw, u, v, x1, g2, tokens_per_batch, final_g=None, tm=512, te=1024):
    N, D = x1.shape
    E = u.shape[0]
    tm = min(tm, N)
    nb = g2.shape[0]
    n_e = E // te
    tok = lambda n: pl.BlockSpec((tm, n), lambda i, j: (i, 0))
    in_specs = [tok(D), tok(a_idx.shape[1]), tok(a_idx.shape[1]), tok(a_idx.shape[1]),
                pl.BlockSpec((te, D), lambda i, j: (jnp.minimum(j, n_e - 1), 0)),
                pl.BlockSpec((te, D), lambda i, j: (jnp.maximum(j - 1, 0), 0)),
                tok(D), pl.BlockSpec((1, 1, D), lambda i, j: (i * tm // tokens_per_batch, 0, 0))]
    args = [h2, a_idx, b_idx, gw, u, v, x1, g2.reshape(nb, 1, D)]
    if final_g is not None:
        in_specs.append(pl.BlockSpec((1, D), lambda i, j: (0, 0)))
        args.append(final_g.reshape(1, D))
    return pl.pallas_call(
        functools.partial(_peer_expert_kernel, final=final_g is not None, n_e=n_e),
        grid=(N // tm, n_e + 1),
        in_specs=in_specs,
        out_specs=tok(D),
        out_shape=jax.ShapeDtypeStruct((N, D), F32),
        scratch_shapes=[pltpu.VMEM((PEER_KEYS, tm, PEER_KEYS), BF16), pltpu.VMEM((tm, D), F32),
                        pltpu.VMEM((tm, te), F32), pltpu.VMEM((tm, te), F32)],
        compiler_params=_cparams("parallel", "arbitrary"),
        name="peer_experts",
    )(*args)


def peer_layer(x1, h2, wq, k1, k2, u, v, g2, final_g=None):
    B, T, D = x1.shape
    h2f = h2.reshape(B * T, D)
    a_idx, b_idx, gw = peer_select(h2f, wq, k1, k2)
    out = peer_experts(h2f, a_idx, b_idx, gw, u, v, x1.reshape(B * T, D), g2, T, final_g)
    return out.reshape(B, T, D)


def kernel(x, c, ada_w, ada_b, norm1_g, norm2_g, w_in, conv_dw_w, conv_dw_b, conv_gn_g, conv_gn_b,
           rwkv_mu, rwkv_w0, rwkv_w_up, rwkv_a0, rwkv_a_up, rwkv_g_up, rwkv_k_k, rwkv_k_a, rwkv_r_k,
           rwkv_ln_g, rwkv_ln_b, nsa_ck_pos, nsa_ck_w1, nsa_ck_w2, nsa_cv_pos, nsa_cv_w1, nsa_cv_w2,
           w_out, peer_wq, peer_k1, peer_k2, peer_u, peer_v, final_g):
    depth, D = norm1_g.shape
    d_conv = conv_dw_b.shape[1]
    d_rwkv = rwkv_w0.shape[1]
    n_conv = 2 * d_conv
    n_rwkv = rwkv_mu.shape[1]
    d_nsa = D - d_conv - d_rwkv
    n_kv = 6 * NSA_KV_HEADS * HEAD_DIM
    n_gate = 3 * NSA_KV_HEADS * NSA_GROUP
    cuts = np.cumsum([0, n_conv, n_rwkv, d_nsa, n_kv, n_gate])
    mod = ada_mod(c, ada_w, ada_b)
    for l in range(depth):
        sh1, sc1, g1, sh2, sc2, g2 = [mod[l, :, i * D:(i + 1) * D] for i in range(6)]
        w_l = w_in[l].astype(BF16)
        pieces = [w_l[:, cuts[i]:cuts[i + 1]] for i in range(5)]
        pieces[4] = jnp.pad(pieces[4], ((0, 0), (0, LANES - n_gate)))
        p_conv, p_rwkv, p_q, p_kv, p_gate = proj_in(x, sh1, sc1, norm1_g[l], pieces)
        o_conv = conv_mixer(p_conv, conv_dw_w[l], conv_dw_b[l], conv_gn_g[l], conv_gn_b[l])
        o_rwkv = rwkv_mixer(p_rwkv, rwkv_mu[l], rwkv_w0[l], rwkv_w_up[l], rwkv_a0[l], rwkv_a_up[l],
                            rwkv_g_up[l], rwkv_k_k[l], rwkv_k_a[l], rwkv_r_k[l], rwkv_ln_g[l], rwkv_ln_b[l])
        o_nsa = nsa_mixer(p_q, p_kv, p_gate, nsa_ck_pos[l], nsa_ck_w1[l], nsa_ck_w2[l],
                          nsa_cv_pos[l], nsa_cv_w1[l], nsa_cv_w2[l])
        wo = w_out[l].astype(BF16)
        wo_pieces = [wo[:d_conv], wo[d_conv:d_conv + d_rwkv], wo[d_conv + d_rwkv:]]
        x1, h2 = proj_out(x, g1, sh2, sc2, norm2_g[l], [o_conv, o_rwkv, o_nsa], wo_pieces)
        x = peer_layer(x1, h2, peer_wq[l].astype(BF16), peer_k1[l], peer_k2[l],
                       peer_u[l].astype(BF16), peer_v[l].astype(BF16), g2,
                       final_g if l == depth - 1 else None)
    return x
```

```python
import functools

import jax
import jax.numpy as jnp
import numpy as np
from jax import lax
from jax.experimental import pallas as pl
from jax.experimental.pallas import tpu as pltpu

F32 = jnp.float32
BF16 = jnp.bfloat16
HIGHEST = lax.Precision.HIGHEST

HEAD_DIM = 64
CONV_WIDTH = 31
CONV_EPS = 1e-5
RWKV_GN_EPS = 64e-5
RMS_EPS = 1e-6
LORA_W = 64
LORA_A = 64
LORA_G = 128
NSA_KV_HEADS = 2
NSA_GROUP = 4
CMP_BLOCK = 32
CMP_STRIDE = 16
CMP_HIDDEN = 128
SEL_BLOCK = 64
N_SEL = 16
WINDOW = 512
FORCE_SCORE = 1e4
NEG_INF = -1e30
LOG2E = 1.4426950408889634
PEER_HEADS = 8
PEER_KEYS = 128
PEER_TOPK = 16

LANES = 128
VMEM_LIMIT = 56 * 1024 * 1024


def _dot_hi(a, b):
    return jnp.dot(a, b, precision=HIGHEST, preferred_element_type=F32)


def _dot_bf(a, b):
    return jnp.dot(a.astype(BF16), b.astype(BF16), preferred_element_type=F32)


def _cparams(*sem):
    return pltpu.CompilerParams(dimension_semantics=sem, vmem_limit_bytes=VMEM_LIMIT)


def _group_avg_matrix(n, group):
    i = np.arange(n)
    return jnp.asarray((i[:, None] // group == i[None, :] // group).astype(np.float32) / group)


def _ada_kernel(c_ref, w_ref, b_ref, o_ref):
    c = c_ref[...]
    cs = c * jax.nn.sigmoid(c)
    o_ref[0] = _dot_hi(cs, w_ref[0]) + b_ref[0]


def ada_mod(c, ada_w, ada_b):
    L, D, N = ada_w.shape
    B = c.shape[0]
    bp = 8
    cp = jnp.zeros((bp, D), F32).at[:B].set(c)
    tn = N // 4
    out = pl.pallas_call(
        _ada_kernel,
        grid=(L, N // tn),
        in_specs=[pl.BlockSpec((bp, D), lambda l, j: (0, 0)),
                  pl.BlockSpec((1, D, tn), lambda l, j: (l, 0, j)),
                  pl.BlockSpec((1, 1, tn), lambda l, j: (l, 0, j))],
        out_specs=pl.BlockSpec((1, bp, tn), lambda l, j: (l, 0, j)),
        out_shape=jax.ShapeDtypeStruct((L, bp, N), F32),
        compiler_params=_cparams("parallel", "parallel"),
        name="ada_mod",
    )(cp, ada_w, ada_b.reshape(L, 1, N))
    return out[:, :B]


def _modulated_norm(x, g, sc, sh):
    ms = jnp.mean(x * x, axis=-1, keepdims=True)
    return x * lax.rsqrt(ms + RMS_EPS) * g * (1.0 + sc) + sh


def _proj_in_kernel(x_ref, sh_ref, sc_ref, g_ref, *refs):
    n = len(refs) // 2
    h = _modulated_norm(x_ref[0], g_ref[...], sc_ref[0], sh_ref[0]).astype(BF16)
    for w_ref, o_ref in zip(refs[:n], refs[n:]):
        o_ref[0] = jnp.dot(h, w_ref[...], preferred_element_type=F32)


def proj_in(x, sh, sc, g, weights, tm=512):
    B, T, D = x.shape
    tm = min(tm, T)
    vec = pl.BlockSpec((1, 1, D), lambda b, i: (b, 0, 0))
    in_specs = [pl.BlockSpec((1, tm, D), lambda b, i: (b, i, 0)), vec, vec,
                pl.BlockSpec((1, D), lambda b, i: (0, 0))]
    in_specs += [pl.BlockSpec(w.shape, lambda b, i: (0, 0)) for w in weights]
    out_specs = [pl.BlockSpec((1, tm, w.shape[1]), lambda b, i: (b, i, 0)) for w in weights]
    out_shape = [jax.ShapeDtypeStruct((B, T, w.shape[1]), F32) for w in weights]
    return pl.pallas_call(
        _proj_in_kernel,
        grid=(B, T // tm),
        in_specs=in_specs, out_specs=out_specs, out_shape=out_shape,
        compiler_params=_cparams("parallel", "parallel"),
        name="proj_in",
    )(x, sh.reshape(B, 1, D), sc.reshape(B, 1, D), g.reshape(1, D), *weights)


CONV_HALO = 32


def _conv_kernel(p_ref, w_ref, b_ref, gg_ref, gb_ref, m_ref, o_ref, ext_ref):
    i = pl.program_id(1)
    tt, dc = o_ref.shape[1], o_ref.shape[2]

    @pl.when(i == 0)
    def _():
        ext_ref[0:CONV_HALO, :] = jnp.zeros((CONV_HALO, dc), F32)

    @pl.when(i > 0)
    def _():
        ext_ref[0:CONV_HALO, :] = ext_ref[tt:tt + CONV_HALO, :]

    p = p_ref[0]
    ext_ref[CONV_HALO:CONV_HALO + tt, :] = p[:, :dc] * jax.nn.sigmoid(p[:, dc:])
    off = CONV_HALO - (CONV_WIDTH - 1)
    acc = jnp.zeros((tt, dc), F32) + b_ref[...]
    for j in range(CONV_WIDTH):
        acc = acc + ext_ref[off + j:off + j + tt, :] * w_ref[j:j + 1, :]
    m = m_ref[...]
    mu = _dot_hi(acc, m)
    d = acc - mu
    var = _dot_hi(d * d, m)
    y = d * lax.rsqrt(var + CONV_EPS) * gg_ref[...] + gb_ref[...]
    o_ref[0] = y * jax.nn.sigmoid(y)


def conv_mixer(p, dw_w, dw_b, gn_g, gn_b, tt=512):
    B, T, two_dc = p.shape
    dc = two_dc // 2
    tt = min(tt, T)
    wpad = jnp.zeros((32, dc), F32).at[:CONV_WIDTH].set(dw_w.reshape(CONV_WIDTH, dc))
    row = pl.BlockSpec((1, dc), lambda b, i: (0, 0))
    return pl.pallas_call(
        _conv_kernel,
        grid=(B, T // tt),
        in_specs=[pl.BlockSpec((1, tt, two_dc), lambda b, i: (b, i, 0)),
                  pl.BlockSpec((32, dc), lambda b, i: (0, 0)), row, row, row,
                  pl.BlockSpec((dc, dc), lambda b, i: (0, 0))],
        out_specs=pl.BlockSpec((1, tt, dc), lambda b, i: (b, i, 0)),
        out_shape=jax.ShapeDtypeStruct((B, T, dc), F32),
        scratch_shapes=[pltpu.VMEM((tt + CONV_HALO, dc), F32)],
        compiler_params=_cparams("parallel", "arbitrary"),
        name="conv_mixer",
    )(p, wpad, dw_b.reshape(1, dc), gn_g.reshape(1, dc), gn_b.reshape(1, dc),
      _group_avg_matrix(dc, HEAD_DIM))


SCAN_SUB = 64
SCAN_CHUNK = 128


def _group_sum_matrix(n, group):
    i = np.arange(n)
    return jnp.asarray((i[:, None] // group == i[None, :] // group).astype(np.float32))


def _rwkv_pre_kernel(p_ref, halo_ref, mu_ref, w0_ref, wup_ref, a0_ref, aup_ref, gup_ref, kk_ref, ka_ref,
                     rk_ref, ones_ref, tri_ref, w_o, kk_o, b_o, k_o, v_o, r_o, bonus_o, g_o, ext_ref):
    i = pl.program_id(1)
    tt = p_ref.shape[1]
    dr = w_o.shape[2]
    p = p_ref[0]
    first = (i > 0).astype(F32)
    ext_ref[0:8, :] = halo_ref[0] * first
    ext_ref[8:8 + tt, :] = p
    prev = ext_ref[7:7 + tt, :]
    xs = p + (prev - p) * mu_ref[...]
    r = xs[:, 0:dr]
    k = xs[:, dr:2 * dr]
    v = xs[:, 2 * dr:3 * dr]
    o = 3 * dr
    xw = xs[:, o:o + LORA_W]
    xa = xs[:, o + LORA_W:o + LORA_W + LORA_A]
    xg = xs[:, o + LORA_W + LORA_A:o + LORA_W + LORA_A + LORA_G]
    z = w0_ref[...] + _dot_hi(jnp.tanh(xw), wup_ref[...])
    w_log = -jax.nn.softplus(-z) - 0.5
    log_decay = -jnp.exp(w_log)
    cum = _dot_hi(tri_ref[...], log_decay)
    p_incl = jnp.exp(cum)
    p_excl = jnp.exp(cum - log_decay)
    p_inv = jnp.exp(-cum)
    a = jax.nn.sigmoid(a0_ref[...] + _dot_hi(xa, aup_ref[...]))
    g = _dot_hi(jax.nn.sigmoid(xg), gup_ref[...])
    ones = ones_ref[...]
    kk = k * kk_ref[...]
    nrm = jnp.sqrt(_dot_hi(kk * kk, ones))
    kk = kk / jnp.maximum(nrm, 1e-12)
    k_eff = k * (1.0 + (a - 1.0) * ka_ref[...])
    bonus = _dot_hi(r * k_eff * rk_ref[...], ones) * v
    w_o[0] = p_incl
    kk_o[0] = kk * p_excl
    b_o[0] = kk * a * p_inv
    k_o[0] = k_eff * p_inv
    v_o[0] = v
    r_o[0] = r * p_incl
    bonus_o[0] = bonus
    g_o[0] = g


def rwkv_pre(p, mu, w0, w_up, a0, a_up, g_up, k_k, k_a, r_k, tt=512):
    B, T, n_in = p.shape
    dr = w0.shape[0]
    tt = min(tt, T)
    nb8 = tt // 8
    full = lambda a: pl.BlockSpec(a.shape, lambda b, i: (0,) * a.ndim)
    blk_id = np.arange(tt) // SCAN_SUB
    tri = jnp.asarray(((blk_id[:, None] == blk_id[None, :])
                       & (np.arange(tt)[:, None] >= np.arange(tt)[None, :])).astype(np.float32))
    args = [mu.reshape(1, n_in), w0.reshape(1, dr), w_up, a0.reshape(1, dr), a_up, g_up,
            k_k.reshape(1, dr), k_a.reshape(1, dr), r_k.reshape(1, dr), _group_sum_matrix(dr, HEAD_DIM), tri]
    out = jax.ShapeDtypeStruct((B, T, dr), F32)
    ospec = pl.BlockSpec((1, tt, dr), lambda b, i: (b, i, 0))
    return pl.pallas_call(
        _rwkv_pre_kernel,
        grid=(B, T // tt),
        in_specs=[pl.BlockSpec((1, tt, n_in), lambda b, i: (b, i, 0)),
                  pl.BlockSpec((1, 8, n_in), lambda b, i: (b, jnp.maximum(i * nb8 - 1, 0), 0))]
                 + [full(a) for a in args],
        out_specs=[ospec] * 8, out_shape=[out] * 8,
        scratch_shapes=[pltpu.VMEM((tt + 8, n_in), F32)],
        compiler_params=_cparams("parallel", "parallel"),
        name="rwkv_pre",
    )(p, p, *args)


def _scan_select_matrices():
    e = np.zeros((SCAN_SUB // 2, 4 * SCAN_SUB, 2 * LANES), np.float32)
    for t in range(SCAN_SUB):
        c = (t % 2) * LANES
        for part in range(2):
            e[t // 2, 2 * part * SCAN_SUB + t, c:c + HEAD_DIM] = 1.0
            e[t // 2, (2 * part + 1) * SCAN_SUB + t, c + HEAD_DIM:c + LANES] = 1.0
    return jnp.asarray(e, BF16)


def _rwkv_scan_kernel(pc_ref, kk_ref, b_ref, k_ref, r_ref, v_ref, bonus_ref, g_ref, lng_ref, lnb_ref,
                      e_ref, avg_ref, o_ref, st_ref, lhs_ref, y_ref):
    nb, tc, dr = v_ref.shape
    n_half = dr // LANES
    n_pair = nb * n_half
    n_sub = tc // SCAN_SUB
    quantities = (kk_ref, b_ref, k_ref, r_ref)

    @pl.when(pl.program_id(0) == 0)
    def _():
        st_ref[...] = jnp.zeros(st_ref.shape, F32)

    for qi, q_ref in enumerate(quantities):
        for p in range(n_pair):
            b, hb = divmod(p, n_half)
            xt = q_ref[b, :, hb * LANES:(hb + 1) * LANES].T
            for s in range(n_sub):
                cat = jnp.concatenate([xt[0:HEAD_DIM, s * SCAN_SUB:(s + 1) * SCAN_SUB],
                                       xt[HEAD_DIM:, s * SCAN_SUB:(s + 1) * SCAN_SUB]], axis=1)
                hi = cat.astype(BF16)
                lo = (cat - hi.astype(F32)).astype(BF16)
                row = (qi * n_pair + p) * HEAD_DIM
                lhs_ref[s, row:row + HEAD_DIM, :] = jnp.concatenate([hi, lo], axis=1)

    sub_iota = lax.broadcasted_iota(jnp.int32, (8, LANES), 0)
    ri = lax.broadcasted_iota(jnp.int32, (HEAD_DIM, LANES), 0)
    ci = lax.broadcasted_iota(jnp.int32, (HEAD_DIM, LANES), 1)
    diag2 = (ci % HEAD_DIM == ri).astype(F32)
    rj = lax.broadcasted_iota(jnp.int32, (LANES, LANES), 0)
    cj = lax.broadcasted_iota(jnp.int32, (LANES, LANES), 1)
    half_ones = (rj // HEAD_DIM == cj // HEAD_DIM).astype(F32)
    for s in range(n_sub):
        def steps8(t8, carry, s=s):
            base = pl.multiple_of(s * SCAN_SUB + t8 * 8, 8)
            y8 = [jnp.zeros((8, LANES), F32) for _ in range(n_pair)]
            for j in range(8):
                if j % 2 == 0:
                    z2 = jnp.dot(lhs_ref[s], e_ref[t8 * 4 + j // 2], preferred_element_type=F32)
                z = z2[:, (j % 2) * LANES:(j % 2 + 1) * LANES]
                for p in range(n_pair):
                    b, hb = divmod(p, n_half)
                    col = lambda qi: z[(qi * n_pair + p) * HEAD_DIM:(qi * n_pair + p + 1) * HEAD_DIM, :]
                    st = st_ref[p]
                    sa = jnp.sum(st * col(0), axis=0, keepdims=True)
                    v_row = v_ref[b, pl.ds(base, 8), hb * LANES:(hb + 1) * LANES][j:j + 1, :]
                    st = st - col(1) * sa + col(2) * v_row
                    y = jnp.sum(st * col(3), axis=0, keepdims=True)
                    y8[p] = jnp.where(sub_iota == j, y, y8[p])
                    st_ref[p] = st
            for p in range(n_pair):
                b, hb = divmod(p, n_half)
                y_ref[b, pl.ds(base, 8), hb * LANES:(hb + 1) * LANES] = y8[p]
            return carry
        lax.fori_loop(0, SCAN_SUB // 8, steps8, 0)
        last = s * SCAN_SUB + SCAN_SUB - 1
        for p in range(n_pair):
            b, hb = divmod(p, n_half)
            p_row = pc_ref[b, last - 7:last + 1, hb * LANES:(hb + 1) * LANES][7:8, :]
            st_ref[p] = st_ref[p] * _dot_hi(diag2 * p_row, half_ones)

    avg = avg_ref[...]
    for b in range(nb):
        y = y_ref[b]
        mu = _dot_hi(y, avg)
        d = y - mu
        var = _dot_hi(d * d, avg)
        yn = d * lax.rsqrt(var + RWKV_GN_EPS) * lng_ref[...] + lnb_ref[...]
        o_ref[b] = (yn + bonus_ref[b]) * g_ref[b]


def rwkv_scan(w, kk, bq, k, r, v, bonus, g, ln_g, ln_b):
    B, T, dr = v.shape
    tc = min(SCAN_CHUNK, T)
    n_pair = B * dr // LANES
    blk = pl.BlockSpec((B, tc, dr), lambda i: (0, i, 0))
    row = pl.BlockSpec((1, dr), lambda i: (0, 0))
    e = _scan_select_matrices()
    return pl.pallas_call(
        _rwkv_scan_kernel,
        grid=(T // tc,),
        in_specs=[blk] * 8 + [row, row, pl.BlockSpec(e.shape, lambda i: (0, 0, 0)),
                              pl.BlockSpec((dr, dr), lambda i: (0, 0))],
        out_specs=blk,
        out_shape=jax.ShapeDtypeStruct((B, T, dr), F32),
        scratch_shapes=[pltpu.VMEM((n_pair, HEAD_DIM, LANES), F32),
                        pltpu.VMEM((tc // SCAN_SUB, 4 * n_pair * HEAD_DIM, 4 * SCAN_SUB), BF16),
                        pltpu.VMEM((B, tc, dr), F32)],
        compiler_params=_cparams("arbitrary"),
        name="rwkv_scan",
    )(w, kk, bq, k, r, v, bonus, g, ln_g.reshape(1, dr), ln_b.reshape(1, dr), e, _group_avg_matrix(dr, HEAD_DIM))


def rwkv_mixer(p, mu, w0, w_up, a0, a_up, g_up, k_k, k_a, r_k, ln_g, ln_b):
    w, kk, bq, k, v, r, bonus, g = rwkv_pre(p, mu, w0, w_up, a0, a_up, g_up, k_k, k_a, r_k.reshape(-1))
    return rwkv_scan(w, kk, bq, k, r, v, bonus, g, ln_g, ln_b)


def _dot_nt(a, b, **kw):
    return lax.dot_general(a, b, (((1,), (1,)), ((), ())), preferred_element_type=F32, **kw)


def _compress_kernel(kz_ref, vz_ref, kpos_ref, kw1_ref, kw2_ref, vpos_ref, vw1_ref, vw2_ref,
                     kc_ref, vc_ref, shift_ref):
    n = kz_ref.shape[2]
    half = kz_ref.shape[3]

    def one(z_ref, pos_ref, w1_ref, w2_ref, o_ref):
        z = z_ref[0, 0]
        top = _dot_bf(z + pos_ref[0:1, :], w1_ref[0:half, :])
        bot = _dot_bf(z + pos_ref[1:2, :], w1_ref[half:2 * half, :])
        shift_ref[0:n, :] = bot
        shift_ref[n:n + 8, :] = jnp.zeros((8, bot.shape[1]), F32)
        pre = top + shift_ref[1:n + 1, :]
        hid = pre * jax.nn.sigmoid(pre)
        o_ref[0, 0] = _dot_bf(hid, w2_ref[...])

    one(kz_ref, kpos_ref, kw1_ref, kw2_ref, kc_ref)
    one(vz_ref, vpos_ref, vw1_ref, vw2_ref, vc_ref)


def nsa_compress(k_cmp, v_cmp, ck_pos, ck_w1, ck_w2, cv_pos, cv_w1, cv_w2):
    B, G, T, dk = k_cmp.shape
    n = T // CMP_STRIDE
    half = CMP_STRIDE * dk
    zspec = pl.BlockSpec((1, 1, n, half), lambda b, g: (b, g, 0, 0))
    full = lambda a: pl.BlockSpec(a.shape, lambda b, g: (0,) * a.ndim)
    ospec = pl.BlockSpec((1, 1, n, dk), lambda b, g: (b, g, 0, 0))
    args = [ck_pos.reshape(2, half), ck_w1.astype(BF16), ck_w2.astype(BF16),
            cv_pos.reshape(2, half), cv_w1.astype(BF16), cv_w2.astype(BF16)]
    return pl.pallas_call(
        _compress_kernel,
        grid=(B, G),
        in_specs=[zspec, zspec] + [full(a) for a in args],
        out_specs=[ospec, ospec],
        out_shape=[jax.ShapeDtypeStruct((B, G, n, dk), F32)] * 2,
        scratch_shapes=[pltpu.VMEM((n + 8, CMP_HIDDEN), F32)],
        compiler_params=_cparams("parallel", "parallel"),
        name="nsa_compress",
    )(k_cmp.reshape(B, G, n, half), v_cmp.reshape(B, G, n, half), *args)


NSA_TQ = 256
NSA_TK = 1024


def _split_bf16(a):
    hi = a.astype(BF16)
    return hi, (a - hi.astype(F32)).astype(BF16)


def _dot_3pass(a, b):
    ah, al = _split_bf16(a)
    bh, bl = _split_bf16(b)
    d = lambda x, y: jnp.dot(x, y, preferred_element_type=F32)
    return d(ah, bh) + d(ah, bl) + d(al, bh)


def _nsa_kernel(q_ref, gate_ref, kc_ref, vct_ref, ks_ref, vst_ref, kw_ref, vwt_ref, ovt_ref,
                o_ref, m_ref, acc_ref, ow_ref, sel_ref, s0_ref, s1_ref, p_ref):
    qi = pl.program_id(2)
    dk = q_ref.shape[3]
    tq = gate_ref.shape[3]
    R = q_ref.shape[4] // tq
    n_c = kc_ref.shape[2]
    n_blk = ovt_ref.shape[0]
    T = ks_ref.shape[2]
    tk = s0_ref.shape[0]
    t0 = qi * tq
    qt = q_ref[0, 0, 0]
    qt_bf = qt.astype(BF16)
    t_row = t0 + lax.broadcasted_iota(jnp.int32, (1, tq), 1)
    lanes = lambda r: slice(r * tq, (r + 1) * tq)

    s = _dot_3pass(kc_ref[0, 0], qt)
    c_end = lax.broadcasted_iota(jnp.int32, (n_c, tq), 0) * CMP_STRIDE + (CMP_BLOCK - 1)
    valid_c = c_end <= t_row
    any_c = (t_row >= CMP_BLOCK - 1).astype(F32)
    p_sum = jnp.zeros((n_c, tq), F32)
    ps = []
    for r in range(R):
        s_r = jnp.where(valid_c, s[:, lanes(r)], NEG_INF)
        e = jnp.exp2(s_r - jnp.max(s_r, axis=0, keepdims=True))
        p_r = e * (any_c / jnp.sum(e, axis=0, keepdims=True))
        p_sum = p_sum + p_r
        ps.append(p_r.astype(BF16))
    o_c = jnp.dot(vct_ref[0, 0].astype(BF16), jnp.concatenate(ps, axis=1), preferred_element_type=F32)

    imp = _dot_hi(ovt_ref[...], p_sum)
    blk = lax.broadcasted_iota(jnp.int32, (n_blk, tq), 0).astype(F32)
    cur = (t_row // SEL_BLOCK).astype(F32)
    forced = (blk == 0.0) | (blk == cur) | (blk == cur - 1.0)
    x = jnp.where(forced, FORCE_SCORE, jnp.where(blk <= cur, imp, -1.0))
    x = jnp.where(blk < float(T // SEL_BLOCK), x, -3e38)
    sel = jnp.zeros((n_blk, tq), F32)
    for _ in range(N_SEL):
        mx = jnp.max(x, axis=0, keepdims=True)
        idx = jnp.min(jnp.where(x == mx, blk, float(n_blk)), axis=0, keepdims=True)
        hit = blk == idx
        sel = jnp.where(hit, 1.0, sel)
        x = jnp.where(hit, -jnp.inf, x)

    span = WINDOW + tq
    w0 = pl.multiple_of(jnp.maximum(t0 - WINDOW, 0), tq)
    s = jnp.dot(kw_ref[0, 0, pl.ds(w0, span), :], qt_bf, preferred_element_type=F32)
    dist = t_row - (w0 + lax.broadcasted_iota(jnp.int32, (span, tq), 0))
    bias_w = jnp.where((dist >= 0) & (dist < WINDOW), 0.0, NEG_INF)
    ps = []
    for r in range(R):
        s_r = s[:, lanes(r)] + bias_w
        ps.append(jnp.exp2(s_r - jnp.max(s_r, axis=0, keepdims=True)).astype(BF16))
    ow_ref[...] = jnp.dot(vwt_ref[0, 0, :, pl.ds(w0, span)], jnp.concatenate(ps, axis=1),
                          preferred_element_type=F32)

    bpt = tk // SEL_BLOCK
    n_kt = T // tk
    n_full = (t0 + tq - 1) // tk
    sel_ref[...] = jnp.where(sel > 0.5, 0.0, NEG_INF)
    m_ref[...] = jnp.full(m_ref.shape, NEG_INF, F32)
    acc_ref[...] = jnp.zeros(acc_ref.shape, F32)
    q_pad = jnp.zeros((ks_ref.shape[3] - dk - bpt, R * tq), BF16)

    def scores(kt):
        kt = jnp.minimum(kt, n_kt - 1)
        rows = sel_ref[pl.ds(pl.multiple_of(kt * bpt, bpt), bpt), :].astype(BF16)
        q_aug = jnp.concatenate([qt_bf, jnp.concatenate([rows] * R, axis=1), q_pad], axis=0)
        off = pl.multiple_of(kt * tk, tk)
        return jnp.dot(ks_ref[0, 0, pl.ds(off, tk), :], q_aug, preferred_element_type=F32)

    def softmax_tile(s_ref, kt, causal):
        if causal:
            key_pos = kt * tk + lax.broadcasted_iota(jnp.int32, (tk, tq), 0)
            bias = jnp.where(key_pos <= t_row, 0.0, NEG_INF)
        for r in range(R):
            m_old = m_ref[:, lanes(r)]
            if causal:
                m_new = jnp.maximum(m_old, jnp.max(s_ref[:, lanes(r)] + bias, axis=0, keepdims=True))
                p_r = jnp.exp2((s_ref[:, lanes(r)] - m_new) + bias)
            else:
                m_new = jnp.maximum(m_old, jnp.max(s_ref[:, lanes(r)], axis=0, keepdims=True))
                p_r = jnp.exp2(s_ref[:, lanes(r)] - m_new)
            m_ref[:, lanes(r)] = m_new
            p_ref[:, lanes(r)] = p_r.astype(BF16)
            acc_ref[:, lanes(r)] = acc_ref[:, lanes(r)] * jnp.exp2(m_old - m_new)
        off = pl.multiple_of(kt * tk, tk)
        acc_ref[...] += jnp.dot(vst_ref[0, 0, :, pl.ds(off, tk)], p_ref[...], preferred_element_type=F32)

    s0_ref[...] = scores(0)

    def tile_pair(i, carry):
        s1_ref[...] = scores(2 * i + 1)
        softmax_tile(s0_ref, 2 * i, False)
        s0_ref[...] = scores(2 * i + 2)
        softmax_tile(s1_ref, 2 * i + 1, False)
        return carry

    lax.fori_loop(0, n_full // 2, tile_pair, 0)
    odd = n_full % 2 == 1

    @pl.when(odd)
    def _():
        s1_ref[...] = scores(n_full)
        softmax_tile(s0_ref, n_full - 1, False)
        softmax_tile(s1_ref, n_full, True)

    @pl.when(jnp.logical_not(odd))
    def _():
        softmax_tile(s0_ref, n_full, True)

    acc = acc_ref[...]
    o_s = acc[0:dk] * (1.0 / acc[dk:dk + 1])
    o_w = ow_ref[...]
    o_w = o_w[0:dk] * (1.0 / o_w[dk:dk + 1])

    gate = jax.nn.sigmoid(gate_ref[0, 0])
    outs = []
    for r in range(R):
        outs.append(gate[r:r + 1] * o_c[:, lanes(r)] + gate[R + r:R + r + 1] * o_s[:, lanes(r)]
                    + gate[2 * R + r:2 * R + r + 1] * o_w[:, lanes(r)])
    o_ref[0] = jnp.concatenate(outs, axis=0).T


def _overlap_matrix_t(n_c, n_blk):
    c0 = np.arange(n_c)[None, :] * CMP_STRIDE
    s0 = np.arange(n_blk)[:, None] * SEL_BLOCK
    return jnp.asarray(((c0 < s0 + SEL_BLOCK) & (c0 + CMP_BLOCK > s0)).astype(np.float32))


def nsa_attention(qt, gate_t, kc, vct, k_sel, vt_sel, k_win, vt_win):
    B, G, nq, dk, L = qt.shape
    T = k_win.shape[2]
    tq = T // nq
    R = L // tq
    n_c = kc.shape[2]
    tk = min(NSA_TK, T)
    n_blk = max(T // SEL_BLOCK, LANES)
    ovt = _overlap_matrix_t(n_c, n_blk)
    res = lambda a: pl.BlockSpec((1, 1) + a.shape[2:], lambda b, g, i: (b, g, 0, 0))
    return pl.pallas_call(
        _nsa_kernel,
        grid=(B, G, nq),
        in_specs=[pl.BlockSpec((1, 1, 1, dk, L), lambda b, g, i: (b, g, i, 0, 0)),
                  pl.BlockSpec((1, 1, gate_t.shape[2], tq), lambda b, g, i: (b, g, 0, i)),
                  res(kc), res(vct), res(k_sel), res(vt_sel), res(k_win), res(vt_win),
                  pl.BlockSpec(ovt.shape, lambda b, g, i: (0, 0))],
        out_specs=pl.BlockSpec((1, tq, R * dk), lambda b, g, i: (b, i, g)),
        out_shape=jax.ShapeDtypeStruct((B, T, G * R * dk), F32),
        scratch_shapes=[pltpu.VMEM((1, L), F32), pltpu.VMEM((dk + 16, L), F32),
                        pltpu.VMEM((dk + 16, L), F32), pltpu.VMEM((n_blk, tq), F32),
                        pltpu.VMEM((tk, L), F32), pltpu.VMEM((tk, L), F32), pltpu.VMEM((tk, L), BF16)],
        compiler_params=_cparams("parallel", "parallel", "arbitrary"),
        name="nsa_attention",
    )(qt, gate_t, kc, vct, k_sel, vt_sel, k_win, vt_win, ovt)


def nsa_mixer(p_q, p_kv, p_gate, ck_pos, ck_w1, ck_w2, cv_pos, cv_w1, cv_w2):
    B, T, _ = p_q.shape
    G, R, dk = NSA_KV_HEADS, NSA_GROUP, HEAD_DIM
    tq = min(NSA_TQ, T)
    kv6 = p_kv.reshape(B, T, 6, G, dk)
    rows = lambda i: kv6[:, :, i].transpose(0, 2, 1, 3)
    cols = lambda i: kv6[:, :, i].transpose(0, 2, 3, 1)
    kc, vc = nsa_compress(rows(0), rows(1), ck_pos, ck_w1, ck_w2, cv_pos, cv_w1, cv_w2)
    qt = (p_q * (dk ** -0.5 * LOG2E)).reshape(B, T // tq, tq, G, R, dk).transpose(0, 3, 1, 5, 4, 2)
    qt = qt.reshape(B, G, T // tq, dk, R * tq)
    gate_t = p_gate[..., :3 * G * R].reshape(B, T, 3, G, R).transpose(0, 3, 2, 4, 1).reshape(B, G, 3 * R, T)
    gate_t = jnp.pad(gate_t, ((0, 0), (0, 0), (0, 16 - 3 * R), (0, 0)))
    tk = min(NSA_TK, T)
    blk_onehot = (jnp.arange(T)[:, None] // SEL_BLOCK % (tk // SEL_BLOCK) == jnp.arange(dk)[None, :]).astype(BF16)
    k_sel = jnp.concatenate([rows(2).astype(BF16), jnp.broadcast_to(blk_onehot, (B, G, T, dk))], axis=-1)
    ones_rows = jnp.zeros((B, G, 16, T), BF16).at[:, :, 0].set(1.0)
    with_ones = lambda vt: jnp.concatenate([vt.astype(BF16), ones_rows], axis=2)
    return nsa_attention(qt, gate_t, kc, vc.transpose(0, 1, 3, 2), k_sel, with_ones(cols(3)),
                         rows(4).astype(BF16), with_ones(cols(5)))


def _proj_out_kernel(x_ref, g1_ref, sh_ref, sc_ref, ng_ref, *refs):
    n = (len(refs) - 2) // 2
    x1_ref, h2_ref = refs[2 * n:]
    acc = None
    for m_ref, w_ref in zip(refs[:n], refs[n:2 * n]):
        d = jnp.dot(m_ref[0].astype(BF16), w_ref[...], preferred_element_type=F32)
        acc = d if acc is None else acc + d
    x1 = x_ref[0] + g1_ref[0] * acc
    x1_ref[0] = x1
    h2_ref[0] = _modulated_norm(x1, ng_ref[...], sc_ref[0], sh_ref[0]).astype(BF16)


def proj_out(x, g1, sh2, sc2, norm_g, mixes, weights, tm=512):
    B, T, D = x.shape
    tm = min(tm, T)
    vec = pl.BlockSpec((1, 1, D), lambda b, i: (b, 0, 0))
    tile = lambda n: pl.BlockSpec((1, tm, n), lambda b, i: (b, i, 0))
    return pl.pallas_call(
        _proj_out_kernel,
        grid=(B, T // tm),
        in_specs=[tile(D), vec, vec, vec, pl.BlockSpec((1, D), lambda b, i: (0, 0))]
                 + [tile(m.shape[2]) for m in mixes]
                 + [pl.BlockSpec(w.shape, lambda b, i: (0, 0)) for w in weights],
        out_specs=[tile(D), tile(D)],
        out_shape=[jax.ShapeDtypeStruct((B, T, D), F32), jax.ShapeDtypeStruct((B, T, D), BF16)],
        compiler_params=_cparams("parallel", "parallel"),
        name="proj_out",
    )(x, g1.reshape(B, 1, D), sh2.reshape(B, 1, D), sc2.reshape(B, 1, D), norm_g.reshape(1, D), *mixes, *weights)


_CAND_BLOCKS = [(0, 16), (1, 8), (None, 8), (2, 5), (3, 4), (4, 3), (5, 2), (6, 2), (7, 2)]
_CAND_PAD = 6


def _cand_constants(tm):
    flat, neg = [], []
    for i, rows in _CAND_BLOCKS:
        for r in range(rows):
            flat.append((8 + r) * PEER_TOPK if i is None else i * PEER_TOPK + r)
            neg.append(0.0)
    flat += [1e9] * _CAND_PAD
    neg += [-np.inf] * _CAND_PAD
    flat = np.tile(np.asarray(flat, np.float32)[:, None], (1, tm))
    neg = np.tile(np.asarray(neg, np.float32)[:, None], (1, tm))
    return jnp.asarray(flat), jnp.asarray(neg)


def _topk_rows(x, row_id, k, n_rows):
    vals, idxs = [], []
    for _ in range(k):
        mx = jnp.max(x, axis=0, keepdims=True)
        idx = jnp.min(jnp.where(x == mx, row_id, float(n_rows)), axis=0, keepdims=True)
        x = jnp.where(row_id == idx, -jnp.inf, x)
        vals.append(mx)
        idxs.append(idx)
    return jnp.concatenate(vals, axis=0), jnp.concatenate(idxs, axis=0)


def _sorting_network(n):
    pairs = []
    p = 1
    while p < n:
        k = p
        while k >= 1:
            for j in range(k % p, n - k, 2 * k):
                for i in range(min(k, n - j - k)):
                    if (i + j) // (2 * p) == (i + j + k) // (2 * p):
                        pairs.append((i + j, i + j + k))
            k //= 2
        p *= 2
    return pairs


def _topk_sorted_columns(x, k):
    n_rows, m = x.shape
    n_lvl = n_rows // 8
    slot = lax.broadcasted_iota(jnp.int32, (8, m), 0).astype(F32)
    s = [x[8 * v:8 * v + 8] for v in range(n_lvl)]
    p = [slot + 8.0 * v for v in range(n_lvl)]
    for i, j in _sorting_network(n_lvl):
        swap = s[j] > s[i]
        s[i], s[j] = jnp.where(swap, s[j], s[i]), jnp.where(swap, s[i], s[j])
        p[i], p[j] = jnp.where(swap, p[j], p[i]), jnp.where(swap, p[i], p[j])
    tie = jnp.zeros((8, m), F32)
    for d in range(n_lvl - 1):
        tie = jnp.where(s[d] == s[d + 1], 1.0, tie)
    tie = jnp.max(tie, axis=0, keepdims=True)
    s.append(jnp.full((8, m), -jnp.inf, F32))
    p.append(jnp.zeros((8, m), F32))
    vals, idxs = [], []
    for r in range(k):
        mx = jnp.max(s[0], axis=0, keepdims=True)
        hit = s[0] == mx
        cnt = jnp.sum(jnp.where(hit, 1.0, 0.0), axis=0, keepdims=True)
        tie = jnp.where(cnt > 1.5, 1.0, tie)
        vals.append(mx)
        idxs.append(jnp.sum(jnp.where(hit, p[0], 0.0), axis=0, keepdims=True))
        for d in range(n_lvl - r):
            s[d] = jnp.where(hit, s[d + 1], s[d])
            p[d] = jnp.where(hit, p[d + 1], p[d])
    tie = jnp.where(jnp.max(s[0], axis=0, keepdims=True) == vals[-1], 1.0, tie)
    return jnp.concatenate(vals, axis=0), jnp.concatenate(idxs, axis=0), tie


def _peer_select_kernel(h_ref, wq_ref, k1_ref, k2_ref, flat_ref, neg_ref, a_ref, b_ref, g_ref):
    tm = h_ref.shape[0]
    nk = k1_ref.shape[0]
    half = k1_ref.shape[1]
    q = jnp.dot(h_ref[...], wq_ref[...], preferred_element_type=F32)
    flat = flat_ref[...]
    neg = neg_ref[...]
    k1 = k1_ref[...]
    k2 = k2_ref[...]

    def route(topk):
        a_all, b_all, g_all = [], [], []
        tie = jnp.zeros((1, tm), F32)
        for h in range(PEER_HEADS):
            q1 = q[:, (2 * h) * half:(2 * h + 1) * half]
            q2 = q[:, (2 * h + 1) * half:(2 * h + 2) * half]
            v1, i1, t1 = topk(_dot_nt(k1, q1, precision=HIGHEST))
            v2, i2, t2 = topk(_dot_nt(k2, q2, precision=HIGHEST))
            tie = jnp.maximum(tie, jnp.maximum(t1, t2))
            vals, ai, bj = [], [], []
            for i, rows in _CAND_BLOCKS:
                if i is None:
                    vals.append(v1[8:16] + v2[0:1])
                    ai.append(i1[8:16])
                    bj.append(jnp.broadcast_to(i2[0:1], (8, tm)))
                else:
                    vals.append(v1[i:i + 1] + v2[0:rows])
                    ai.append(jnp.broadcast_to(i1[i:i + 1], (rows, tm)))
                    bj.append(i2[0:rows])
            pad = [jnp.zeros((_CAND_PAD, tm), F32)]
            cand = jnp.concatenate(vals + pad, axis=0) + neg
            ai = jnp.concatenate(ai + pad, axis=0)
            bj = jnp.concatenate(bj + pad, axis=0)
            sc, sa, sb = [], [], []
            for _ in range(PEER_TOPK):
                mx = jnp.max(cand, axis=0, keepdims=True)
                fid = jnp.min(jnp.where(cand == mx, flat, 1e9), axis=0, keepdims=True)
                hit = flat == fid
                sc.append(mx)
                sa.append(jnp.sum(jnp.where(hit, ai, 0.0), axis=0, keepdims=True))
                sb.append(jnp.sum(jnp.where(hit, bj, 0.0), axis=0, keepdims=True))
                cand = jnp.where(hit, -jnp.inf, cand)
            sc = jnp.concatenate(sc, axis=0)
            e = jnp.exp(sc - sc[0:1])
            g_all.append(e / jnp.sum(e, axis=0, keepdims=True))
            a_all.append(jnp.concatenate(sa, axis=0))
            b_all.append(jnp.concatenate(sb, axis=0))
        a_ref[...] = jnp.concatenate(a_all, axis=0).T
        b_ref[...] = jnp.concatenate(b_all, axis=0).T
        g_ref[...] = jnp.concatenate(g_all, axis=0).T
        return tie

    tie = route(functools.partial(_topk_sorted_columns, k=PEER_TOPK))

    @pl.when(jnp.max(tie) > 0.0)
    def _():
        key_id = lax.broadcasted_iota(jnp.int32, (nk, tm), 0).astype(F32)
        route(lambda s: _topk_rows(s, key_id, PEER_TOPK, nk) + (jnp.zeros((1, tm), F32),))


def peer_select(h2, wq, k1, k2, tm=256):
    N, D = h2.shape
    tm = min(tm, N)
    flat, neg = _cand_constants(tm)
    full = lambda a: pl.BlockSpec(a.shape, lambda i: (0,) * a.ndim)
    out = jax.ShapeDtypeStruct((N, PEER_HEADS * PEER_TOPK), F32)
    ospec = pl.BlockSpec((tm, PEER_HEADS * PEER_TOPK), lambda i: (i, 0))
    return pl.pallas_call(
        _peer_select_kernel,
        grid=(N // tm,),
        in_specs=[pl.BlockSpec((tm, D), lambda i: (i, 0)), full(wq), full(k1), full(k2), full(flat), full(neg)],
        out_specs=[ospec] * 3, out_shape=[out] * 3,
        compiler_params=_cparams("parallel"),
        name="peer_select",
    )(h2, wq, k1, k2, flat, neg)


PEER_BUILD_GROUP = 16
PEER_BUILD_UNROLL = 4


def _gelu(x):
    return 0.5 * x * (1.0 + lax.erf(x * 0.7071067811865476))


def _peer_expert_kernel(*refs, final, n_e):
    if final:
        h_ref, a_ref, b_ref, g_ref, u_ref, v_ref, x_ref, g2_ref, fg_ref, o_ref, w3_ref, acc_ref, ha_ref, hb_ref = refs
    else:
        h_ref, a_ref, b_ref, g_ref, u_ref, v_ref, x_ref, g2_ref, o_ref, w3_ref, acc_ref, ha_ref, hb_ref = refs
    j = pl.program_id(1)
    tm = h_ref.shape[0]
    te = u_ref.shape[0]
    nk = PEER_KEYS
    n_a = te // nk

    def hidden():
        return _dot_nt(h_ref[...], u_ref[...])

    def finish(hid_ref, tile):
        acts = []
        for al in range(n_a):
            w_a = w3_ref[tile * n_a + al].astype(F32)
            acts.append((_gelu(hid_ref[:, al * nk:(al + 1) * nk]) * w_a).astype(BF16))
        acc_ref[...] += jnp.dot(jnp.concatenate(acts, axis=1), v_ref[...], preferred_element_type=F32)

    @pl.when(j == 0)
    def _():
        acc_ref[...] = jnp.zeros(acc_ref.shape, F32)
        ha_ref[...] = hidden()
        key_id = lax.broadcasted_iota(jnp.int32, (nk, a_ref.shape[1]), 0).astype(F32).astype(BF16)
        zero = jnp.zeros((), BF16)
        one = jnp.ones((), BF16)

        def tokens(tb, carry):
            for grp in range(PEER_BUILD_UNROLL):
                base = pl.multiple_of((tb * PEER_BUILD_UNROLL + grp) * PEER_BUILD_GROUP, PEER_BUILD_GROUP)
                a8 = a_ref[pl.ds(base, PEER_BUILD_GROUP), :].astype(BF16)
                b8 = b_ref[pl.ds(base, PEER_BUILD_GROUP), :].astype(BF16)
                g8 = g_ref[pl.ds(base, PEER_BUILD_GROUP), :]
                g8_hi = g8.astype(BF16)
                g8_lo = (g8 - g8_hi.astype(F32)).astype(BF16)
                ws = []
                for r in range(PEER_BUILD_GROUP):
                    eq_a = key_id == a8[r:r + 1]
                    x = jnp.concatenate([jnp.where(eq_a, g8_hi[r:r + 1], zero),
                                         jnp.where(eq_a, g8_lo[r:r + 1], zero)], axis=1)
                    y1 = jnp.where(key_id == b8[r:r + 1], one, zero)
                    y = jnp.concatenate([y1, y1], axis=1)
                    ws.append(_dot_nt(x, y))
                w3_ref[:, pl.ds(base, PEER_BUILD_GROUP), :] = jnp.swapaxes(
                    jnp.stack(ws, axis=0).astype(BF16), 0, 1)
            return carry
        lax.fori_loop(0, tm // (PEER_BUILD_GROUP * PEER_BUILD_UNROLL), tokens, 0)

    for parity, (h_write, h_read) in enumerate(((ha_ref, hb_ref), (hb_ref, ha_ref))):
        @pl.when((j > 0) & (j < n_e) & (j % 2 == parity))
        def _(h_write=h_write, h_read=h_read):
            h_write[...] = hidden()
            finish(h_read, j - 1)

    @pl.when(j == n_e)
    def _():
        finish(hb_ref if (n_e - 1) % 2 else ha_ref, n_e - 1)
        y = x_ref[...] + g2_ref[0] * acc_ref[...]
        if final:
            ms = jnp.mean(y * y, axis=-1, keepdims=True)
            y = y * lax.rsqrt(ms + RMS_EPS) * fg_ref[...]
        o_ref[...] = y


def peer_experts(h2, a_idx, b_idx, gw, u, v, x1, g2, tokens_per_batch, final_g=None, tm=512, te=1024):
    N, D = x1.shape
    E = u.shape[0]
    tm = min(tm, N)
    nb = g2.shape[0]
    n_e = E // te
    tok = lambda n: pl.BlockSpec((tm, n), lambda i, j: (i, 0))
    in_specs = [tok(D), tok(a_idx.shape[1]), tok(a_idx.shape[1]), tok(a_idx.shape[1]),
                pl.BlockSpec((te, D), lambda i, j: (jnp.minimum(j, n_e - 1), 0)),
                pl.BlockSpec((te, D), lambda i, j: (jnp.maximum(j - 1, 0), 0)),
                tok(D), pl.BlockSpec((1, 1, D), lambda i, j: (i * tm // tokens_per_batch, 0, 0))]
    args = [h2, a_idx, b_idx, gw, u, v, x1, g2.reshape(nb, 1, D)]
    if final_g is not None:
        in_specs.append(pl.BlockSpec((1, D), lambda i, j: (0, 0)))
        args.append(final_g.reshape(1, D))
    return pl.pallas_call(
        functools.partial(_peer_expert_kernel, final=final_g is not None, n_e=n_e),
        grid=(N // tm, n_e + 1),
        in_specs=in_specs,
        out_specs=tok(D),
        out_shape=jax.ShapeDtypeStruct((N, D), F32),
        scratch_shapes=[pltpu.VMEM((PEER_KEYS, tm, PEER_KEYS), BF16), pltpu.VMEM((tm, D), F32),
                        pltpu.VMEM((tm, te), F32), pltpu.VMEM((tm, te), F32)],
        compiler_params=_cparams("parallel", "arbitrary"),
        name="peer_experts",
    )(*args)


def peer_layer(x1, h2, wq, k1, k2, u, v, g2, final_g=None):
    B, T, D = x1.shape
    h2f = h2.reshape(B * T, D)
    a_idx, b_idx, gw = peer_select(h2f, wq, k1, k2)
    out = peer_experts(h2f, a_idx, b_idx, gw, u, v, x1.reshape(B * T, D), g2, T, final_g)
    return out.reshape(B, T, D)


def kernel(x, c, ada_w, ada_b, norm1_g, norm2_g, w_in, conv_dw_w, conv_dw_b, conv_gn_g, conv_gn_b,
           rwkv_mu, rwkv_w0, rwkv_w_up, rwkv_a0, rwkv_a_up, rwkv_g_up, rwkv_k_k, rwkv_k_a, rwkv_r_k,
           rwkv_ln_g, rwkv_ln_b, nsa_ck_pos, nsa_ck_w1, nsa_ck_w2, nsa_cv_pos, nsa_cv_w1, nsa_cv_w2,
           w_out, peer_wq, peer_k1, peer_k2, peer_u, peer_v, final_g):
    depth, D = norm1_g.shape
    d_conv = conv_dw_b.shape[1]
    d_rwkv = rwkv_w0.shape[1]
    n_conv = 2 * d_conv
    n_rwkv = rwkv_mu.shape[1]
    d_nsa = D - d_conv - d_rwkv
    n_kv = 6 * NSA_KV_HEADS * HEAD_DIM
    n_gate = 3 * NSA_KV_HEADS * NSA_GROUP
    cuts = np.cumsum([0, n_conv, n_rwkv, d_nsa, n_kv, n_gate])
    mod = ada_mod(c, ada_w, ada_b)
    for l in range(depth):
        sh1, sc1, g1, sh2, sc2, g2 = [mod[l, :, i * D:(i + 1) * D] for i in range(6)]
        w_l = w_in[l].astype(BF16)
        pieces = [w_l[:, cuts[i]:cuts[i + 1]] for i in range(5)]
        pieces[4] = jnp.pad(pieces[4], ((0, 0), (0, LANES - n_gate)))
        p_conv, p_rwkv, p_q, p_kv, p_gate = proj_in(x, sh1, sc1, norm1_g[l], pieces)
        o_conv = conv_mixer(p_conv, conv_dw_w[l], conv_dw_b[l], conv_gn_g[l], conv_gn_b[l])
        o_rwkv = rwkv_mixer(p_rwkv, rwkv_mu[l], rwkv_w0[l], rwkv_w_up[l], rwkv_a0[l], rwkv_a_up[l],
                            rwkv_g_up[l], rwkv_k_k[l], rwkv_k_a[l], rwkv_r_k[l], rwkv_ln_g[l], rwkv_ln_b[l])
        o_nsa = nsa_mixer(p_q, p_kv, p_gate, nsa_ck_pos[l], nsa_ck_w1[l], nsa_ck_w2[l],
                          nsa_cv_pos[l], nsa_cv_w1[l], nsa_cv_w2[l])
        wo = w_out[l].astype(BF16)
        wo_pieces = [wo[:d_conv], wo[d_conv:d_conv + d_rwkv], wo[d_conv + d_rwkv:]]
        x1, h2 = proj_out(x, g1, sh2, sc2, norm2_g[l], [o_conv, o_rwkv, o_nsa], wo_pieces)
        x = peer_layer(x1, h2, peer_wq[l].astype(BF16), peer_k1[l], peer_k2[l],
                       peer_u[l].astype(BF16), peer_v[l].astype(BF16), g2,
                       final_g if l == depth - 1 else None)
    return x
```
